```python
import math
import jax
import jax.numpy as jnp
from jax import lax
import numpy as np

D_MODEL = 2048
BATCH = 2
SEQ = 4096
DEPTH = 4
DEC_BATCH = 8
DEC_SEQ = 4
PAST_LEN = 16384
PAGE_SIZE = 128

HEAD_DIM = 128
NSA_HEADS = 8
NSA_KV_HEADS = 2
NSA_HPG = NSA_HEADS // NSA_KV_HEADS
NSA_BLOCK = 64
NSA_N_SEL = 16
NSA_WINDOW = 512
NSA_FORCE = 1.0e4
SLC_Q_BLOCK = 64
DSA_HEADS = 8
DSA_KV_HEADS = 2
DSA_HPG = DSA_HEADS // DSA_KV_HEADS
IDX_HEADS = 8
IDX_DIM = 64
DSA_TOPK = 256
Q_BLOCK = 128
GDN_HEADS = 8
GDN_DK = 128
GDN_DV = 128
GDN_CHUNK = 64
CONV_WIDTH = 4
SSD_D_INNER = D_MODEL // 2
SSD_HEAD_DIM = 64
SSD_HEADS = SSD_D_INNER // SSD_HEAD_DIM
SSD_GROUPS = 2
SSD_STATE = 128
SSD_CHUNK = 64
D_FF = 5632
NORM_EPS = 1e-6
N_ATTN_LAYERS = (DEPTH + 1) // 2
N_REC_LAYERS = DEPTH // 2

EVEN_SIZES = (NSA_HEADS * HEAD_DIM, 6 * NSA_KV_HEADS * HEAD_DIM, 3 * NSA_HEADS,
              DSA_HEADS * HEAD_DIM, 2 * DSA_KV_HEADS * HEAD_DIM,
              IDX_HEADS * IDX_DIM, IDX_DIM, IDX_HEADS)
EVEN_IN = sum(EVEN_SIZES)
ATTN_OUT = (NSA_HEADS + DSA_HEADS) * HEAD_DIM
GDN_CONV_DIM = GDN_HEADS * (2 * GDN_DK + GDN_DV)
SSD_CONV_DIM = SSD_D_INNER + 2 * SSD_GROUPS * SSD_STATE
ODD_SIZES = (GDN_CONV_DIM, GDN_HEADS, GDN_HEADS, GDN_HEADS * GDN_DV,
             SSD_D_INNER, SSD_CONV_DIM, SSD_HEADS)
ODD_IN = sum(ODD_SIZES)
REC_OUT = GDN_HEADS * GDN_DV + SSD_D_INNER

kernel_name = 'hybrid_nsa_dsa_gdn_ssd_macaron_step'


def _split(u, sizes):
    return jnp.split(u, np.cumsum(sizes)[:-1].tolist(), axis=-1)


def rmsnorm(x, g):
    xf = x.astype(jnp.float32)
    y = xf * lax.rsqrt(jnp.mean(xf * xf, axis=-1, keepdims=True) + NORM_EPS)
    return (y * g.astype(jnp.float32)).astype(x.dtype)


def l2norm(x):
    xf = x.astype(jnp.float32)
    return xf * lax.rsqrt(jnp.sum(xf * xf, axis=-1, keepdims=True) + 1e-6)


def masked_softmax(s, mask):
    s = jnp.where(mask, s.astype(jnp.float32), -jnp.inf)
    m = jnp.max(s, axis=-1, keepdims=True)
    e = jnp.exp(s - jnp.where(jnp.isfinite(m), m, 0.0))
    return e / jnp.maximum(jnp.sum(e, axis=-1, keepdims=True), 1e-30)


def half_ffn(x, g, wg, wu, wd):
    h = rmsnorm(x, g)
    return 0.5 * ((jax.nn.silu(h @ wg) * (h @ wu)) @ wd)


def causal_conv(buf, u, w, b=None):
    xp = jnp.concatenate([buf.astype(u.dtype), u], axis=1)
    y = lax.conv_general_dilated(xp, w[:, None, :].astype(u.dtype), window_strides=(1,), padding='VALID',
                                 dimension_numbers=('NWC', 'WIO', 'NWC'), feature_group_count=u.shape[-1])
    if b is not None:
        y = y + b
    return y, xp[:, xp.shape[1] - (CONV_WIDTH - 1):]


def gather_pages(pool, page_table):
    g = pool[page_table]
    return g.reshape((g.shape[0], g.shape[1] * g.shape[2]) + g.shape[3:])


def _blocks(x, n, axis):
    shp = x.shape
    x = x.reshape(shp[:axis] + (n, shp[axis] // n) + shp[axis + 1:])
    return jnp.moveaxis(x, axis, 0)


def nsa_compressed_selected(q, kv, q_pos, cmp_w):
    B, Tq = q.shape[:2]
    L = kv.shape[1]
    nb = -(-L // NSA_BLOCK)
    kv = jnp.pad(kv, ((0, 0), (0, nb * NSA_BLOCK - L), (0, 0), (0, 0), (0, 0)))
    kvb = kv.reshape(B, nb, NSA_BLOCK, 4, NSA_KV_HEADS, HEAD_DIM)
    scale = HEAD_DIM ** -0.5
    k_c = jnp.einsum('bnjgd,jd->bngd', kvb[:, :, :, 0], cmp_w[0].astype(kv.dtype))
    v_c = jnp.einsum('bnjgd,jd->bngd', kvb[:, :, :, 1], cmp_w[1].astype(kv.dtype))
    s = jnp.einsum('btghd,bngd->bghtn', q, k_c) * scale
    blk = jnp.arange(nb)
    vis = (blk[None, :] + 1) * NSA_BLOCK - 1 <= q_pos[:, None]
    p = masked_softmax(s, vis)
    o_cmp = jnp.einsum('bghtn,bngd->btghd', p.astype(q.dtype), v_c)
    imp = jnp.sum(p, axis=2)
    cur = (q_pos // NSA_BLOCK)[:, None]
    forced = (blk == 0) | (blk == cur) | (blk == cur - 1)
    score = jnp.where(blk > cur, -jnp.inf, jnp.where(forced, NSA_FORCE, imp))
    n_sel = min(NSA_N_SEL, nb)
    _, sel = lax.top_k(score, n_sel)
    k_s = jnp.moveaxis(kvb[:, :, :, 2], 3, 1)
    v_s = jnp.moveaxis(kvb[:, :, :, 3], 3, 1)
    qb = min(SLC_Q_BLOCK, Tq)
    n_qb = Tq // qb
    n_keys = n_sel * NSA_BLOCK
    gather = jax.vmap(jax.vmap(lambda src, idx: src[idx]))

    def slc_block(args):
        q_b, sel_b, pos_b = args
        kg = gather(k_s, sel_b).reshape(B, NSA_KV_HEADS, qb, n_keys, HEAD_DIM)
        vg = gather(v_s, sel_b).reshape(B, NSA_KV_HEADS, qb, n_keys, HEAD_DIM)
        kpos = (sel_b[..., None] * NSA_BLOCK + jnp.arange(NSA_BLOCK)).reshape(B, NSA_KV_HEADS, qb, n_keys)
        sb = jnp.einsum('btghd,bgtkd->bghtk', q_b, kg) * scale
        pb = masked_softmax(sb, (kpos <= pos_b[:, None])[:, :, None])
        return jnp.einsum('bghtk,bgtkd->btghd', pb.astype(q_b.dtype), vg)

    o_slc = lax.map(slc_block, (_blocks(q, n_qb, 1), _blocks(sel, n_qb, 2), q_pos.reshape(n_qb, qb)))
    o_slc = jnp.moveaxis(o_slc, 0, 1).reshape(B, Tq, NSA_KV_HEADS, NSA_HPG, HEAD_DIM)
    return o_cmp, o_slc


def nsa_window_banded(q, kw, vw):
    B, T = q.shape[:2]
    qb = min(Q_BLOCK, T)
    n = T // qb
    span = NSA_WINDOW + qb
    pad = ((0, 0), (NSA_WINDOW, 0), (0, 0), (0, 0))
    idx = jnp.arange(n)[:, None] * qb + jnp.arange(span)[None, :]
    kb = jnp.pad(kw, pad)[:, idx]
    vb = jnp.pad(vw, pad)[:, idx]
    qr = q.reshape(B, n, qb, NSA_KV_HEADS, NSA_HPG, HEAD_DIM)
    s = jnp.einsum('bnqghd,bnkgd->bnghqk', qr, kb) * HEAD_DIM ** -0.5
    qpos = (jnp.arange(n)[:, None] * qb + jnp.arange(qb)[None, :])[:, :, None]
    kpos = (idx - NSA_WINDOW)[:, None, :]
    m = (kpos <= qpos) & (kpos >= qpos - NSA_WINDOW) & (kpos >= 0)
    p = masked_softmax(s, m[None, :, None, None])
    o = jnp.einsum('bnghqk,bnkgd->bnqghd', p.astype(q.dtype), vb)
    return o.reshape(B, T, NSA_KV_HEADS, NSA_HPG, HEAD_DIM)


def nsa_window_cached(q, buf, rows, q_pos):
    wk = buf.shape[1]
    kv = jnp.concatenate([buf, rows], axis=1)
    kpos = q_pos[0] - wk + jnp.arange(wk + q.shape[1])
    s = jnp.einsum('btghd,bkgd->bghtk', q, kv[:, :, 0]) * HEAD_DIM ** -0.5
    m = (kpos[None, :] <= q_pos[:, None]) & (kpos[None, :] >= q_pos[:, None] - NSA_WINDOW)
    p = masked_softmax(s, m)
    return jnp.einsum('bghtk,bkgd->btghd', p.astype(q.dtype), kv[:, :, 1])


def dsa_attend(q, kv, qi, wi, ki, q_pos):
    B, Tq = q.shape[:2]
    L = kv.shape[1]
    n_keep = min(DSA_TOPK, L // 4)
    qb = min(Q_BLOCK, Tq)
    n = Tq // qb
    kpos = jnp.arange(L)
    ki32 = ki.astype(jnp.float32)
    gather = jax.vmap(lambda src, idx: src[idx])

    def block(args):
        q_b, qi_b, wi_b, pos_b = args
        rel = jax.nn.relu(jnp.einsum('bthd,bsd->bths', qi_b.astype(jnp.float32), ki32) * IDX_DIM ** -0.5)
        idx_score = jnp.einsum('bths,bth->bts', rel, wi_b.astype(jnp.float32) * IDX_HEADS ** -0.5)
        idx_score = jnp.where(kpos[None, None, :] <= pos_b[None, :, None], idx_score, -jnp.inf)
        _, sel = lax.top_k(idx_score, n_keep)
        kvg = gather(kv, sel)
        s = jnp.einsum('btghd,btkgd->bghtk', q_b, kvg[:, :, :, 0]) * HEAD_DIM ** -0.5
        p = masked_softmax(s, (sel <= pos_b[None, :, None])[:, None, None])
        return jnp.einsum('bghtk,btkgd->btghd', p.astype(q_b.dtype), kvg[:, :, :, 1])

    o = lax.map(block, (_blocks(q, n, 1), _blocks(qi, n, 1), _blocks(wi, n, 1), q_pos.reshape(n, qb)))
    return jnp.moveaxis(o, 0, 1).reshape(B, Tq, DSA_KV_HEADS, DSA_HPG, HEAD_DIM)


def even_mixer(h, w_in, w_out, cmp_w, past):
    B, T, _ = h.shape
    qa, kva, ga, qb, kvb, qi, ki, wi = _split(h @ w_in, EVEN_SIZES)
    qa = qa.reshape(B, T, NSA_KV_HEADS, NSA_HPG, HEAD_DIM)
    kva = kva.reshape(B, T, 6, NSA_KV_HEADS, HEAD_DIM)
    ga = jax.nn.sigmoid(ga.reshape(B, T, NSA_KV_HEADS, NSA_HPG, 3))
    qb = qb.reshape(B, T, DSA_KV_HEADS, DSA_HPG, HEAD_DIM)
    kvb = kvb.reshape(B, T, 2, DSA_KV_HEADS, HEAD_DIM)
    qi = qi.reshape(B, T, IDX_HEADS, IDX_DIM)
    nsa_rows, win_rows = kva[:, :, :4], kva[:, :, 4:]
    if past is None:
        q_pos = jnp.arange(T, dtype=jnp.int32)
        nsa_kv, dsa_kv, idx_k = nsa_rows, kvb, ki
        o_win = nsa_window_banded(qa, win_rows[:, :, 0], win_rows[:, :, 1])
        new_win = win_rows[:, T - min(NSA_WINDOW, T):]
    else:
        nsa_past, dsa_past, idx_past, win_buf = past
        q_pos = nsa_past.shape[1] + jnp.arange(T, dtype=jnp.int32)
        nsa_kv = jnp.concatenate([nsa_past.astype(h.dtype), nsa_rows], axis=1)
        dsa_kv = jnp.concatenate([dsa_past.astype(h.dtype), kvb], axis=1)
        idx_k = jnp.concatenate([idx_past.astype(h.dtype), ki], axis=1)
        win_buf = win_buf.astype(h.dtype)
        o_win = nsa_window_cached(qa, win_buf, win_rows, q_pos)
        new_win = jnp.concatenate([win_buf, win_rows], axis=1)[:, T:]
    o_cmp, o_slc = nsa_compressed_selected(qa, nsa_kv, q_pos, cmp_w)
    o_a = ga[..., 0:1] * o_cmp + ga[..., 1:2] * o_slc + ga[..., 2:3] * o_win
    o_b = dsa_attend(qb, dsa_kv, qi, wi, idx_k, q_pos)
    o = jnp.concatenate([o_a.reshape(B, T, -1), o_b.reshape(B, T, -1)], axis=-1) @ w_out
    return o, (nsa_rows, kvb, ki, new_win)


def gated_delta_rule(q, k, v, g, beta, S0):
    f32 = jnp.float32
    B, T, H, DK = q.shape
    DV = v.shape[-1]
    C = min(GDN_CHUNK, T)
    nC = -(-T // C)
    pad = nC * C - T

    def prep(x):
        x = jnp.pad(x.astype(f32), ((0, 0), (0, pad)) + ((0, 0),) * (x.ndim - 2))
        return jnp.moveaxis(x.reshape((B, nC, C) + x.shape[2:]), 3, 1)

    q, k, v, g, beta = prep(q), prep(k), prep(v), prep(g), prep(beta)
    gc = jnp.cumsum(g, axis=-1)
    causal = jnp.tril(jnp.ones((C, C), bool))
    strict = jnp.tril(jnp.ones((C, C), bool), -1)
    gam = jnp.exp(jnp.where(causal, gc[..., :, None] - gc[..., None, :], -jnp.inf))
    kb = k * beta[..., None]
    A = jnp.where(strict, jnp.einsum('bhnid,bhnjd->bhnij', kb, k) * gam, 0.0)
    eye = jnp.eye(C, dtype=f32)
    Tm = lax.linalg.triangular_solve(A + eye, jnp.broadcast_to(eye, A.shape), left_side=True,
                                     lower=True, unit_diagonal=True)
    U0 = Tm @ (v * beta[..., None])
    Wd = Tm @ (kb * jnp.exp(gc)[..., None])
    QK = jnp.einsum('bhnid,bhnjd->bhnij', q, k) * gam
    k_dec = k * jnp.exp(gc[..., -1:] - gc)[..., None]

    def step(S, xs):
        q_c, gc_c, U0_c, W_c, QK_c, kd_c = xs
        U = U0_c - W_c @ S
        o = (q_c * jnp.exp(gc_c)[..., None]) @ S + QK_c @ U
        S = S * jnp.exp(gc_c[..., -1])[..., None, None] + jnp.swapaxes(kd_c, -1, -2) @ U
        return S, o

    xs = tuple(jnp.moveaxis(t, 2, 0) for t in (q, gc, U0, Wd, QK, k_dec))
    S, o = lax.scan(step, S0.astype(f32), xs)
    o = jnp.moveaxis(jnp.moveaxis(o, 0, 2), 1, 3).reshape(B, nC * C, H, DV)[:, :T]
    return o, S


def ssd_chunked(x, dt, A, Bm, Cm, h0):
    f32 = jnp.float32
    B, T, H, P = x.shape
    G, N = Bm.shape[2:]
    hpg = H // G
    C = min(SSD_CHUNK, T)
    nC = -(-T // C)
    pad = nC * C - T

    def padt(t):
        return jnp.pad(t.astype(f32), ((0, 0), (0, pad)) + ((0, 0),) * (t.ndim - 2))

    x = padt(x).reshape(B, nC, C, G, hpg, P)
    dt = padt(dt).reshape(B, nC, C, G, hpg)
    Bm = padt(Bm).reshape(B, nC, C, G, N)
    Cm = padt(Cm).reshape(B, nC, C, G, N)
    acs = jnp.cumsum(jnp.moveaxis(dt * A.reshape(G, hpg).astype(f32), 2, -1), axis=-1)
    causal = jnp.tril(jnp.ones((C, C), bool))
    Lm = jnp.exp(jnp.where(causal, acs[..., :, None] - acs[..., None, :], -jnp.inf))
    xdt = x * dt[..., None]
    CB = jnp.einsum('bcign,bcjgn->bcgij', Cm, Bm)
    y_diag = jnp.einsum('bcgij,bcghij,bcjghp->bcighp', CB, Lm, xdt)
    states = jnp.einsum('bcjgn,bcghj,bcjghp->bcghpn', Bm, jnp.exp(acs[..., -1:] - acs), xdt)
    chunk_decay = jnp.exp(acs[..., -1])

    def step(hs, inp):
        st, cd = inp
        return hs * cd[..., None, None] + st, hs

    hN, h_prev = lax.scan(step, h0.astype(f32).reshape(B, G, hpg, P, N),
                          (jnp.moveaxis(states, 1, 0), jnp.moveaxis(chunk_decay, 1, 0)))
    h_prev = jnp.moveaxis(h_prev, 0, 1)
    y_off = jnp.einsum('bcign,bcghpn,bcghi->bcighp', Cm, h_prev, jnp.exp(acs))
    y = (y_diag + y_off).reshape(B, nC * C, H, P)[:, :T]
    return y, hN.reshape(B, H, P, N)


def odd_mixer(h, W, i, init):
    B, T, _ = h.shape
    f32 = jnp.float32
    S0, conv_c0, h0, conv_d0 = init
    qkv, beta, a, zc, zd, xbc, dt = _split(h @ W['rec_w_in'][i], ODD_SIZES)
    qkv, conv_c = causal_conv(conv_c0, qkv, W['gdn_conv_w'][i])
    q, k, v = _split(jax.nn.silu(qkv), (GDN_HEADS * GDN_DK, GDN_HEADS * GDN_DK, GDN_HEADS * GDN_DV))
    q = l2norm(q.reshape(B, T, GDN_HEADS, GDN_DK)) * GDN_DK ** -0.5
    k = l2norm(k.reshape(B, T, GDN_HEADS, GDN_DK))
    v = v.reshape(B, T, GDN_HEADS, GDN_DV)
    beta = jax.nn.sigmoid(beta.astype(f32))
    g = -jnp.exp(W['gdn_a_log'][i].astype(f32)) * jax.nn.softplus(a.astype(f32) + W['gdn_dt_bias'][i])
    o_c, S = gated_delta_rule(q, k, v, g, beta, S0)
    o_c = rmsnorm(o_c, W['gdn_norm_g'][i]) * jax.nn.silu(zc.reshape(B, T, GDN_HEADS, GDN_DV).astype(f32))
    xbc, conv_d = causal_conv(conv_d0, xbc, W['ssd_conv_w'][i], W['ssd_conv_b'][i])
    xs, Bm, Cm = _split(jax.nn.silu(xbc), (SSD_D_INNER, SSD_GROUPS * SSD_STATE, SSD_GROUPS * SSD_STATE))
    xs = xs.reshape(B, T, SSD_HEADS, SSD_HEAD_DIM)
    dtp = jax.nn.softplus(dt.astype(f32) + W['ssd_dt_bias'][i])
    A = -jnp.exp(W['ssd_a_log'][i].astype(f32))
    y, hN = ssd_chunked(xs, dtp, A, Bm.reshape(B, T, SSD_GROUPS, SSD_STATE),
                        Cm.reshape(B, T, SSD_GROUPS, SSD_STATE), h0)
    y = y + W['ssd_d'][i][:, None] * xs
    y = y.reshape(B, T, SSD_D_INNER) * jax.nn.silu(zd.astype(f32))
    y = rmsnorm(y.reshape(B, T, SSD_GROUPS, -1), W['ssd_norm_g'][i].reshape(SSD_GROUPS, -1))
    o_cat = jnp.concatenate([o_c.reshape(B, T, -1), y.reshape(B, T, -1)], axis=-1).astype(h.dtype)
    return o_cat @ W['rec_w_out'][i], (S, conv_c, hN, conv_d)


def trunk(x, W, cache):
    B = x.shape[0]
    attn_new, rec_new = [], []
    for l in range(DEPTH):
        i = l // 2
        x = x + half_ffn(x, W['norm_g'][l, 0], W['ffn_w_gate'][l, 0], W['ffn_w_up'][l, 0], W['ffn_w_down'][l, 0])
        h = rmsnorm(x, W['norm_g'][l, 1])
        if l % 2 == 0:
            past = None
            if cache is not None:
                pt = cache['page_table']
                past = (gather_pages(cache['nsa_kv'][i], pt), gather_pages(cache['dsa_kv'][i], pt),
                        gather_pages(cache['dsa_idx_k'][i], pt), cache['nsa_win'][i])
            o, st = even_mixer(h, W['attn_w_in'][i], W['attn_w_out'][i], W['nsa_cmp_w'][i], past)
            attn_new.append(st)
        else:
            if cache is None:
                init = (jnp.zeros((B, GDN_HEADS, GDN_DK, GDN_DV), x.dtype),
                        jnp.zeros((B, CONV_WIDTH - 1, GDN_CONV_DIM), x.dtype),
                        jnp.zeros((B, SSD_HEADS, SSD_HEAD_DIM, SSD_STATE), x.dtype),
                        jnp.zeros((B, CONV_WIDTH - 1, SSD_CONV_DIM), x.dtype))
            else:
                init = (cache['gdn'][i], cache['gdn_conv'][i], cache['ssd'][i], cache['ssd_conv'][i])
            o, st = odd_mixer(h, W, i, init)
            rec_new.append(st)
        x = x + o
        x = x + half_ffn(x, W['norm_g'][l, 2], W['ffn_w_gate'][l, 1], W['ffn_w_up'][l, 1], W['ffn_w_down'][l, 1])
    y = rmsnorm(x, W['final_norm_g'])

    def stack(lst, j):
        return jnp.stack([s[j] for s in lst])

    return (y, stack(attn_new, 0), stack(attn_new, 1), stack(attn_new, 2), stack(attn_new, 3),
            stack(rec_new, 0), stack(rec_new, 1), stack(rec_new, 2), stack(rec_new, 3))


def setup_inputs(seed: int = 0) -> dict:
    key = jax.random.key(seed)
    ks = iter(jax.random.split(key, 48))
    f32 = jnp.float32

    def nrm(shape, scale):
        return jax.random.normal(next(ks), shape, f32) * scale

    def dt_bias(shape):
        dt = jnp.exp(jax.random.uniform(next(ks), shape, f32, math.log(1e-3), math.log(1e-1)))
        return dt + jnp.log(-jnp.expm1(-dt))

    n_pages = PAST_LEN // PAGE_SIZE
    n_pool = (5 * DEC_BATCH * n_pages) // 4
    win_keep = min(NSA_WINDOW, PAST_LEN)
    na, nr = N_ATTN_LAYERS, N_REC_LAYERS
    perm = jax.random.permutation(next(ks), n_pool)
    page_table = perm[:DEC_BATCH * n_pages].reshape(DEC_BATCH, n_pages).astype(jnp.int32)
    return {
        'x_prompt': nrm((BATCH, SEQ, D_MODEL), 1.0),
        'x_sample': nrm((DEC_BATCH, DEC_SEQ, D_MODEL), 1.0),
        'cache_nsa_kv': nrm((na, n_pool, PAGE_SIZE, 4, NSA_KV_HEADS, HEAD_DIM), 1.0),
        'cache_dsa_kv': nrm((na, n_pool, PAGE_SIZE, 2, DSA_KV_HEADS, HEAD_DIM), 1.0),
        'cache_dsa_idx_k': nrm((na, n_pool, PAGE_SIZE, IDX_DIM), 1.0),
        'page_table': page_table,
        'state_nsa_win': nrm((na, DEC_BATCH, win_keep, 2, NSA_KV_HEADS, HEAD_DIM), 1.0),
        'state_gdn': nrm((nr, DEC_BATCH, GDN_HEADS, GDN_DK, GDN_DV), 0.1),
        'state_gdn_conv': nrm((nr, DEC_BATCH, CONV_WIDTH - 1, GDN_CONV_DIM), 1.0),
        'state_ssd': nrm((nr, DEC_BATCH, SSD_HEADS, SSD_HEAD_DIM, SSD_STATE), 0.05),
        'state_ssd_conv': nrm((nr, DEC_BATCH, CONV_WIDTH - 1, SSD_CONV_DIM), 1.0),
        'norm_g': 1.0 + nrm((DEPTH, 3, D_MODEL), 0.1),
        'final_norm_g': 1.0 + nrm((D_MODEL,), 0.1),
        'ffn_w_gate': nrm((DEPTH, 2, D_MODEL, D_FF), D_MODEL ** -0.5),
        'ffn_w_up': nrm((DEPTH, 2, D_MODEL, D_FF), D_MODEL ** -0.5),
        'ffn_w_down': nrm((DEPTH, 2, D_FF, D_MODEL), D_FF ** -0.5),
        'attn_w_in': nrm((na, D_MODEL, EVEN_IN), D_MODEL ** -0.5),
        'attn_w_out': nrm((na, ATTN_OUT, D_MODEL), ATTN_OUT ** -0.5),
        'nsa_cmp_w': (1.0 + nrm((na, 2, NSA_BLOCK, HEAD_DIM), 0.2)) / NSA_BLOCK,
        'rec_w_in': nrm((nr, D_MODEL, ODD_IN), D_MODEL ** -0.5),
        'rec_w_out': nrm((nr, REC_OUT, D_MODEL), REC_OUT ** -0.5),
        'gdn_conv_w': nrm((nr, CONV_WIDTH, GDN_CONV_DIM), CONV_WIDTH ** -0.5),
        'gdn_a_log': jnp.log(jax.random.uniform(next(ks), (nr, GDN_HEADS), f32, 1.0, 16.0)),
        'gdn_dt_bias': dt_bias((nr, GDN_HEADS)),
        'gdn_norm_g': 1.0 + nrm((nr, GDN_DV), 0.1),
        'ssd_conv_w': nrm((nr, CONV_WIDTH, SSD_CONV_DIM), CONV_WIDTH ** -0.5),
        'ssd_conv_b': nrm((nr, SSD_CONV_DIM), 0.01),
        'ssd_dt_bias': dt_bias((nr, SSD_HEADS)),
        'ssd_a_log': jnp.log(jax.random.uniform(next(ks), (nr, SSD_HEADS), f32, 1.0, 16.0)),
        'ssd_d': 1.0 + nrm((nr, SSD_HEADS), 0.1),
        'ssd_norm_g': 1.0 + nrm((nr, SSD_D_INNER), 0.1),
    }


def reference(x_prompt, x_sample, cache_nsa_kv, cache_dsa_kv, cache_dsa_idx_k, page_table, state_nsa_win,
              state_gdn, state_gdn_conv, state_ssd, state_ssd_conv, norm_g, final_norm_g, ffn_w_gate, ffn_w_up,
              ffn_w_down, attn_w_in, attn_w_out, nsa_cmp_w, rec_w_in, rec_w_out, gdn_conv_w, gdn_a_log,
              gdn_dt_bias, gdn_norm_g, ssd_conv_w, ssd_conv_b, ssd_dt_bias, ssd_a_log, ssd_d, ssd_norm_g):
    W = {'norm_g': norm_g, 'final_norm_g': final_norm_g, 'ffn_w_gate': ffn_w_gate, 'ffn_w_up': ffn_w_up,
         'ffn_w_down': ffn_w_down, 'attn_w_in': attn_w_in, 'attn_w_out': attn_w_out, 'nsa_cmp_w': nsa_cmp_w,
         'rec_w_in': rec_w_in, 'rec_w_out': rec_w_out, 'gdn_conv_w': gdn_conv_w, 'gdn_a_log': gdn_a_log,
         'gdn_dt_bias': gdn_dt_bias, 'gdn_norm_g': gdn_norm_g, 'ssd_conv_w': ssd_conv_w, 'ssd_conv_b': ssd_conv_b,
         'ssd_dt_bias': ssd_dt_bias, 'ssd_a_log': ssd_a_log, 'ssd_d': ssd_d, 'ssd_norm_g': ssd_norm_g}
    cache = {'nsa_kv': cache_nsa_kv, 'dsa_kv': cache_dsa_kv, 'dsa_idx_k': cache_dsa_idx_k,
             'page_table': page_table, 'nsa_win': state_nsa_win, 'gdn': state_gdn, 'gdn_conv': state_gdn_conv,
             'ssd': state_ssd, 'ssd_conv': state_ssd_conv}
    (y_prompt, p_nsa_kv, p_dsa_kv, p_dsa_idx_k, p_nsa_win,
     p_gdn, p_gdn_conv, p_ssd, p_ssd_conv) = trunk(x_prompt, W, None)
    (y_sample, s_nsa_kv, s_dsa_kv, s_dsa_idx_k, s_nsa_win,
     s_gdn, s_gdn_conv, s_ssd, s_ssd_conv) = trunk(x_sample, W, cache)
    return (y_prompt, y_sample, p_nsa_kv, p_dsa_kv, p_dsa_idx_k, p_nsa_win, p_gdn, p_gdn_conv, p_ssd, p_ssd_conv,
            s_nsa_kv, s_dsa_kv, s_dsa_idx_k, s_nsa_win, s_gdn, s_gdn_conv, s_ssd, s_ssd_conv)
```

```python
import functools
import math

import jax
import jax.numpy as jnp
import numpy as np
from jax import lax
from jax.experimental import pallas as pl
from jax.experimental.pallas import tpu as pltpu

D_MODEL = 2048
DEPTH = 4
PAGE_SIZE = 128
HEAD_DIM = 128
NSA_HEADS = 8
NSA_KV_HEADS = 2
NSA_HPG = NSA_HEADS // NSA_KV_HEADS
NSA_BLOCK = 64
NSA_N_SEL = 16
NSA_WINDOW = 512
NSA_FORCE = 1.0e4
SLC_Q_BLOCK = 64
DSA_HEADS = 8
DSA_KV_HEADS = 2
DSA_HPG = DSA_HEADS // DSA_KV_HEADS
IDX_HEADS = 8
IDX_DIM = 64
DSA_TOPK = 256
Q_BLOCK = 128
GDN_HEADS = 8
GDN_DK = 128
GDN_DV = 128
GDN_CHUNK = 64
CONV_WIDTH = 4
SSD_D_INNER = D_MODEL // 2
SSD_HEAD_DIM = 64
SSD_HEADS = SSD_D_INNER // SSD_HEAD_DIM
SSD_GROUPS = 2
SSD_STATE = 128
SSD_CHUNK = 64
D_FF = 5632
NORM_EPS = 1e-6

EVEN_SIZES = (NSA_HEADS * HEAD_DIM, 6 * NSA_KV_HEADS * HEAD_DIM, 3 * NSA_HEADS,
              DSA_HEADS * HEAD_DIM, 2 * DSA_KV_HEADS * HEAD_DIM,
              IDX_HEADS * IDX_DIM, IDX_DIM, IDX_HEADS)
EVEN_IN = sum(EVEN_SIZES)
GDN_CONV_DIM = GDN_HEADS * (2 * GDN_DK + GDN_DV)
SSD_CONV_DIM = SSD_D_INNER + 2 * SSD_GROUPS * SSD_STATE
ODD_SIZES = (GDN_CONV_DIM, GDN_HEADS, GDN_HEADS, GDN_HEADS * GDN_DV,
             SSD_D_INNER, SSD_CONV_DIM, SSD_HEADS)
ODD_IN = sum(ODD_SIZES)

V7X_VMEM_BYTES = 64 * 1024 * 1024
VMEM_LIMIT = V7X_VMEM_BYTES - 12 * 1024 * 1024
PROJ_TN = 1024
FFN_TF = 512

bf16 = jnp.bfloat16
f32 = jnp.float32


def _round_up(n, m):
    return -(-n // m) * m


def _row_tile(m):
    return 512 if m % 512 == 0 else m


def _rms(x, g):
    return x * lax.rsqrt(jnp.mean(x * x, axis=-1, keepdims=True) + NORM_EPS) * g


def _ffn_kernel(x_ref, g_ref, wg_ref, wu_ref, wd_ref, fg_ref, o_ref, h_scr, acc_scr, *, final):
    f = pl.program_id(1)

    @pl.when(f == 0)
    def _():
        h_scr[...] = _rms(x_ref[...], g_ref[...]).astype(bf16)
        acc_scr[...] = jnp.zeros_like(acc_scr)

    h = h_scr[...]
    gate = jnp.dot(h, wg_ref[...], preferred_element_type=f32)
    up = jnp.dot(h, wu_ref[...], preferred_element_type=f32)
    act = (gate * jax.nn.sigmoid(gate) * up).astype(bf16)
    acc_scr[...] += jnp.dot(act, wd_ref[...], preferred_element_type=f32)

    @pl.when(f == pl.num_programs(1) - 1)
    def _():
        y = x_ref[...] + 0.5 * acc_scr[...]
        if final:
            y = _rms(y, fg_ref[...])
        o_ref[...] = y


def _ffn(x, g, wg, wu, wd, l, j, final_g=None):
    m, d = x.shape
    tm = _row_tile(m)
    final = final_g is not None
    fg = final_g if final else g
    return pl.pallas_call(
        functools.partial(_ffn_kernel, final=final),
        grid=(m // tm, D_FF // FFN_TF),
        in_specs=[
            pl.BlockSpec((tm, d), lambda i, f: (i, 0)),
            pl.BlockSpec((1, d), lambda i, f: (0, 0)),
            pl.BlockSpec((None, None, d, FFN_TF), lambda i, f: (l, j, 0, f)),
            pl.BlockSpec((None, None, d, FFN_TF), lambda i, f: (l, j, 0, f)),
            pl.BlockSpec((None, None, FFN_TF, d), lambda i, f: (l, j, f, 0)),
            pl.BlockSpec((1, d), lambda i, f: (0, 0)),
        ],
        out_specs=pl.BlockSpec((tm, d), lambda i, f: (i, 0)),
        out_shape=jax.ShapeDtypeStruct((m, d), f32),
        scratch_shapes=[pltpu.VMEM((tm, d), bf16), pltpu.VMEM((tm, d), f32)],
        compiler_params=pltpu.CompilerParams(
            dimension_semantics=("parallel", "arbitrary"), vmem_limit_bytes=VMEM_LIMIT),
        name="ffn_half",
    )(x, g.reshape(1, d), wg, wu, wd, fg.reshape(1, d))


def _inproj_kernel(x_ref, g_ref, w_ref, o_ref, h_scr):
    @pl.when(pl.program_id(1) == 0)
    def _():
        h_scr[...] = _rms(x_ref[...], g_ref[...]).astype(bf16)

    o_ref[...] = jnp.dot(h_scr[...], w_ref[...], preferred_element_type=f32)


def _inproj(x, g, w, i):
    m, d = x.shape
    n = w.shape[-1]
    tm = _row_tile(m)
    return pl.pallas_call(
        _inproj_kernel,
        grid=(m // tm, n // PROJ_TN),
        in_specs=[
            pl.BlockSpec((tm, d), lambda r, c: (r, 0)),
            pl.BlockSpec((1, d), lambda r, c: (0, 0)),
            pl.BlockSpec((None, d, PROJ_TN), lambda r, c: (i, 0, c)),
        ],
        out_specs=pl.BlockSpec((tm, PROJ_TN), lambda r, c: (r, c)),
        out_shape=jax.ShapeDtypeStruct((m, n), f32),
        scratch_shapes=[pltpu.VMEM((tm, d), bf16)],
        compiler_params=pltpu.CompilerParams(
            dimension_semantics=("parallel", "arbitrary"), vmem_limit_bytes=VMEM_LIMIT),
        name="mixer_in_proj",
    )(x, g.reshape(1, d), w)


def _outproj_kernel(x_ref, a_ref, w_ref, o_ref):
    o_ref[...] = x_ref[...] + jnp.dot(a_ref[...].astype(bf16), w_ref[...], preferred_element_type=f32)


def _outproj(x, a, w, i):
    m, d = x.shape
    k = a.shape[-1]
    tm = _row_tile(m)
    return pl.pallas_call(
        _outproj_kernel,
        grid=(m // tm, d // PROJ_TN),
        in_specs=[
            pl.BlockSpec((tm, PROJ_TN), lambda r, c: (r, c)),
            pl.BlockSpec((tm, k), lambda r, c: (r, 0)),
            pl.BlockSpec((None, k, PROJ_TN), lambda r, c: (i, 0, c)),
        ],
        out_specs=pl.BlockSpec((tm, PROJ_TN), lambda r, c: (r, c)),
        out_shape=jax.ShapeDtypeStruct((m, d), f32),
        compiler_params=pltpu.CompilerParams(
            dimension_semantics=("parallel", "arbitrary"), vmem_limit_bytes=VMEM_LIMIT),
        name="mixer_out_proj",
    )(x, a, w)


def _split(u, sizes):
    return jnp.split(u, np.cumsum(sizes)[:-1].tolist(), axis=-1)


def _rmsnorm(x, g):
    xf = x.astype(f32)
    y = xf * lax.rsqrt(jnp.mean(xf * xf, axis=-1, keepdims=True) + NORM_EPS)
    return (y * g.astype(f32)).astype(x.dtype)


def _l2norm(x):
    xf = x.astype(f32)
    return xf * lax.rsqrt(jnp.sum(xf * xf, axis=-1, keepdims=True) + 1e-6)


def _masked_softmax(s, mask):
    s = jnp.where(mask, s.astype(f32), -jnp.inf)
    m = jnp.max(s, axis=-1, keepdims=True)
    e = jnp.exp(s - jnp.where(jnp.isfinite(m), m, 0.0))
    return e / jnp.maximum(jnp.sum(e, axis=-1, keepdims=True), 1e-30)


def _causal_conv(buf, u, w, b=None):
    xp = jnp.concatenate([buf.astype(u.dtype), u], axis=1)
    y = lax.conv_general_dilated(xp, w[:, None, :].astype(u.dtype), window_strides=(1,), padding='VALID',
                                 dimension_numbers=('NWC', 'WIO', 'NWC'), feature_group_count=u.shape[-1])
    if b is not None:
        y = y + b
    return y, xp[:, xp.shape[1] - (CONV_WIDTH - 1):]


def _gather_pages(pool, page_table):
    g = pool[page_table]
    return g.reshape((g.shape[0], g.shape[1] * g.shape[2]) + g.shape[3:])


def _blocks(x, n, axis):
    shp = x.shape
    x = x.reshape(shp[:axis] + (n, shp[axis] // n) + shp[axis + 1:])
    return jnp.moveaxis(x, axis, 0)


def _nsa_compressed_selected(q, kv, q_pos, cmp_w):
    B, Tq = q.shape[:2]
    L = kv.shape[1]
    nb = -(-L // NSA_BLOCK)
    kv = jnp.pad(kv, ((0, 0), (0, nb * NSA_BLOCK - L), (0, 0), (0, 0), (0, 0)))
    kvb = kv.reshape(B, nb, NSA_BLOCK, 4, NSA_KV_HEADS, HEAD_DIM)
    scale = HEAD_DIM ** -0.5
    k_c = jnp.einsum('bnjgd,jd->bngd', kvb[:, :, :, 0], cmp_w[0].astype(kv.dtype))
    v_c = jnp.einsum('bnjgd,jd->bngd', kvb[:, :, :, 1], cmp_w[1].astype(kv.dtype))
    s = jnp.einsum('btghd,bngd->bghtn', q, k_c) * scale
    blk = jnp.arange(nb)
    vis = (blk[None, :] + 1) * NSA_BLOCK - 1 <= q_pos[:, None]
    p = _masked_softmax(s, vis)
    o_cmp = jnp.einsum('bghtn,bngd->btghd', p.astype(q.dtype), v_c)
    imp = jnp.sum(p, axis=2)
    cur = (q_pos // NSA_BLOCK)[:, None]
    forced = (blk == 0) | (blk == cur) | (blk == cur - 1)
    score = jnp.where(blk > cur, -jnp.inf, jnp.where(forced, NSA_FORCE, imp))
    n_sel = min(NSA_N_SEL, nb)
    _, sel = lax.top_k(score, n_sel)
    k_s = jnp.moveaxis(kvb[:, :, :, 2], 3, 1)
    v_s = jnp.moveaxis(kvb[:, :, :, 3], 3, 1)
    qb = min(SLC_Q_BLOCK, Tq)
    n_qb = Tq // qb
    n_keys = n_sel * NSA_BLOCK
    gather = jax.vmap(jax.vmap(lambda src, idx: src[idx]))

    def slc_block(args):
        q_b, sel_b, pos_b = args
        kg = gather(k_s, sel_b).reshape(B, NSA_KV_HEADS, qb, n_keys, HEAD_DIM)
        vg = gather(v_s, sel_b).reshape(B, NSA_KV_HEADS, qb, n_keys, HEAD_DIM)
        kpos = (sel_b[..., None] * NSA_BLOCK + jnp.arange(NSA_BLOCK)).reshape(B, NSA_KV_HEADS, qb, n_keys)
        sb = jnp.einsum('btghd,bgtkd->bghtk', q_b, kg) * scale
        pb = _masked_softmax(sb, (kpos <= pos_b[:, None])[:, :, None])
        return jnp.einsum('bghtk,bgtkd->btghd', pb.astype(q_b.dtype), vg)

    o_slc = lax.map(slc_block, (_blocks(q, n_qb, 1), _blocks(sel, n_qb, 2), q_pos.reshape(n_qb, qb)))
    o_slc = jnp.moveaxis(o_slc, 0, 1).reshape(B, Tq, NSA_KV_HEADS, NSA_HPG, HEAD_DIM)
    return o_cmp, o_slc


def _nsa_window_banded(q, kw, vw):
    B, T = q.shape[:2]
    qb = min(Q_BLOCK, T)
    n = T // qb
    span = NSA_WINDOW + qb
    pad = ((0, 0), (NSA_WINDOW, 0), (0, 0), (0, 0))
    idx = jnp.arange(n)[:, None] * qb + jnp.arange(span)[None, :]
    kb = jnp.pad(kw, pad)[:, idx]
    vb = jnp.pad(vw, pad)[:, idx]
    qr = q.reshape(B, n, qb, NSA_KV_HEADS, NSA_HPG, HEAD_DIM)
    s = jnp.einsum('bnqghd,bnkgd->bnghqk', qr, kb) * HEAD_DIM ** -0.5
    qpos = (jnp.arange(n)[:, None] * qb + jnp.arange(qb)[None, :])[:, :, None]
    kpos = (idx - NSA_WINDOW)[:, None, :]
    m = (kpos <= qpos) & (kpos >= qpos - NSA_WINDOW) & (kpos >= 0)
    p = _masked_softmax(s, m[None, :, None, None])
    o = jnp.einsum('bnghqk,bnkgd->bnqghd', p.astype(q.dtype), vb)
    return o.reshape(B, T, NSA_KV_HEADS, NSA_HPG, HEAD_DIM)


def _nsa_window_cached(q, buf, rows, q_pos):
    wk = buf.shape[1]
    kv = jnp.concatenate([buf, rows], axis=1)
    kpos = q_pos[0] - wk + jnp.arange(wk + q.shape[1])
    s = jnp.einsum('btghd,bkgd->bghtk', q, kv[:, :, 0]) * HEAD_DIM ** -0.5
    m = (kpos[None, :] <= q_pos[:, None]) & (kpos[None, :] >= q_pos[:, None] - NSA_WINDOW)
    p = _masked_softmax(s, m)
    return jnp.einsum('bghtk,bkgd->btghd', p.astype(q.dtype), kv[:, :, 1])


def _dsa_attend(q, kv, qi, wi, ki, q_pos):
    B, Tq = q.shape[:2]
    L = kv.shape[1]
    n_keep = min(DSA_TOPK, L // 4)
    qb = min(Q_BLOCK, Tq)
    n = Tq // qb
    kpos = jnp.arange(L)
    ki32 = ki.astype(f32)
    gather = jax.vmap(lambda src, idx: src[idx])

    def block(args):
        q_b, qi_b, wi_b, pos_b = args
        rel = jax.nn.relu(jnp.einsum('bthd,bsd->bths', qi_b.astype(f32), ki32) * IDX_DIM ** -0.5)
        idx_score = jnp.einsum('bths,bth->bts', rel, wi_b.astype(f32) * IDX_HEADS ** -0.5)
        idx_score = jnp.where(kpos[None, None, :] <= pos_b[None, :, None], idx_score, -jnp.inf)
        _, sel = lax.top_k(idx_score, n_keep)
        kvg = gather(kv, sel)
        s = jnp.einsum('btghd,btkgd->bghtk', q_b, kvg[:, :, :, 0]) * HEAD_DIM ** -0.5
        p = _masked_softmax(s, (sel <= pos_b[None, :, None])[:, None, None])
        return jnp.einsum('bghtk,btkgd->btghd', p.astype(q_b.dtype), kvg[:, :, :, 1])

    o = lax.map(block, (_blocks(q, n, 1), _blocks(qi, n, 1), _blocks(wi, n, 1), q_pos.reshape(n, qb)))
    return jnp.moveaxis(o, 0, 1).reshape(B, Tq, DSA_KV_HEADS, DSA_HPG, HEAD_DIM)


def _even_mixer(u, cmp_w, past):
    B, T, _ = u.shape
    qa, kva, ga, qb, kvb, qi, ki, wi = _split(u, EVEN_SIZES)
    qa = qa.reshape(B, T, NSA_KV_HEADS, NSA_HPG, HEAD_DIM)
    kva = kva.reshape(B, T, 6, NSA_KV_HEADS, HEAD_DIM)
    ga = jax.nn.sigmoid(ga.reshape(B, T, NSA_KV_HEADS, NSA_HPG, 3))
    qb = qb.reshape(B, T, DSA_KV_HEADS, DSA_HPG, HEAD_DIM)
    kvb = kvb.reshape(B, T, 2, DSA_KV_HEADS, HEAD_DIM)
    qi = qi.reshape(B, T, IDX_HEADS, IDX_DIM)
    nsa_rows, win_rows = kva[:, :, :4], kva[:, :, 4:]
    if past is None:
        q_pos = jnp.arange(T, dtype=jnp.int32)
        nsa_kv, dsa_kv, idx_k = nsa_rows, kvb, ki
        o_win = _nsa_window_banded(qa, win_rows[:, :, 0], win_rows[:, :, 1])
        new_win = win_rows[:, T - min(NSA_WINDOW, T):]
    else:
        nsa_past, dsa_past, idx_past, win_buf = past
        q_pos = nsa_past.shape[1] + jnp.arange(T, dtype=jnp.int32)
        nsa_kv = jnp.concatenate([nsa_past, nsa_rows], axis=1)
        dsa_kv = jnp.concatenate([dsa_past, kvb], axis=1)
        idx_k = jnp.concatenate([idx_past, ki], axis=1)
        o_win = _nsa_window_cached(qa, win_buf, win_rows, q_pos)
        new_win = jnp.concatenate([win_buf, win_rows], axis=1)[:, T:]
    o_cmp, o_slc = _nsa_compressed_selected(qa, nsa_kv, q_pos, cmp_w)
    o_a = ga[..., 0:1] * o_cmp + ga[..., 1:2] * o_slc + ga[..., 2:3] * o_win
    o_b = _dsa_attend(qb, dsa_kv, qi, wi, idx_k, q_pos)
    o = jnp.concatenate([o_a.reshape(B, T, -1), o_b.reshape(B, T, -1)], axis=-1)
    return o, (nsa_rows, kvb, ki, new_win)


def _gated_delta_rule(q, k, v, g, beta, S0):
    B, T, H, DK = q.shape
    DV = v.shape[-1]
    C = min(GDN_CHUNK, T)
    nC = -(-T // C)
    pad = nC * C - T

    def prep(x):
        x = jnp.pad(x.astype(f32), ((0, 0), (0, pad)) + ((0, 0),) * (x.ndim - 2))
        return jnp.moveaxis(x.reshape((B, nC, C) + x.shape[2:]), 3, 1)

    q, k, v, g, beta = prep(q), prep(k), prep(v), prep(g), prep(beta)
    gc = jnp.cumsum(g, axis=-1)
    causal = jnp.tril(jnp.ones((C, C), bool))
    strict = jnp.tril(jnp.ones((C, C), bool), -1)
    gam = jnp.exp(jnp.where(causal, gc[..., :, None] - gc[..., None, :], -jnp.inf))
    kb = k * beta[..., None]
    A = jnp.where(strict, jnp.einsum('bhnid,bhnjd->bhnij', kb, k) * gam, 0.0)
    eye = jnp.eye(C, dtype=f32)
    Tm = lax.linalg.triangular_solve(A + eye, jnp.broadcast_to(eye, A.shape), left_side=True,
                                     lower=True, unit_diagonal=True)
    U0 = Tm @ (v * beta[..., None])
    Wd = Tm @ (kb * jnp.exp(gc)[..., None])
    QK = jnp.einsum('bhnid,bhnjd->bhnij', q, k) * gam
    k_dec = k * jnp.exp(gc[..., -1:] - gc)[..., None]

    def step(S, xs):
        q_c, gc_c, U0_c, W_c, QK_c, kd_c = xs
        U = U0_c - W_c @ S
        o = (q_c * jnp.exp(gc_c)[..., None]) @ S + QK_c @ U
        S = S * jnp.exp(gc_c[..., -1])[..., None, None] + jnp.swapaxes(kd_c, -1, -2) @ U
        return S, o

    xs = tuple(jnp.moveaxis(t, 2, 0) for t in (q, gc, U0, Wd, QK, k_dec))
    S, o = lax.scan(step, S0.astype(f32), xs)
    o = jnp.moveaxis(jnp.moveaxis(o, 0, 2), 1, 3).reshape(B, nC * C, H, DV)[:, :T]
    return o, S


def _ssd_chunked(x, dt, A, Bm, Cm, h0):
    B, T, H, P = x.shape
    G, N = Bm.shape[2:]
    hpg = H // G
    C = min(SSD_CHUNK, T)
    nC = -(-T // C)
    pad = nC * C - T

    def padt(t):
        return jnp.pad(t.astype(f32), ((0, 0), (0, pad)) + ((0, 0),) * (t.ndim - 2))

    x = padt(x).reshape(B, nC, C, G, hpg, P)
    dt = padt(dt).reshape(B, nC, C, G, hpg)
    Bm = padt(Bm).reshape(B, nC, C, G, N)
    Cm = padt(Cm).reshape(B, nC, C, G, N)
    acs = jnp.cumsum(jnp.moveaxis(dt * A.reshape(G, hpg).astype(f32), 2, -1), axis=-1)
    causal = jnp.tril(jnp.ones((C, C), bool))
    Lm = jnp.exp(jnp.where(causal, acs[..., :, None] - acs[..., None, :], -jnp.inf))
    xdt = x * dt[..., None]
    CB = jnp.einsum('bcign,bcjgn->bcgij', Cm, Bm)
    y_diag = jnp.einsum('bcgij,bcghij,bcjghp->bcighp', CB, Lm, xdt)
    states = jnp.einsum('bcjgn,bcghj,bcjghp->bcghpn', Bm, jnp.exp(acs[..., -1:] - acs), xdt)
    chunk_decay = jnp.exp(acs[..., -1])

    def step(hs, inp):
        st, cd = inp
        return hs * cd[..., None, None] + st, hs

    hN, h_prev = lax.scan(step, h0.astype(f32).reshape(B, G, hpg, P, N),
                          (jnp.moveaxis(states, 1, 0), jnp.moveaxis(chunk_decay, 1, 0)))
    h_prev = jnp.moveaxis(h_prev, 0, 1)
    y_off = jnp.einsum('bcign,bcghpn,bcghi->bcighp', Cm, h_prev, jnp.exp(acs))
    y = (y_diag + y_off).reshape(B, nC * C, H, P)[:, :T]
    return y, hN.reshape(B, H, P, N)


def _odd_mixer(u, W, i, init):
    B, T, _ = u.shape
    S0, conv_c0, h0, conv_d0 = init
    qkv, beta, a, zc, zd, xbc, dt = _split(u, ODD_SIZES)
    qkv, conv_c = _causal_conv(conv_c0, qkv, W['gdn_conv_w'][i])
    q, k, v = _split(jax.nn.silu(qkv), (GDN_HEADS * GDN_DK, GDN_HEADS * GDN_DK, GDN_HEADS * GDN_DV))
    q = _l2norm(q.reshape(B, T, GDN_HEADS, GDN_DK)) * GDN_DK ** -0.5
    k = _l2norm(k.reshape(B, T, GDN_HEADS, GDN_DK))
    v = v.reshape(B, T, GDN_HEADS, GDN_DV)
    beta = jax.nn.sigmoid(beta.astype(f32))
    g = -jnp.exp(W['gdn_a_log'][i].astype(f32)) * jax.nn.softplus(a.astype(f32) + W['gdn_dt_bias'][i])
    o_c, S = _gated_delta_rule(q, k, v, g, beta, S0)
    o_c = _rmsnorm(o_c, W['gdn_norm_g'][i]) * jax.nn.silu(zc.reshape(B, T, GDN_HEADS, GDN_DV).astype(f32))
    xbc, conv_d = _causal_conv(conv_d0, xbc, W['ssd_conv_w'][i], W['ssd_conv_b'][i])
    xs, Bm, Cm = _split(jax.nn.silu(xbc), (SSD_D_INNER, SSD_GROUPS * SSD_STATE, SSD_GROUPS * SSD_STATE))
    xs = xs.reshape(B, T, SSD_HEADS, SSD_HEAD_DIM)
    dtp = jax.nn.softplus(dt.astype(f32) + W['ssd_dt_bias'][i])
    A = -jnp.exp(W['ssd_a_log'][i].astype(f32))
    y, hN = _ssd_chunked(xs, dtp, A, Bm.reshape(B, T, SSD_GROUPS, SSD_STATE),
                         Cm.reshape(B, T, SSD_GROUPS, SSD_STATE), h0)
    y = y + W['ssd_d'][i][:, None] * xs
    y = y.reshape(B, T, SSD_D_INNER) * jax.nn.silu(zd.astype(f32))
    y = _rmsnorm(y.reshape(B, T, SSD_GROUPS, -1), W['ssd_norm_g'][i].reshape(SSD_GROUPS, -1))
    o_cat = jnp.concatenate([o_c.reshape(B, T, -1), y.reshape(B, T, -1)], axis=-1)
    return o_cat, (S, conv_c, hN, conv_d)


def _trunk(x, W, cache):
    B, T, D = x.shape
    x = x.reshape(B * T, D)
    attn_new, rec_new = [], []
    for l in range(DEPTH):
        i = l // 2
        x = _ffn(x, W['norm_g'][l, 0], W['ffn_wg'], W['ffn_wu'], W['ffn_wd'], l, 0)
        if l % 2 == 0:
            u = _inproj(x, W['norm_g'][l, 1], W['attn_w_in'], i)[:, :EVEN_IN].reshape(B, T, EVEN_IN)
            past = None
            if cache is not None:
                pt = cache['page_table']
                past = (_gather_pages(cache['nsa_kv'][i], pt), _gather_pages(cache['dsa_kv'][i], pt),
                        _gather_pages(cache['dsa_idx_k'][i], pt), cache['nsa_win'][i])
            o, st = _even_mixer(u, W['nsa_cmp_w'][i], past)
            attn_new.append(st)
            x = _outproj(x, o.reshape(B * T, -1), W['attn_w_out'], i)
        else:
            u = _inproj(x, W['norm_g'][l, 1], W['rec_w_in'], i)[:, :ODD_IN].reshape(B, T, ODD_IN)
            if cache is None:
                init = (jnp.zeros((B, GDN_HEADS, GDN_DK, GDN_DV), f32),
                        jnp.zeros((B, CONV_WIDTH - 1, GDN_CONV_DIM), f32),
                        jnp.zeros((B, SSD_HEADS, SSD_HEAD_DIM, SSD_STATE), f32),
                        jnp.zeros((B, CONV_WIDTH - 1, SSD_CONV_DIM), f32))
            else:
                init = (cache['gdn'][i], cache['gdn_conv'][i], cache['ssd'][i], cache['ssd_conv'][i])
            o, st = _odd_mixer(u, W, i, init)
            rec_new.append(st)
            x = _outproj(x, o.reshape(B * T, -1), W['rec_w_out'], i)
        x = _ffn(x, W['norm_g'][l, 2], W['ffn_wg'], W['ffn_wu'], W['ffn_wd'], l, 1,
                 final_g=W['final_norm_g'] if l == DEPTH - 1 else None)
    y = x.reshape(B, T, D)

    def stack(lst, j):
        return jnp.stack([s[j] for s in lst])

    return (y, stack(attn_new, 0), stack(attn_new, 1), stack(attn_new, 2), stack(attn_new, 3),
            stack(rec_new, 0), stack(rec_new, 1), stack(rec_new, 2), stack(rec_new, 3))


def _pad_cols(w, n):
    return jnp.pad(w, ((0, 0), (0, 0), (0, n - w.shape[-1])))


def kernel(x_prompt, x_sample, cache_nsa_kv, cache_dsa_kv, cache_dsa_idx_k, page_table, state_nsa_win,
           state_gdn, state_gdn_conv, state_ssd, state_ssd_conv, norm_g, final_norm_g, ffn_w_gate, ffn_w_up,
           ffn_w_down, attn_w_in, attn_w_out, nsa_cmp_w, rec_w_in, rec_w_out, gdn_conv_w, gdn_a_log,
           gdn_dt_bias, gdn_norm_g, ssd_conv_w, ssd_conv_b, ssd_dt_bias, ssd_a_log, ssd_d, ssd_norm_g):
    W = {'norm_g': norm_g, 'final_norm_g': final_norm_g,
         'ffn_wg': ffn_w_gate.astype(bf16), 'ffn_wu': ffn_w_up.astype(bf16), 'ffn_wd': ffn_w_down.astype(bf16),
         'attn_w_in': _pad_cols(attn_w_in, _round_up(EVEN_IN, PROJ_TN)).astype(bf16),
         'attn_w_out': attn_w_out.astype(bf16), 'nsa_cmp_w': nsa_cmp_w,
         'rec_w_in': _pad_cols(rec_w_in, _round_up(ODD_IN, PROJ_TN)).astype(bf16),
         'rec_w_out': rec_w_out.astype(bf16), 'gdn_conv_w': gdn_conv_w, 'gdn_a_log': gdn_a_log,
         'gdn_dt_bias': gdn_dt_bias, 'gdn_norm_g': gdn_norm_g, 'ssd_conv_w': ssd_conv_w, 'ssd_conv_b': ssd_conv_b,
         'ssd_dt_bias': ssd_dt_bias, 'ssd_a_log': ssd_a_log, 'ssd_d': ssd_d, 'ssd_norm_g': ssd_norm_g}
    cache = {'nsa_kv': cache_nsa_kv, 'dsa_kv': cache_dsa_kv, 'dsa_idx_k': cache_dsa_idx_k,
             'page_table': page_table, 'nsa_win': state_nsa_win, 'gdn': state_gdn, 'gdn_conv': state_gdn_conv,
             'ssd': state_ssd, 'ssd_conv': state_ssd_conv}
    (y_prompt, p_nsa_kv, p_dsa_kv, p_dsa_idx_k, p_nsa_win,
     p_gdn, p_gdn_conv, p_ssd, p_ssd_conv) = _trunk(x_prompt, W, None)
    (y_sample, s_nsa_kv, s_dsa_kv, s_dsa_idx_k, s_nsa_win,
     s_gdn, s_gdn_conv, s_ssd, s_ssd_conv) = _trunk(x_sample, W, cache)
    return (y_prompt, y_sample, p_nsa_kv, p_dsa_kv, p_dsa_idx_k, p_nsa_win, p_gdn, p_gdn_conv, p_ssd, p_ssd_conv,
            s_nsa_kv, s_dsa_kv, s_dsa_idx_k, s_nsa_win, s_gdn, s_gdn_conv, s_ssd, s_ssd_conv)
```

```python
import functools
import math

import jax
import jax.numpy as jnp
import numpy as np
from jax import lax
from jax.experimental import pallas as pl
from jax.experimental.pallas import tpu as pltpu

D_MODEL = 2048
DEPTH = 4
PAGE_SIZE = 128
HEAD_DIM = 128
NSA_HEADS = 8
NSA_KV_HEADS = 2
NSA_HPG = NSA_HEADS // NSA_KV_HEADS
NSA_BLOCK = 64
NSA_N_SEL = 16
NSA_WINDOW = 512
NSA_FORCE = 1.0e4
SLC_Q_BLOCK = 64
DSA_HEADS = 8
DSA_KV_HEADS = 2
DSA_HPG = DSA_HEADS // DSA_KV_HEADS
IDX_HEADS = 8
IDX_DIM = 64
DSA_TOPK = 256
Q_BLOCK = 128
GDN_HEADS = 8
GDN_DK = 128
GDN_DV = 128
GDN_CHUNK = 64
CONV_WIDTH = 4
SSD_D_INNER = D_MODEL // 2
SSD_HEAD_DIM = 64
SSD_HEADS = SSD_D_INNER // SSD_HEAD_DIM
SSD_GROUPS = 2
SSD_STATE = 128
SSD_CHUNK = 64
D_FF = 5632
NORM_EPS = 1e-6

EVEN_SIZES = (NSA_HEADS * HEAD_DIM, 6 * NSA_KV_HEADS * HEAD_DIM, 3 * NSA_HEADS,
              DSA_HEADS * HEAD_DIM, 2 * DSA_KV_HEADS * HEAD_DIM,
              IDX_HEADS * IDX_DIM, IDX_DIM, IDX_HEADS)
EVEN_IN = sum(EVEN_SIZES)
GDN_CONV_DIM = GDN_HEADS * (2 * GDN_DK + GDN_DV)
SSD_CONV_DIM = SSD_D_INNER + 2 * SSD_GROUPS * SSD_STATE
ODD_SIZES = (GDN_CONV_DIM, GDN_HEADS, GDN_HEADS, GDN_HEADS * GDN_DV,
             SSD_D_INNER, SSD_CONV_DIM, SSD_HEADS)
ODD_IN = sum(ODD_SIZES)

COL_QA = 0
COL_QB = COL_QA + NSA_HEADS * HEAD_DIM
COL_KVA = COL_QB + DSA_HEADS * HEAD_DIM
COL_KVB = COL_KVA + 6 * NSA_KV_HEADS * HEAD_DIM
COL_QI = COL_KVB + 2 * DSA_KV_HEADS * HEAD_DIM
COL_SM = COL_QI + IDX_HEADS * IDX_DIM
SM_KI = 0
SM_GA = SM_KI + IDX_DIM
SM_WI = SM_GA + 3 * NSA_HEADS

V7X_VMEM_BYTES = 64 * 1024 * 1024
VMEM_LIMIT = V7X_VMEM_BYTES - 12 * 1024 * 1024
PROJ_TN = 1024
FFN_TF = 512

bf16 = jnp.bfloat16
f32 = jnp.float32


def _round_up(n, m):
    return -(-n // m) * m


def _row_tile(m):
    return 512 if m % 512 == 0 else m


def _rms(x, g):
    return x * lax.rsqrt(jnp.mean(x * x, axis=-1, keepdims=True) + NORM_EPS) * g


def _ffn_kernel(x_ref, g_ref, wg_ref, wu_ref, wd_ref, fg_ref, o_ref, h_scr, acc_scr, *, final):
    f = pl.program_id(1)

    @pl.when(f == 0)
    def _():
        h_scr[...] = _rms(x_ref[...], g_ref[...]).astype(bf16)
        acc_scr[...] = jnp.zeros_like(acc_scr)

    h = h_scr[...]
    gate = jnp.dot(h, wg_ref[...], preferred_element_type=f32)
    up = jnp.dot(h, wu_ref[...], preferred_element_type=f32)
    act = (gate * jax.nn.sigmoid(gate) * up).astype(bf16)
    acc_scr[...] += jnp.dot(act, wd_ref[...], preferred_element_type=f32)

    @pl.when(f == pl.num_programs(1) - 1)
    def _():
        y = x_ref[...] + 0.5 * acc_scr[...]
        if final:
            y = _rms(y, fg_ref[...])
        o_ref[...] = y


def _ffn(x, g, wg, wu, wd, l, j, final_g=None):
    m, d = x.shape
    tm = _row_tile(m)
    final = final_g is not None
    fg = final_g if final else g
    return pl.pallas_call(
        functools.partial(_ffn_kernel, final=final),
        grid=(m // tm, D_FF // FFN_TF),
        in_specs=[
            pl.BlockSpec((tm, d), lambda i, f: (i, 0)),
            pl.BlockSpec((1, d), lambda i, f: (0, 0)),
            pl.BlockSpec((None, None, d, FFN_TF), lambda i, f: (l, j, 0, f)),
            pl.BlockSpec((None, None, d, FFN_TF), lambda i, f: (l, j, 0, f)),
            pl.BlockSpec((None, None, FFN_TF, d), lambda i, f: (l, j, f, 0)),
            pl.BlockSpec((1, d), lambda i, f: (0, 0)),
        ],
        out_specs=pl.BlockSpec((tm, d), lambda i, f: (i, 0)),
        out_shape=jax.ShapeDtypeStruct((m, d), f32),
        scratch_shapes=[pltpu.VMEM((tm, d), bf16), pltpu.VMEM((tm, d), f32)],
        compiler_params=pltpu.CompilerParams(
            dimension_semantics=("parallel", "arbitrary"), vmem_limit_bytes=VMEM_LIMIT),
        name="ffn_half",
    )(x, g.reshape(1, d), wg, wu, wd, fg.reshape(1, d))


def _inproj_kernel(x_ref, g_ref, w_ref, o_ref, h_scr):
    @pl.when(pl.program_id(1) == 0)
    def _():
        h_scr[...] = _rms(x_ref[...], g_ref[...]).astype(bf16)

    o_ref[...] = jnp.dot(h_scr[...], w_ref[...], preferred_element_type=f32)


def _inproj(x, g, w, i):
    m, d = x.shape
    n = w.shape[-1]
    tm = _row_tile(m)
    return pl.pallas_call(
        _inproj_kernel,
        grid=(m // tm, n // PROJ_TN),
        in_specs=[
            pl.BlockSpec((tm, d), lambda r, c: (r, 0)),
            pl.BlockSpec((1, d), lambda r, c: (0, 0)),
            pl.BlockSpec((None, d, PROJ_TN), lambda r, c: (i, 0, c)),
        ],
        out_specs=pl.BlockSpec((tm, PROJ_TN), lambda r, c: (r, c)),
        out_shape=jax.ShapeDtypeStruct((m, n), f32),
        scratch_shapes=[pltpu.VMEM((tm, d), bf16)],
        compiler_params=pltpu.CompilerParams(
            dimension_semantics=("parallel", "arbitrary"), vmem_limit_bytes=VMEM_LIMIT),
        name="mixer_in_proj",
    )(x, g.reshape(1, d), w)


def _outproj_kernel(x_ref, a0_ref, a1_ref, w0_ref, w1_ref, o_ref):
    o_ref[...] = (x_ref[...]
                  + jnp.dot(a0_ref[...].astype(bf16), w0_ref[...], preferred_element_type=f32)
                  + jnp.dot(a1_ref[...].astype(bf16), w1_ref[...], preferred_element_type=f32))


def _outproj(x, a0, a1, w, i):
    m, d = x.shape
    k = a0.shape[-1]
    tm = _row_tile(m)
    return pl.pallas_call(
        _outproj_kernel,
        grid=(m // tm, d // PROJ_TN),
        in_specs=[
            pl.BlockSpec((tm, PROJ_TN), lambda r, c: (r, c)),
            pl.BlockSpec((tm, k), lambda r, c: (r, 0)),
            pl.BlockSpec((tm, k), lambda r, c: (r, 0)),
            pl.BlockSpec((None, k, PROJ_TN), lambda r, c: (i, 0, c)),
            pl.BlockSpec((None, k, PROJ_TN), lambda r, c: (i, 1, c)),
        ],
        out_specs=pl.BlockSpec((tm, PROJ_TN), lambda r, c: (r, c)),
        out_shape=jax.ShapeDtypeStruct((m, d), f32),
        compiler_params=pltpu.CompilerParams(
            dimension_semantics=("parallel", "arbitrary"), vmem_limit_bytes=VMEM_LIMIT),
        name="mixer_out_proj",
    )(x, a0, a1, w, w)


ATT_TQ = 128
ATT_KC = 512
MASK_NEG = -1e30
INT_MIN = -2 ** 31
INT_MAX = 2 ** 31 - 1


def _stack_heads(q, first, n, scale):
    rows = jnp.concatenate([q[:, (first + h) * HEAD_DIM:(first + h + 1) * HEAD_DIM] for h in range(n)], axis=0)
    return (rows * scale).astype(bf16)


def _dot_nt(a, b):
    return lax.dot_general(a, b, (((1,), (1,)), ((), ())), preferred_element_type=f32)


def _flash_chunks(qg, k_ref, v_ref, bias_scr, nvis, m_scr, l_scr, acc_scr, reps):
    m_scr[...] = jnp.full_like(m_scr, MASK_NEG)
    l_scr[...] = jnp.zeros_like(l_scr)
    acc_scr[...] = jnp.zeros_like(acc_scr)

    def body(c, carry):
        r0 = pl.multiple_of(c * ATT_KC, ATT_KC)
        kc = k_ref[pl.ds(r0, ATT_KC), :].astype(bf16)
        vc = v_ref[pl.ds(r0, ATT_KC), :].astype(bf16)
        b = bias_scr[c]
        s = _dot_nt(qg, kc) + jnp.concatenate([b] * reps, axis=0)
        m_old = m_scr[...]
        m_new = jnp.maximum(m_old, jnp.max(s, axis=-1, keepdims=True))
        alpha = jnp.exp(m_old - m_new)
        p = jnp.exp(s - m_new)
        l_scr[...] = alpha * l_scr[...] + jnp.sum(p, axis=-1, keepdims=True)
        acc_scr[...] = alpha * acc_scr[...] + jnp.dot(p.astype(bf16), vc, preferred_element_type=f32)
        m_scr[...] = m_new
        return carry

    lax.fori_loop(0, nvis, body, 0)
    return acc_scr[...] / jnp.maximum(l_scr[...], 1e-30)


def _dsa_kernel(q_ref, qi_ref, sm_ref, ksm_ref, k0_ref, k1_ref, v0_ref, v1_ref, o_ref,
                key_scr, bias_scr, cut_scr, m_scr, l_scr, acc_scr, *, n_keep, idx_bits):
    tq = ATT_TQ
    t0 = pl.program_id(1) * tq
    nvis = (t0 + tq - 1) // ATT_KC + 1
    qpos = t0 + lax.broadcasted_iota(jnp.int32, (tq, ATT_KC), 0)
    lane = lax.broadcasted_iota(jnp.int32, (tq, ATT_KC), 1)
    sm = sm_ref[...]
    wi = sm[:, SM_WI:SM_WI + IDX_HEADS] * (IDX_HEADS ** -0.5)
    qi_all = qi_ref[...]
    qi_h = [qi_all[:, h * IDX_DIM:(h + 1) * IDX_DIM].astype(bf16) for h in range(IDX_HEADS)]

    def score_chunk(c, carry):
        r0 = pl.multiple_of(c * ATT_KC, ATT_KC)
        ki_c = ksm_ref[pl.ds(r0, ATT_KC), :][:, SM_KI:SM_KI + IDX_DIM].astype(bf16)
        acc = jnp.zeros((tq, ATT_KC), f32)
        for h in range(IDX_HEADS):
            rel = jnp.maximum(_dot_nt(qi_h[h], ki_c) * (IDX_DIM ** -0.5), 0.0)
            acc = acc + rel * wi[:, h:h + 1]
        bits = lax.bitcast_convert_type(acc, jnp.int32)
        key = jnp.where(bits < 0, bits ^ INT_MAX, bits)
        key_scr[c] = jnp.where(r0 + lane <= qpos, key, INT_MIN)
        return carry

    lax.fori_loop(0, nvis, score_chunk, 0)

    def count(indicator):
        def body(c, acc):
            one = indicator(key_scr[c], c)
            for j in range(ATT_KC // 128):
                acc = acc + one[:, j * 128:(j + 1) * 128]
            return acc
        acc = lax.fori_loop(0, nvis, body, jnp.zeros((tq, 128), jnp.int32))
        return jnp.sum(acc, axis=-1, keepdims=True)

    thr = jnp.where(count(lambda k, c: jnp.where(k >= 0, 1, 0)) >= n_keep, 0, INT_MIN)

    def bit_body(i, thr):
        cand = thr | lax.shift_left(jnp.int32(1), 30 - i)
        return jnp.where(count(lambda k, c: jnp.where(k >= cand, 1, 0)) >= n_keep, cand, thr)

    thr = lax.fori_loop(0, 31, bit_body, thr)
    n_gt = count(lambda k, c: jnp.where(k > thr, 1, 0))
    n_ge = count(lambda k, c: jnp.where(k >= thr, 1, 0))
    need = n_keep - n_gt
    tie_rows = jnp.where(n_ge - n_gt > need, jnp.where(thr > INT_MIN, 1, 0), 0)
    cut_scr[...] = jnp.full((tq, 1), INT_MAX, jnp.int32)

    @pl.when(jnp.max(tie_rows) > 0)
    def _():
        def idx_body(i, cut):
            cand = cut | lax.shift_left(jnp.int32(1), idx_bits - 1 - i)
            n = count(lambda k, c: jnp.where(k == thr, jnp.where(c * ATT_KC + lane < cand, 1, 0), 0))
            return jnp.where(n < need, cand, cut)
        cut_scr[...] = lax.fori_loop(0, idx_bits, idx_body, jnp.zeros((tq, 1), jnp.int32))

    cut = cut_scr[...]

    def bias_chunk(c, carry):
        k = key_scr[c]
        kpos = c * ATT_KC + lane
        tie = jnp.where(k == thr, jnp.where(kpos <= cut, 0.0, MASK_NEG), MASK_NEG)
        bias_scr[c] = jnp.where(kpos <= qpos, jnp.where(k > thr, 0.0, tie), MASK_NEG)
        return carry

    lax.fori_loop(0, nvis, bias_chunk, 0)

    q = q_ref[...]
    for g, (k_ref, v_ref) in enumerate(((k0_ref, v0_ref), (k1_ref, v1_ref))):
        qg = _stack_heads(q, g * DSA_HPG, DSA_HPG, HEAD_DIM ** -0.5)
        o = _flash_chunks(qg, k_ref, v_ref, bias_scr, nvis, m_scr, l_scr, acc_scr, DSA_HPG)
        for h in range(DSA_HPG):
            c0 = (g * DSA_HPG + h) * HEAD_DIM
            o_ref[:, c0:c0 + HEAD_DIM] = o[h * tq:(h + 1) * tq]


def _dsa_prompt(u, n_keep):
    B, T, _ = u.shape
    tq = ATT_TQ
    nck = T // ATT_KC
    col = lambda off, w: off // w
    kv = lambda r, g: pl.BlockSpec((None, T, HEAD_DIM), lambda b, i: (b, 0, col(COL_KVB, HEAD_DIM) + 2 * r + g))
    return pl.pallas_call(
        functools.partial(_dsa_kernel, n_keep=n_keep, idx_bits=max(1, (T - 1).bit_length())),
        grid=(B, T // tq),
        in_specs=[
            pl.BlockSpec((None, tq, DSA_HEADS * HEAD_DIM), lambda b, i: (b, i, col(COL_QB, DSA_HEADS * HEAD_DIM))),
            pl.BlockSpec((None, tq, IDX_HEADS * IDX_DIM), lambda b, i: (b, i, col(COL_QI, IDX_HEADS * IDX_DIM))),
            pl.BlockSpec((None, tq, 128), lambda b, i: (b, i, col(COL_SM, 128))),
            pl.BlockSpec((None, T, 128), lambda b, i: (b, 0, col(COL_SM, 128))),
            kv(0, 0), kv(0, 1), kv(1, 0), kv(1, 1),
        ],
        out_specs=pl.BlockSpec((None, tq, DSA_HEADS * HEAD_DIM), lambda b, i: (b, i, 0)),
        out_shape=jax.ShapeDtypeStruct((B, T, DSA_HEADS * HEAD_DIM), f32),
        scratch_shapes=[
            pltpu.VMEM((nck, tq, ATT_KC), jnp.int32), pltpu.VMEM((nck, tq, ATT_KC), f32),
            pltpu.VMEM((tq, 1), jnp.int32),
            pltpu.VMEM((DSA_HPG * tq, 1), f32), pltpu.VMEM((DSA_HPG * tq, 1), f32),
            pltpu.VMEM((DSA_HPG * tq, HEAD_DIM), f32),
        ],
        compiler_params=pltpu.CompilerParams(
            dimension_semantics=("parallel", "arbitrary"), vmem_limit_bytes=VMEM_LIMIT),
        name="dsa_prompt",
    )(u, u, u, u, u, u, u, u)


def _nsa_compress_kernel(x_ref, w_ref, o_ref, *, nb):
    x = x_ref[...].reshape(nb, NSA_BLOCK, HEAD_DIM)
    o_ref[...] = jnp.zeros_like(o_ref)
    o_ref[0:nb, :] = jnp.sum(x * w_ref[...][None], axis=1)


def _nsa_compress(u, cmp_w, nbp):
    B, T, _ = u.shape
    nb = T // NSA_BLOCK
    return pl.pallas_call(
        functools.partial(_nsa_compress_kernel, nb=nb),
        grid=(B, 2, NSA_KV_HEADS),
        in_specs=[
            pl.BlockSpec((None, T, HEAD_DIM), lambda b, r, g: (b, 0, COL_KVA // HEAD_DIM + 2 * r + g)),
            pl.BlockSpec((None, NSA_BLOCK, HEAD_DIM), lambda b, r, g: (r, 0, 0)),
        ],
        out_specs=pl.BlockSpec((None, None, None, nbp, HEAD_DIM), lambda b, r, g: (b, r, g, 0, 0)),
        out_shape=jax.ShapeDtypeStruct((B, 2, NSA_KV_HEADS, nbp, HEAD_DIM), f32),
        compiler_params=pltpu.CompilerParams(dimension_semantics=("parallel", "parallel", "parallel")),
        name="nsa_compress",
    )(u, cmp_w)


def _nsa_kernel(q_ref, sm_ref, kc_ref, vc_ref, ks_ref, vs_ref, kw_ref, vw_ref, o_ref,
                bias_scr, m_scr, l_scr, acc_scr, *, nb, nbp, span):
    tq = ATT_TQ
    hpg = NSA_HPG
    g = pl.program_id(1)
    t0 = pl.program_id(2) * tq
    nvis = (t0 + tq - 1) // ATT_KC + 1
    qg = _stack_heads(q_ref[...], 0, hpg, HEAD_DIM ** -0.5)

    blk = lax.broadcasted_iota(jnp.int32, (tq, nbp), 1)
    qpos_b = t0 + lax.broadcasted_iota(jnp.int32, (tq, nbp), 0)
    blk4 = lax.broadcasted_iota(jnp.int32, (hpg * tq, nbp), 1)
    qpos4 = t0 + (lax.broadcasted_iota(jnp.int32, (hpg * tq, nbp), 0) & (tq - 1))
    vis4 = (blk4 + 1) * NSA_BLOCK - 1 <= qpos4
    s = jnp.where(vis4, _dot_nt(qg, kc_ref[...].astype(bf16)), MASK_NEG)
    e = jnp.where(vis4, jnp.exp(s - jnp.max(s, axis=-1, keepdims=True)), 0.0)
    p = e / jnp.maximum(jnp.sum(e, axis=-1, keepdims=True), 1e-30)
    o_cmp = jnp.dot(p.astype(bf16), vc_ref[...].astype(bf16), preferred_element_type=f32)
    imp = p[0:tq]
    for h in range(1, hpg):
        imp = imp + p[h * tq:(h + 1) * tq]

    cur = qpos_b // NSA_BLOCK
    forced = jnp.where(blk == 0, NSA_FORCE, jnp.where(blk == cur, NSA_FORCE, jnp.where(blk == cur - 1, NSA_FORCE, imp)))
    score = jnp.where(blk > cur, -jnp.inf, forced)
    rank = jnp.zeros((tq, nbp), jnp.int32)
    for b2 in range(nb):
        colv = score[:, b2:b2 + 1]
        rank = rank + jnp.where(colv > score, 1, jnp.where(colv == score, jnp.where(blk > b2, 1, 0), 0))
    sel = jnp.where(blk <= cur, jnp.where(rank < NSA_N_SEL, 1.0, 0.0), 0.0).astype(bf16)

    qpos = t0 + lax.broadcasted_iota(jnp.int32, (tq, ATT_KC), 0)
    lane = lax.broadcasted_iota(jnp.int32, (tq, ATT_KC), 1)
    blk_row = lax.broadcasted_iota(jnp.int32, (nbp, ATT_KC), 0)
    key_blk = lax.broadcasted_iota(jnp.int32, (nbp, ATT_KC), 1) // NSA_BLOCK

    def bias_chunk(c, carry):
        expand = jnp.where(key_blk + c * (ATT_KC // NSA_BLOCK) == blk_row, 1.0, 0.0).astype(bf16)
        selk = jnp.dot(sel, expand, preferred_element_type=f32)
        bias_scr[c] = jnp.where(c * ATT_KC + lane <= qpos, jnp.where(selk > 0.5, 0.0, MASK_NEG), MASK_NEG)
        return carry

    lax.fori_loop(0, nvis, bias_chunk, 0)
    o_slc = _flash_chunks(qg, ks_ref, vs_ref, bias_scr, nvis, m_scr, l_scr, acc_scr, hpg)

    start = pl.multiple_of(jnp.maximum(t0 + tq - span, 0), tq)
    kw = kw_ref[pl.ds(start, span), :].astype(bf16)
    vw = vw_ref[pl.ds(start, span), :].astype(bf16)
    kpos_w = start + lax.broadcasted_iota(jnp.int32, (tq, span), 1)
    qpos_w = t0 + lax.broadcasted_iota(jnp.int32, (tq, span), 0)
    bias_w = jnp.where(kpos_w <= qpos_w, jnp.where(kpos_w >= qpos_w - NSA_WINDOW, 0.0, MASK_NEG), MASK_NEG)
    sw = _dot_nt(qg, kw) + jnp.concatenate([bias_w] * hpg, axis=0)
    ew = jnp.exp(sw - jnp.max(sw, axis=-1, keepdims=True))
    pw = ew / jnp.sum(ew, axis=-1, keepdims=True)
    o_win = jnp.dot(pw.astype(bf16), vw, preferred_element_type=f32)

    gates = jax.nn.sigmoid(sm_ref[...][:, SM_GA:SM_GA + 3 * NSA_HEADS])
    for h in range(hpg):
        gh = [jnp.where(g == 0, gates[:, h * 3 + j:h * 3 + j + 1],
                        gates[:, 3 * hpg + h * 3 + j:3 * hpg + h * 3 + j + 1]) for j in range(3)]
        rows = slice(h * tq, (h + 1) * tq)
        o_ref[:, h * HEAD_DIM:(h + 1) * HEAD_DIM] = gh[0] * o_cmp[rows] + gh[1] * o_slc[rows] + gh[2] * o_win[rows]


def _nsa_prompt(u, cmp_w):
    B, T, _ = u.shape
    tq = ATT_TQ
    nb = T // NSA_BLOCK
    nbp = _round_up(nb, 128)
    span = min(NSA_WINDOW + tq, T)
    kcv = _nsa_compress(u, cmp_w, nbp)
    gw = NSA_HPG * HEAD_DIM
    kva = lambda r: pl.BlockSpec((None, T, HEAD_DIM), lambda b, g, i: (b, 0, COL_KVA // HEAD_DIM + 2 * r + g))
    kc = lambda r: pl.BlockSpec((None, None, None, nbp, HEAD_DIM), lambda b, g, i: (b, r, g, 0, 0))
    nck = T // ATT_KC
    return pl.pallas_call(
        functools.partial(_nsa_kernel, nb=nb, nbp=nbp, span=span),
        grid=(B, NSA_KV_HEADS, T // tq),
        in_specs=[
            pl.BlockSpec((None, tq, gw), lambda b, g, i: (b, i, COL_QA // gw + g)),
            pl.BlockSpec((None, tq, 128), lambda b, g, i: (b, i, COL_SM // 128)),
            kc(0), kc(1), kva(2), kva(3), kva(4), kva(5),
        ],
        out_specs=pl.BlockSpec((None, tq, gw), lambda b, g, i: (b, i, g)),
        out_shape=jax.ShapeDtypeStruct((B, T, NSA_HEADS * HEAD_DIM), f32),
        scratch_shapes=[
            pltpu.VMEM((nck, tq, ATT_KC), f32),
            pltpu.VMEM((NSA_HPG * tq, 1), f32), pltpu.VMEM((NSA_HPG * tq, 1), f32),
            pltpu.VMEM((NSA_HPG * tq, HEAD_DIM), f32),
        ],
        compiler_params=pltpu.CompilerParams(
            dimension_semantics=("parallel", "parallel", "arbitrary"), vmem_limit_bytes=VMEM_LIMIT),
        name="nsa_prompt",
    )(u, u, kcv, kcv, u, u, u, u)


def _split(u, sizes):
    return jnp.split(u, np.cumsum(sizes)[:-1].tolist(), axis=-1)


def _rmsnorm(x, g):
    xf = x.astype(f32)
    y = xf * lax.rsqrt(jnp.mean(xf * xf, axis=-1, keepdims=True) + NORM_EPS)
    return (y * g.astype(f32)).astype(x.dtype)


def _l2norm(x):
    xf = x.astype(f32)
    return xf * lax.rsqrt(jnp.sum(xf * xf, axis=-1, keepdims=True) + 1e-6)


def _masked_softmax(s, mask):
    s = jnp.where(mask, s.astype(f32), -jnp.inf)
    m = jnp.max(s, axis=-1, keepdims=True)
    e = jnp.exp(s - jnp.where(jnp.isfinite(m), m, 0.0))
    return e / jnp.maximum(jnp.sum(e, axis=-1, keepdims=True), 1e-30)


def _causal_conv(buf, u, w, b=None):
    xp = jnp.concatenate([buf.astype(u.dtype), u], axis=1)
    y = lax.conv_general_dilated(xp, w[:, None, :].astype(u.dtype), window_strides=(1,), padding='VALID',
                                 dimension_numbers=('NWC', 'WIO', 'NWC'), feature_group_count=u.shape[-1])
    if b is not None:
        y = y + b
    return y, xp[:, xp.shape[1] - (CONV_WIDTH - 1):]


def _gather_pages(pool, page_table):
    g = pool[page_table]
    return g.reshape((g.shape[0], g.shape[1] * g.shape[2]) + g.shape[3:])


def _blocks(x, n, axis):
    shp = x.shape
    x = x.reshape(shp[:axis] + (n, shp[axis] // n) + shp[axis + 1:])
    return jnp.moveaxis(x, axis, 0)


def _nsa_compressed_selected(q, kv, q_pos, cmp_w):
    B, Tq = q.shape[:2]
    L = kv.shape[1]
    nb = -(-L // NSA_BLOCK)
    kv = jnp.pad(kv, ((0, 0), (0, nb * NSA_BLOCK - L), (0, 0), (0, 0), (0, 0)))
    kvb = kv.reshape(B, nb, NSA_BLOCK, 4, NSA_KV_HEADS, HEAD_DIM)
    scale = HEAD_DIM ** -0.5
    k_c = jnp.einsum('bnjgd,jd->bngd', kvb[:, :, :, 0], cmp_w[0].astype(kv.dtype))
    v_c = jnp.einsum('bnjgd,jd->bngd', kvb[:, :, :, 1], cmp_w[1].astype(kv.dtype))
    s = jnp.einsum('btghd,bngd->bghtn', q, k_c) * scale
    blk = jnp.arange(nb)
    vis = (blk[None, :] + 1) * NSA_BLOCK - 1 <= q_pos[:, None]
    p = _masked_softmax(s, vis)
    o_cmp = jnp.einsum('bghtn,bngd->btghd', p.astype(q.dtype), v_c)
    imp = jnp.sum(p, axis=2)
    cur = (q_pos // NSA_BLOCK)[:, None]
    forced = (blk == 0) | (blk == cur) | (blk == cur - 1)
    score = jnp.where(blk > cur, -jnp.inf, jnp.where(forced, NSA_FORCE, imp))
    n_sel = min(NSA_N_SEL, nb)
    _, sel = lax.top_k(score, n_sel)
    k_s = jnp.moveaxis(kvb[:, :, :, 2], 3, 1)
    v_s = jnp.moveaxis(kvb[:, :, :, 3], 3, 1)
    qb = min(SLC_Q_BLOCK, Tq)
    n_qb = Tq // qb
    n_keys = n_sel * NSA_BLOCK
    gather = jax.vmap(jax.vmap(lambda src, idx: src[idx]))

    def slc_block(args):
        q_b, sel_b, pos_b = args
        kg = gather(k_s, sel_b).reshape(B, NSA_KV_HEADS, qb, n_keys, HEAD_DIM)
        vg = gather(v_s, sel_b).reshape(B, NSA_KV_HEADS, qb, n_keys, HEAD_DIM)
        kpos = (sel_b[..., None] * NSA_BLOCK + jnp.arange(NSA_BLOCK)).reshape(B, NSA_KV_HEADS, qb, n_keys)
        sb = jnp.einsum('btghd,bgtkd->bghtk', q_b, kg) * scale
        pb = _masked_softmax(sb, (kpos <= pos_b[:, None])[:, :, None])
        return jnp.einsum('bghtk,bgtkd->btghd', pb.astype(q_b.dtype), vg)

    o_slc = lax.map(slc_block, (_blocks(q, n_qb, 1), _blocks(sel, n_qb, 2), q_pos.reshape(n_qb, qb)))
    o_slc = jnp.moveaxis(o_slc, 0, 1).reshape(B, Tq, NSA_KV_HEADS, NSA_HPG, HEAD_DIM)
    return o_cmp, o_slc


def _nsa_window_banded(q, kw, vw):
    B, T = q.shape[:2]
    qb = min(Q_BLOCK, T)
    n = T // qb
    span = NSA_WINDOW + qb
    pad = ((0, 0), (NSA_WINDOW, 0), (0, 0), (0, 0))
    idx = jnp.arange(n)[:, None] * qb + jnp.arange(span)[None, :]
    kb = jnp.pad(kw, pad)[:, idx]
    vb = jnp.pad(vw, pad)[:, idx]
    qr = q.reshape(B, n, qb, NSA_KV_HEADS, NSA_HPG, HEAD_DIM)
    s = jnp.einsum('bnqghd,bnkgd->bnghqk', qr, kb) * HEAD_DIM ** -0.5
    qpos = (jnp.arange(n)[:, None] * qb + jnp.arange(qb)[None, :])[:, :, None]
    kpos = (idx - NSA_WINDOW)[:, None, :]
    m = (kpos <= qpos) & (kpos >= qpos - NSA_WINDOW) & (kpos >= 0)
    p = _masked_softmax(s, m[None, :, None, None])
    o = jnp.einsum('bnghqk,bnkgd->bnqghd', p.astype(q.dtype), vb)
    return o.reshape(B, T, NSA_KV_HEADS, NSA_HPG, HEAD_DIM)


def _nsa_window_cached(q, buf, rows, q_pos):
    wk = buf.shape[1]
    kv = jnp.concatenate([buf, rows], axis=1)
    kpos = q_pos[0] - wk + jnp.arange(wk + q.shape[1])
    s = jnp.einsum('btghd,bkgd->bghtk', q, kv[:, :, 0]) * HEAD_DIM ** -0.5
    m = (kpos[None, :] <= q_pos[:, None]) & (kpos[None, :] >= q_pos[:, None] - NSA_WINDOW)
    p = _masked_softmax(s, m)
    return jnp.einsum('bghtk,bkgd->btghd', p.astype(q.dtype), kv[:, :, 1])


def _dsa_attend(q, kv, qi, wi, ki, q_pos):
    B, Tq = q.shape[:2]
    L = kv.shape[1]
    n_keep = min(DSA_TOPK, L // 4)
    qb = min(Q_BLOCK, Tq)
    n = Tq // qb
    kpos = jnp.arange(L)
    ki32 = ki.astype(f32)
    gather = jax.vmap(lambda src, idx: src[idx])

    def block(args):
        q_b, qi_b, wi_b, pos_b = args
        rel = jax.nn.relu(jnp.einsum('bthd,bsd->bths', qi_b.astype(f32), ki32) * IDX_DIM ** -0.5)
        idx_score = jnp.einsum('bths,bth->bts', rel, wi_b.astype(f32) * IDX_HEADS ** -0.5)
        idx_score = jnp.where(kpos[None, None, :] <= pos_b[None, :, None], idx_score, -jnp.inf)
        _, sel = lax.top_k(idx_score, n_keep)
        kvg = gather(kv, sel)
        s = jnp.einsum('btghd,btkgd->bghtk', q_b, kvg[:, :, :, 0]) * HEAD_DIM ** -0.5
        p = _masked_softmax(s, (sel <= pos_b[None, :, None])[:, None, None])
        return jnp.einsum('bghtk,btkgd->btghd', p.astype(q_b.dtype), kvg[:, :, :, 1])

    o = lax.map(block, (_blocks(q, n, 1), _blocks(qi, n, 1), _blocks(wi, n, 1), q_pos.reshape(n, qb)))
    return jnp.moveaxis(o, 0, 1).reshape(B, Tq, DSA_KV_HEADS, DSA_HPG, HEAD_DIM)


def _even_mixer(u, cmp_w, past):
    B, T, _ = u.shape
    kva = u[..., COL_KVA:COL_KVB].reshape(B, T, 6, NSA_KV_HEADS, HEAD_DIM)
    kvb = u[..., COL_KVB:COL_QI].reshape(B, T, 2, DSA_KV_HEADS, HEAD_DIM)
    ki = u[..., COL_SM + SM_KI:COL_SM + SM_GA]
    nsa_rows, win_rows = kva[:, :, :4], kva[:, :, 4:]
    if past is None:
        o_a = _nsa_prompt(u, cmp_w)
        o_b = _dsa_prompt(u, min(DSA_TOPK, T // 4))
        return (o_a, o_b), (nsa_rows, kvb, ki, win_rows[:, T - min(NSA_WINDOW, T):])
    qa = u[..., COL_QA:COL_QB].reshape(B, T, NSA_KV_HEADS, NSA_HPG, HEAD_DIM)
    ga = jax.nn.sigmoid(u[..., COL_SM + SM_GA:COL_SM + SM_WI].reshape(B, T, NSA_KV_HEADS, NSA_HPG, 3))
    qb = u[..., COL_QB:COL_KVA].reshape(B, T, DSA_KV_HEADS, DSA_HPG, HEAD_DIM)
    qi = u[..., COL_QI:COL_SM].reshape(B, T, IDX_HEADS, IDX_DIM)
    wi = u[..., COL_SM + SM_WI:COL_SM + SM_WI + IDX_HEADS]
    nsa_past, dsa_past, idx_past, win_buf = past
    q_pos = nsa_past.shape[1] + jnp.arange(T, dtype=jnp.int32)
    nsa_kv = jnp.concatenate([nsa_past, nsa_rows], axis=1)
    dsa_kv = jnp.concatenate([dsa_past, kvb], axis=1)
    idx_k = jnp.concatenate([idx_past, ki], axis=1)
    o_win = _nsa_window_cached(qa, win_buf, win_rows, q_pos)
    new_win = jnp.concatenate([win_buf, win_rows], axis=1)[:, T:]
    o_cmp, o_slc = _nsa_compressed_selected(qa, nsa_kv, q_pos, cmp_w)
    o_a = ga[..., 0:1] * o_cmp + ga[..., 1:2] * o_slc + ga[..., 2:3] * o_win
    o_b = _dsa_attend(qb, dsa_kv, qi, wi, idx_k, q_pos)
    return (o_a.reshape(B, T, -1), o_b.reshape(B, T, -1)), (nsa_rows, kvb, ki, new_win)


def _gated_delta_rule(q, k, v, g, beta, S0):
    B, T, H, DK = q.shape
    DV = v.shape[-1]
    C = min(GDN_CHUNK, T)
    nC = -(-T // C)
    pad = nC * C - T

    def prep(x):
        x = jnp.pad(x.astype(f32), ((0, 0), (0, pad)) + ((0, 0),) * (x.ndim - 2))
        return jnp.moveaxis(x.reshape((B, nC, C) + x.shape[2:]), 3, 1)

    q, k, v, g, beta = prep(q), prep(k), prep(v), prep(g), prep(beta)
    gc = jnp.cumsum(g, axis=-1)
    causal = jnp.tril(jnp.ones((C, C), bool))
    strict = jnp.tril(jnp.ones((C, C), bool), -1)
    gam = jnp.exp(jnp.where(causal, gc[..., :, None] - gc[..., None, :], -jnp.inf))
    kb = k * beta[..., None]
    A = jnp.where(strict, jnp.einsum('bhnid,bhnjd->bhnij', kb, k) * gam, 0.0)
    eye = jnp.eye(C, dtype=f32)
    Tm = lax.linalg.triangular_solve(A + eye, jnp.broadcast_to(eye, A.shape), left_side=True,
                                     lower=True, unit_diagonal=True)
    U0 = Tm @ (v * beta[..., None])
    Wd = Tm @ (kb * jnp.exp(gc)[..., None])
    QK = jnp.einsum('bhnid,bhnjd->bhnij', q, k) * gam
    k_dec = k * jnp.exp(gc[..., -1:] - gc)[..., None]

    def step(S, xs):
        q_c, gc_c, U0_c, W_c, QK_c, kd_c = xs
        U = U0_c - W_c @ S
        o = (q_c * jnp.exp(gc_c)[..., None]) @ S + QK_c @ U
        S = S * jnp.exp(gc_c[..., -1])[..., None, None] + jnp.swapaxes(kd_c, -1, -2) @ U
        return S, o

    xs = tuple(jnp.moveaxis(t, 2, 0) for t in (q, gc, U0, Wd, QK, k_dec))
    S, o = lax.scan(step, S0.astype(f32), xs)
    o = jnp.moveaxis(jnp.moveaxis(o, 0, 2), 1, 3).reshape(B, nC * C, H, DV)[:, :T]
    return o, S


def _ssd_chunked(x, dt, A, Bm, Cm, h0):
    B, T, H, P = x.shape
    G, N = Bm.shape[2:]
    hpg = H // G
    C = min(SSD_CHUNK, T)
    nC = -(-T // C)
    pad = nC * C - T

    def padt(t):
        return jnp.pad(t.astype(f32), ((0, 0), (0, pad)) + ((0, 0),) * (t.ndim - 2))

    x = padt(x).reshape(B, nC, C, G, hpg, P)
    dt = padt(dt).reshape(B, nC, C, G, hpg)
    Bm = padt(Bm).reshape(B, nC, C, G, N)
    Cm = padt(Cm).reshape(B, nC, C, G, N)
    acs = jnp.cumsum(jnp.moveaxis(dt * A.reshape(G, hpg).astype(f32), 2, -1), axis=-1)
    causal = jnp.tril(jnp.ones((C, C), bool))
    Lm = jnp.exp(jnp.where(causal, acs[..., :, None] - acs[..., None, :], -jnp.inf))
    xdt = x * dt[..., None]
    CB = jnp.einsum('bcign,bcjgn->bcgij', Cm, Bm)
    y_diag = jnp.einsum('bcgij,bcghij,bcjghp->bcighp', CB, Lm, xdt)
    states = jnp.einsum('bcjgn,bcghj,bcjghp->bcghpn', Bm, jnp.exp(acs[..., -1:] - acs), xdt)
    chunk_decay = jnp.exp(acs[..., -1])

    def step(hs, inp):
        st, cd = inp
        return hs * cd[..., None, None] + st, hs

    hN, h_prev = lax.scan(step, h0.astype(f32).reshape(B, G, hpg, P, N),
                          (jnp.moveaxis(states, 1, 0), jnp.moveaxis(chunk_decay, 1, 0)))
    h_prev = jnp.moveaxis(h_prev, 0, 1)
    y_off = jnp.einsum('bcign,bcghpn,bcghi->bcighp', Cm, h_prev, jnp.exp(acs))
    y = (y_diag + y_off).reshape(B, nC * C, H, P)[:, :T]
    return y, hN.reshape(B, H, P, N)


def _odd_mixer(u, W, i, init):
    B, T, _ = u.shape
    S0, conv_c0, h0, conv_d0 = init
    qkv, beta, a, zc, zd, xbc, dt = _split(u, ODD_SIZES)
    qkv, conv_c = _causal_conv(conv_c0, qkv, W['gdn_conv_w'][i])
    q, k, v = _split(jax.nn.silu(qkv), (GDN_HEADS * GDN_DK, GDN_HEADS * GDN_DK, GDN_HEADS * GDN_DV))
    q = _l2norm(q.reshape(B, T, GDN_HEADS, GDN_DK)) * GDN_DK ** -0.5
    k = _l2norm(k.reshape(B, T, GDN_HEADS, GDN_DK))
    v = v.reshape(B, T, GDN_HEADS, GDN_DV)
    beta = jax.nn.sigmoid(beta.astype(f32))
    g = -jnp.exp(W['gdn_a_log'][i].astype(f32)) * jax.nn.softplus(a.astype(f32) + W['gdn_dt_bias'][i])
    o_c, S = _gated_delta_rule(q, k, v, g, beta, S0)
    o_c = _rmsnorm(o_c, W['gdn_norm_g'][i]) * jax.nn.silu(zc.reshape(B, T, GDN_HEADS, GDN_DV).astype(f32))
    xbc, conv_d = _causal_conv(conv_d0, xbc, W['ssd_conv_w'][i], W['ssd_conv_b'][i])
    xs, Bm, Cm = _split(jax.nn.silu(xbc), (SSD_D_INNER, SSD_GROUPS * SSD_STATE, SSD_GROUPS * SSD_STATE))
    xs = xs.reshape(B, T, SSD_HEADS, SSD_HEAD_DIM)
    dtp = jax.nn.softplus(dt.astype(f32) + W['ssd_dt_bias'][i])
    A = -jnp.exp(W['ssd_a_log'][i].astype(f32))
    y, hN = _ssd_chunked(xs, dtp, A, Bm.reshape(B, T, SSD_GROUPS, SSD_STATE),
                         Cm.reshape(B, T, SSD_GROUPS, SSD_STATE), h0)
    y = y + W['ssd_d'][i][:, None] * xs
    y = y.reshape(B, T, SSD_D_INNER) * jax.nn.silu(zd.astype(f32))
    y = _rmsnorm(y.reshape(B, T, SSD_GROUPS, -1), W['ssd_norm_g'][i].reshape(SSD_GROUPS, -1))
    return (o_c.reshape(B, T, -1), y.reshape(B, T, -1)), (S, conv_c, hN, conv_d)


def _trunk(x, W, cache):
    B, T, D = x.shape
    x = x.reshape(B * T, D)
    attn_new, rec_new = [], []
    for l in range(DEPTH):
        i = l // 2
        x = _ffn(x, W['norm_g'][l, 0], W['ffn_wg'], W['ffn_wu'], W['ffn_wd'], l, 0)
        if l % 2 == 0:
            u = _inproj(x, W['norm_g'][l, 1], W['attn_w_in'], i).reshape(B, T, -1)
            past = None
            if cache is not None:
                pt = cache['page_table']
                past = (_gather_pages(cache['nsa_kv'][i], pt), _gather_pages(cache['dsa_kv'][i], pt),
                        _gather_pages(cache['dsa_idx_k'][i], pt), cache['nsa_win'][i])
            (o0, o1), st = _even_mixer(u, W['nsa_cmp_w'][i], past)
            attn_new.append(st)
            x = _outproj(x, o0.reshape(B * T, -1), o1.reshape(B * T, -1), W['attn_w_out'], i)
        else:
            u = _inproj(x, W['norm_g'][l, 1], W['rec_w_in'], i)[:, :ODD_IN].reshape(B, T, ODD_IN)
            if cache is None:
                init = (jnp.zeros((B, GDN_HEADS, GDN_DK, GDN_DV), f32),
                        jnp.zeros((B, CONV_WIDTH - 1, GDN_CONV_DIM), f32),
                        jnp.zeros((B, SSD_HEADS, SSD_HEAD_DIM, SSD_STATE), f32),
                        jnp.zeros((B, CONV_WIDTH - 1, SSD_CONV_DIM), f32))
            else:
                init = (cache['gdn'][i], cache['gdn_conv'][i], cache['ssd'][i], cache['ssd_conv'][i])
            (o0, o1), st = _odd_mixer(u, W, i, init)
            rec_new.append(st)
            x = _outproj(x, o0.reshape(B * T, -1), o1.reshape(B * T, -1), W['rec_w_out'], i)
        x = _ffn(x, W['norm_g'][l, 2], W['ffn_wg'], W['ffn_wu'], W['ffn_wd'], l, 1,
                 final_g=W['final_norm_g'] if l == DEPTH - 1 else None)
    y = x.reshape(B, T, D)

    def stack(lst, j):
        return jnp.stack([s[j] for s in lst])

    return (y, stack(attn_new, 0), stack(attn_new, 1), stack(attn_new, 2), stack(attn_new, 3),
            stack(rec_new, 0), stack(rec_new, 1), stack(rec_new, 2), stack(rec_new, 3))


def _pad_cols(w, n):
    return jnp.pad(w, ((0, 0), (0, 0), (0, n - w.shape[-1])))


def _pack_even_w(w):
    qa, kva, ga, qb, kvb, qi, ki, wi = _split(w, EVEN_SIZES)
    return _pad_cols(jnp.concatenate([qa, qb, kva, kvb, qi, ki, ga, wi], axis=-1), _round_up(EVEN_IN, PROJ_TN))


def kernel(x_prompt, x_sample, cache_nsa_kv, cache_dsa_kv, cache_dsa_idx_k, page_table, state_nsa_win,
           state_gdn, state_gdn_conv, state_ssd, state_ssd_conv, norm_g, final_norm_g, ffn_w_gate, ffn_w_up,
           ffn_w_down, attn_w_in, attn_w_out, nsa_cmp_w, rec_w_in, rec_w_out, gdn_conv_w, gdn_a_log,
           gdn_dt_bias, gdn_norm_g, ssd_conv_w, ssd_conv_b, ssd_dt_bias, ssd_a_log, ssd_d, ssd_norm_g):
    W = {'norm_g': norm_g, 'final_norm_g': final_norm_g,
         'ffn_wg': ffn_w_gate.astype(bf16), 'ffn_wu': ffn_w_up.astype(bf16), 'ffn_wd': ffn_w_down.astype(bf16),
         'attn_w_in': _pack_even_w(attn_w_in).astype(bf16),
         'attn_w_out': attn_w_out.astype(bf16), 'nsa_cmp_w': nsa_cmp_w,
         'rec_w_in': _pad_cols(rec_w_in, _round_up(ODD_IN, PROJ_TN)).astype(bf16),
         'rec_w_out': rec_w_out.astype(bf16), 'gdn_conv_w': gdn_conv_w, 'gdn_a_log': gdn_a_log,
         'gdn_dt_bias': gdn_dt_bias, 'gdn_norm_g': gdn_norm_g, 'ssd_conv_w': ssd_conv_w, 'ssd_conv_b': ssd_conv_b,
         'ssd_dt_bias': ssd_dt_bias, 'ssd_a_log': ssd_a_log, 'ssd_d': ssd_d, 'ssd_norm_g': ssd_norm_g}
    cache = {'nsa_kv': cache_nsa_kv, 'dsa_kv': cache_dsa_kv, 'dsa_idx_k': cache_dsa_idx_k,
             'page_table': page_table, 'nsa_win': state_nsa_win, 'gdn': state_gdn, 'gdn_conv': state_gdn_conv,
             'ssd': state_ssd, 'ssd_conv': state_ssd_conv}
    (y_prompt, p_nsa_kv, p_dsa_kv, p_dsa_idx_k, p_nsa_win,
     p_gdn, p_gdn_conv, p_ssd, p_ssd_conv) = _trunk(x_prompt, W, None)
    (y_sample, s_nsa_kv, s_dsa_kv, s_dsa_idx_k, s_nsa_win,
     s_gdn, s_gdn_conv, s_ssd, s_ssd_conv) = _trunk(x_sample, W, cache)
    return (y_prompt, y_sample, p_nsa_kv, p_dsa_kv, p_dsa_idx_k, p_nsa_win, p_gdn, p_gdn_conv, p_ssd, p_ssd_conv,
            s_nsa_kv, s_dsa_kv, s_dsa_idx_k, s_nsa_win, s_gdn, s_gdn_conv, s_ssd, s_ssd_conv)
```

```python
import functools
import math

import jax
import jax.numpy as jnp
import numpy as np
from jax import lax
from jax.experimental import pallas as pl
from jax.experimental.pallas import tpu as pltpu

D_MODEL = 2048
DEPTH = 4
PAGE_SIZE = 128
HEAD_DIM = 128
NSA_HEADS = 8
NSA_KV_HEADS = 2
NSA_HPG = NSA_HEADS // NSA_KV_HEADS
NSA_BLOCK = 64
NSA_N_SEL = 16
NSA_WINDOW = 512
NSA_FORCE = 1.0e4
SLC_Q_BLOCK = 64
DSA_HEADS = 8
DSA_KV_HEADS = 2
DSA_HPG = DSA_HEADS // DSA_KV_HEADS
IDX_HEADS = 8
IDX_DIM = 64
DSA_TOPK = 256
Q_BLOCK = 128
GDN_HEADS = 8
GDN_DK = 128
GDN_DV = 128
GDN_CHUNK = 64
CONV_WIDTH = 4
SSD_D_INNER = D_MODEL // 2
SSD_HEAD_DIM = 64
SSD_HEADS = SSD_D_INNER // SSD_HEAD_DIM
SSD_GROUPS = 2
SSD_STATE = 128
SSD_CHUNK = 64
D_FF = 5632
NORM_EPS = 1e-6

EVEN_SIZES = (NSA_HEADS * HEAD_DIM, 6 * NSA_KV_HEADS * HEAD_DIM, 3 * NSA_HEADS,
              DSA_HEADS * HEAD_DIM, 2 * DSA_KV_HEADS * HEAD_DIM,
              IDX_HEADS * IDX_DIM, IDX_DIM, IDX_HEADS)
EVEN_IN = sum(EVEN_SIZES)
GDN_CONV_DIM = GDN_HEADS * (2 * GDN_DK + GDN_DV)
SSD_CONV_DIM = SSD_D_INNER + 2 * SSD_GROUPS * SSD_STATE
ODD_SIZES = (GDN_CONV_DIM, GDN_HEADS, GDN_HEADS, GDN_HEADS * GDN_DV,
             SSD_D_INNER, SSD_CONV_DIM, SSD_HEADS)
ODD_IN = sum(ODD_SIZES)

COL_QA = 0
COL_QB = COL_QA + NSA_HEADS * HEAD_DIM
COL_KVA = COL_QB + DSA_HEADS * HEAD_DIM
COL_KVB = COL_KVA + 6 * NSA_KV_HEADS * HEAD_DIM
COL_QI = COL_KVB + 2 * DSA_KV_HEADS * HEAD_DIM
COL_SM = COL_QI + IDX_HEADS * IDX_DIM
SM_KI = 0
SM_GA = SM_KI + IDX_DIM
SM_WI = SM_GA + 3 * NSA_HEADS
OC_QKV = 0
OC_ZC = OC_QKV + GDN_CONV_DIM
OC_ZD = OC_ZC + GDN_HEADS * GDN_DV
OC_XBC = OC_ZD + SSD_D_INNER
OC_SM = OC_XBC + SSD_CONV_DIM
OSM_BETA = 0
OSM_A = OSM_BETA + GDN_HEADS
OSM_DT = OSM_A + GDN_HEADS

V7X_VMEM_BYTES = 64 * 1024 * 1024
VMEM_LIMIT = V7X_VMEM_BYTES - 12 * 1024 * 1024
PROJ_TN = 1024
FFN_TF = 512

bf16 = jnp.bfloat16
f32 = jnp.float32


def _round_up(n, m):
    return -(-n // m) * m


def _row_tile(m):
    return 512 if m % 512 == 0 else m


def _rms(x, g):
    return x * lax.rsqrt(jnp.mean(x * x, axis=-1, keepdims=True) + NORM_EPS) * g


def _ffn_kernel(x_ref, g_ref, wg_ref, wu_ref, wd_ref, fg_ref, o_ref, h_scr, acc_scr, *, final):
    f = pl.program_id(1)

    @pl.when(f == 0)
    def _():
        h_scr[...] = _rms(x_ref[...], g_ref[...]).astype(bf16)
        acc_scr[...] = jnp.zeros_like(acc_scr)

    h = h_scr[...]
    gate = jnp.dot(h, wg_ref[...], preferred_element_type=f32)
    up = jnp.dot(h, wu_ref[...], preferred_element_type=f32)
    act = (gate * jax.nn.sigmoid(gate) * up).astype(bf16)
    acc_scr[...] += jnp.dot(act, wd_ref[...], preferred_element_type=f32)

    @pl.when(f == pl.num_programs(1) - 1)
    def _():
        y = x_ref[...] + 0.5 * acc_scr[...]
        if final:
            y = _rms(y, fg_ref[...])
        o_ref[...] = y


def _ffn(x, g, wg, wu, wd, l, j, final_g=None):
    m, d = x.shape
    tm = _row_tile(m)
    final = final_g is not None
    fg = final_g if final else g
    return pl.pallas_call(
        functools.partial(_ffn_kernel, final=final),
        grid=(m // tm, D_FF // FFN_TF),
        in_specs=[
            pl.BlockSpec((tm, d), lambda i, f: (i, 0)),
            pl.BlockSpec((1, d), lambda i, f: (0, 0)),
            pl.BlockSpec((None, None, d, FFN_TF), lambda i, f: (l, j, 0, f)),
            pl.BlockSpec((None, None, d, FFN_TF), lambda i, f: (l, j, 0, f)),
            pl.BlockSpec((None, None, FFN_TF, d), lambda i, f: (l, j, f, 0)),
            pl.BlockSpec((1, d), lambda i, f: (0, 0)),
        ],
        out_specs=pl.BlockSpec((tm, d), lambda i, f: (i, 0)),
        out_shape=jax.ShapeDtypeStruct((m, d), f32),
        scratch_shapes=[pltpu.VMEM((tm, d), bf16), pltpu.VMEM((tm, d), f32)],
        compiler_params=pltpu.CompilerParams(
            dimension_semantics=("parallel", "arbitrary"), vmem_limit_bytes=VMEM_LIMIT),
        name="ffn_half",
    )(x, g.reshape(1, d), wg, wu, wd, fg.reshape(1, d))


def _inproj_kernel(x_ref, g_ref, w_ref, o_ref, h_scr):
    @pl.when(pl.program_id(1) == 0)
    def _():
        h_scr[...] = _rms(x_ref[...], g_ref[...]).astype(bf16)

    o_ref[...] = jnp.dot(h_scr[...], w_ref[...], preferred_element_type=f32)


def _inproj(x, g, w, i):
    m, d = x.shape
    n = w.shape[-1]
    tm = _row_tile(m)
    return pl.pallas_call(
        _inproj_kernel,
        grid=(m // tm, n // PROJ_TN),
        in_specs=[
            pl.BlockSpec((tm, d), lambda r, c: (r, 0)),
            pl.BlockSpec((1, d), lambda r, c: (0, 0)),
            pl.BlockSpec((None, d, PROJ_TN), lambda r, c: (i, 0, c)),
        ],
        out_specs=pl.BlockSpec((tm, PROJ_TN), lambda r, c: (r, c)),
        out_shape=jax.ShapeDtypeStruct((m, n), f32),
        scratch_shapes=[pltpu.VMEM((tm, d), bf16)],
        compiler_params=pltpu.CompilerParams(
            dimension_semantics=("parallel", "arbitrary"), vmem_limit_bytes=VMEM_LIMIT),
        name="mixer_in_proj",
    )(x, g.reshape(1, d), w)


def _outproj_kernel(x_ref, a0_ref, a1_ref, w0_ref, w1_ref, g1_ref, o_ref, *, norm_groups):
    a1 = a1_ref[...]
    if norm_groups:
        gw = a1.shape[-1] // norm_groups
        a1 = jnp.concatenate([_rms(a1[:, j * gw:(j + 1) * gw], g1_ref[:, j * gw:(j + 1) * gw])
                              for j in range(norm_groups)], axis=1)
    o_ref[...] = (x_ref[...]
                  + jnp.dot(a0_ref[...].astype(bf16), w0_ref[...], preferred_element_type=f32)
                  + jnp.dot(a1.astype(bf16), w1_ref[...], preferred_element_type=f32))


def _outproj(x, a0, a1, w, i, g1=None, norm_groups=0):
    m, d = x.shape
    k = a0.shape[-1]
    tm = _row_tile(m)
    g1 = jnp.ones((k,), f32) if g1 is None else g1
    return pl.pallas_call(
        functools.partial(_outproj_kernel, norm_groups=norm_groups),
        grid=(m // tm, d // PROJ_TN),
        in_specs=[
            pl.BlockSpec((tm, PROJ_TN), lambda r, c: (r, c)),
            pl.BlockSpec((tm, k), lambda r, c: (r, 0)),
            pl.BlockSpec((tm, k), lambda r, c: (r, 0)),
            pl.BlockSpec((None, k, PROJ_TN), lambda r, c: (i, 0, c)),
            pl.BlockSpec((None, k, PROJ_TN), lambda r, c: (i, 1, c)),
            pl.BlockSpec((1, k), lambda r, c: (0, 0)),
        ],
        out_specs=pl.BlockSpec((tm, PROJ_TN), lambda r, c: (r, c)),
        out_shape=jax.ShapeDtypeStruct((m, d), f32),
        compiler_params=pltpu.CompilerParams(
            dimension_semantics=("parallel", "arbitrary"), vmem_limit_bytes=VMEM_LIMIT),
        name="mixer_out_proj",
    )(x, a0, a1, w, w, g1.reshape(1, k))


ATT_TQ = 128
ATT_KC = 512
MASK_NEG = -1e30
INT_MIN = -2 ** 31
INT_MAX = 2 ** 31 - 1


def _stack_heads(q, first, n, scale):
    rows = jnp.concatenate([q[:, (first + h) * HEAD_DIM:(first + h + 1) * HEAD_DIM] for h in range(n)], axis=0)
    return (rows * scale).astype(bf16)


def _dot_nt(a, b):
    return lax.dot_general(a, b, (((1,), (1,)), ((), ())), preferred_element_type=f32)


def _flash_chunks(qg, k_ref, v_ref, bias_scr, nvis, m_scr, l_scr, acc_scr, reps):
    m_scr[...] = jnp.full_like(m_scr, MASK_NEG)
    l_scr[...] = jnp.zeros_like(l_scr)
    acc_scr[...] = jnp.zeros_like(acc_scr)
    cw = bias_scr.shape[-1]

    def body(c, carry):
        r0 = pl.multiple_of(c * cw, cw)
        kc = k_ref[pl.ds(r0, cw), :].astype(bf16)
        vc = v_ref[pl.ds(r0, cw), :].astype(bf16)
        b = bias_scr[c]
        s = _dot_nt(qg, kc) + jnp.concatenate([b] * reps, axis=0)
        m_old = m_scr[...]
        m_new = jnp.maximum(m_old, jnp.max(s, axis=-1, keepdims=True))
        alpha = jnp.exp(m_old - m_new)
        p = jnp.exp(s - m_new)
        l_scr[...] = alpha * l_scr[...] + jnp.sum(p, axis=-1, keepdims=True)
        acc_scr[...] = alpha * acc_scr[...] + jnp.dot(p.astype(bf16), vc, preferred_element_type=f32)
        m_scr[...] = m_new
        return carry

    lax.fori_loop(0, nvis, body, 0)
    return acc_scr[...] / jnp.maximum(l_scr[...], 1e-30)


def _index_scores(qi_h, wi, ki):
    acc = jnp.zeros((qi_h[0].shape[0], ki.shape[0]), f32)
    for h in range(IDX_HEADS):
        rel = jnp.maximum(_dot_nt(qi_h[h], ki) * (IDX_DIM ** -0.5), 0.0)
        acc = acc + rel * wi[:, h:h + 1]
    return acc


def _order_key(x):
    bits = lax.bitcast_convert_type(x, jnp.int32)
    return jnp.where(bits < 0, bits ^ INT_MAX, bits)


def _topk_bias(key_scr, bias_scr, cut_scr, nvis, qpos, n_keep, idx_bits):
    rows, cw = qpos.shape
    lane = lax.broadcasted_iota(jnp.int32, (rows, cw), 1)

    def count(indicator):
        def body(c, acc):
            one = indicator(key_scr[c], c)
            for j in range(cw // 128):
                acc = acc + one[:, j * 128:(j + 1) * 128]
            return acc
        acc = lax.fori_loop(0, nvis, body, jnp.zeros((rows, 128), jnp.int32))
        return jnp.sum(acc, axis=-1, keepdims=True)

    thr = jnp.where(count(lambda k, c: jnp.where(k >= 0, 1, 0)) >= n_keep, 0, INT_MIN)

    def bit_body(i, thr):
        cand = thr | lax.shift_left(jnp.int32(1), 30 - i)
        return jnp.where(count(lambda k, c: jnp.where(k >= cand, 1, 0)) >= n_keep, cand, thr)

    thr = lax.fori_loop(0, 31, bit_body, thr)
    n_gt = count(lambda k, c: jnp.where(k > thr, 1, 0))
    n_ge = count(lambda k, c: jnp.where(k >= thr, 1, 0))
    need = n_keep - n_gt
    tie_rows = jnp.where(n_ge - n_gt > need, jnp.where(thr > INT_MIN, 1, 0), 0)
    cut_scr[...] = jnp.full((rows, 1), INT_MAX, jnp.int32)

    @pl.when(jnp.max(tie_rows) > 0)
    def _():
        def idx_body(i, cut):
            cand = cut | lax.shift_left(jnp.int32(1), idx_bits - 1 - i)
            n = count(lambda k, c: jnp.where(k == thr, jnp.where(c * cw + lane < cand, 1, 0), 0))
            return jnp.where(n < need, cand, cut)
        cut_scr[...] = lax.fori_loop(0, idx_bits, idx_body, jnp.zeros((rows, 1), jnp.int32))

    cut = cut_scr[...]

    def bias_chunk(c, carry):
        k = key_scr[c]
        kpos = c * cw + lane
        tie = jnp.where(k == thr, jnp.where(kpos <= cut, 0.0, MASK_NEG), MASK_NEG)
        bias_scr[c] = jnp.where(kpos <= qpos, jnp.where(k > thr, 0.0, tie), MASK_NEG)
        return carry

    lax.fori_loop(0, nvis, bias_chunk, 0)


def _dsa_kernel(q_ref, qi_ref, sm_ref, ksm_ref, k0_ref, k1_ref, v0_ref, v1_ref, o_ref,
                key_scr, bias_scr, cut_scr, m_scr, l_scr, acc_scr, *, n_keep, idx_bits):
    tq = ATT_TQ
    t0 = pl.program_id(1) * tq
    nvis = (t0 + tq - 1) // ATT_KC + 1
    qpos = t0 + lax.broadcasted_iota(jnp.int32, (tq, ATT_KC), 0)
    lane = lax.broadcasted_iota(jnp.int32, (tq, ATT_KC), 1)
    wi = sm_ref[...][:, SM_WI:SM_WI + IDX_HEADS] * (IDX_HEADS ** -0.5)
    qi_all = qi_ref[...]
    qi_h = [qi_all[:, h * IDX_DIM:(h + 1) * IDX_DIM].astype(bf16) for h in range(IDX_HEADS)]

    def score_chunk(c, carry):
        r0 = pl.multiple_of(c * ATT_KC, ATT_KC)
        ki_c = ksm_ref[pl.ds(r0, ATT_KC), :][:, SM_KI:SM_KI + IDX_DIM].astype(bf16)
        key_scr[c] = jnp.where(r0 + lane <= qpos, _order_key(_index_scores(qi_h, wi, ki_c)), INT_MIN)
        return carry

    lax.fori_loop(0, nvis, score_chunk, 0)
    _topk_bias(key_scr, bias_scr, cut_scr, nvis, qpos, n_keep, idx_bits)

    q = q_ref[...]
    for g, (k_ref, v_ref) in enumerate(((k0_ref, v0_ref), (k1_ref, v1_ref))):
        qg = _stack_heads(q, g * DSA_HPG, DSA_HPG, HEAD_DIM ** -0.5)
        o = _flash_chunks(qg, k_ref, v_ref, bias_scr, nvis, m_scr, l_scr, acc_scr, DSA_HPG)
        for h in range(DSA_HPG):
            c0 = (g * DSA_HPG + h) * HEAD_DIM
            o_ref[:, c0:c0 + HEAD_DIM] = o[h * tq:(h + 1) * tq]


def _dsa_prompt(u, n_keep):
    B, T, _ = u.shape
    tq = ATT_TQ
    nck = T // ATT_KC
    col = lambda off, w: off // w
    kv = lambda r, g: pl.BlockSpec((None, T, HEAD_DIM), lambda b, i: (b, 0, col(COL_KVB, HEAD_DIM) + 2 * r + g))
    return pl.pallas_call(
        functools.partial(_dsa_kernel, n_keep=n_keep, idx_bits=max(1, (T - 1).bit_length())),
        grid=(B, T // tq),
        in_specs=[
            pl.BlockSpec((None, tq, DSA_HEADS * HEAD_DIM), lambda b, i: (b, i, col(COL_QB, DSA_HEADS * HEAD_DIM))),
            pl.BlockSpec((None, tq, IDX_HEADS * IDX_DIM), lambda b, i: (b, i, col(COL_QI, IDX_HEADS * IDX_DIM))),
            pl.BlockSpec((None, tq, 128), lambda b, i: (b, i, col(COL_SM, 128))),
            pl.BlockSpec((None, T, 128), lambda b, i: (b, 0, col(COL_SM, 128))),
            kv(0, 0), kv(0, 1), kv(1, 0), kv(1, 1),
        ],
        out_specs=pl.BlockSpec((None, tq, DSA_HEADS * HEAD_DIM), lambda b, i: (b, i, 0)),
        out_shape=jax.ShapeDtypeStruct((B, T, DSA_HEADS * HEAD_DIM), f32),
        scratch_shapes=[
            pltpu.VMEM((nck, tq, ATT_KC), jnp.int32), pltpu.VMEM((nck, tq, ATT_KC), f32),
            pltpu.VMEM((tq, 1), jnp.int32),
            pltpu.VMEM((DSA_HPG * tq, 1), f32), pltpu.VMEM((DSA_HPG * tq, 1), f32),
            pltpu.VMEM((DSA_HPG * tq, HEAD_DIM), f32),
        ],
        compiler_params=pltpu.CompilerParams(
            dimension_semantics=("parallel", "arbitrary"), vmem_limit_bytes=VMEM_LIMIT),
        name="dsa_prompt",
    )(u, u, u, u, u, u, u, u)


def _nsa_compress_kernel(x_ref, w_ref, o_ref, *, nb):
    x = x_ref[...].reshape(nb, NSA_BLOCK, HEAD_DIM)
    o_ref[...] = jnp.zeros_like(o_ref)
    o_ref[0:nb, :] = jnp.sum(x * w_ref[...][None], axis=1)


def _nsa_compress(u, cmp_w, nbp):
    B, T, _ = u.shape
    nb = T // NSA_BLOCK
    return pl.pallas_call(
        functools.partial(_nsa_compress_kernel, nb=nb),
        grid=(B, 2, NSA_KV_HEADS),
        in_specs=[
            pl.BlockSpec((None, T, HEAD_DIM), lambda b, r, g: (b, 0, COL_KVA // HEAD_DIM + 2 * r + g)),
            pl.BlockSpec((None, NSA_BLOCK, HEAD_DIM), lambda b, r, g: (r, 0, 0)),
        ],
        out_specs=pl.BlockSpec((None, None, None, nbp, HEAD_DIM), lambda b, r, g: (b, r, g, 0, 0)),
        out_shape=jax.ShapeDtypeStruct((B, 2, NSA_KV_HEADS, nbp, HEAD_DIM), f32),
        compiler_params=pltpu.CompilerParams(dimension_semantics=("parallel", "parallel", "parallel")),
        name="nsa_compress",
    )(u, cmp_w)


def _nsa_compressed(qg, kc, vc, t0, rows, nb):
    hpg = NSA_HPG
    nbp = kc.shape[0]
    blk4 = lax.broadcasted_iota(jnp.int32, (hpg * rows, nbp), 1)
    qpos4 = t0 + (lax.broadcasted_iota(jnp.int32, (hpg * rows, nbp), 0) & (rows - 1))
    vis4 = (blk4 + 1) * NSA_BLOCK - 1 <= qpos4
    s = jnp.where(vis4, _dot_nt(qg, kc.astype(bf16)), MASK_NEG)
    e = jnp.where(vis4, jnp.exp(s - jnp.max(s, axis=-1, keepdims=True)), 0.0)
    p = e / jnp.maximum(jnp.sum(e, axis=-1, keepdims=True), 1e-30)
    o_cmp = jnp.dot(p.astype(bf16), vc.astype(bf16), preferred_element_type=f32)
    imp = p[0:rows]
    for h in range(1, hpg):
        imp = imp + p[h * rows:(h + 1) * rows]
    blk = lax.broadcasted_iota(jnp.int32, (rows, nbp), 1)
    cur = (t0 + lax.broadcasted_iota(jnp.int32, (rows, nbp), 0)) // NSA_BLOCK
    forced = jnp.where(blk == 0, NSA_FORCE, jnp.where(blk == cur, NSA_FORCE, jnp.where(blk == cur - 1, NSA_FORCE, imp)))
    score = jnp.where(blk > cur, -jnp.inf, forced)
    rank = jnp.zeros((rows, nbp), jnp.int32)
    for b2 in range(nb):
        colv = score[:, b2:b2 + 1]
        rank = rank + jnp.where(colv > score, 1, jnp.where(colv == score, jnp.where(blk > b2, 1, 0), 0))
    sel = jnp.where(blk <= cur, jnp.where(rank < NSA_N_SEL, 1.0, 0.0), 0.0).astype(bf16)
    return o_cmp, sel


def _nsa_selection_bias(sel, bias_scr, nvis, t0):
    rows, nbp = sel.shape
    cw = bias_scr.shape[-1]
    qpos = t0 + lax.broadcasted_iota(jnp.int32, (rows, cw), 0)
    lane = lax.broadcasted_iota(jnp.int32, (rows, cw), 1)
    blk_row = lax.broadcasted_iota(jnp.int32, (nbp, cw), 0)
    key_blk = lax.broadcasted_iota(jnp.int32, (nbp, cw), 1) // NSA_BLOCK

    def bias_chunk(c, carry):
        expand = jnp.where(key_blk + c * (cw // NSA_BLOCK) == blk_row, 1.0, 0.0).astype(bf16)
        selk = jnp.dot(sel, expand, preferred_element_type=f32)
        bias_scr[c] = jnp.where(c * cw + lane <= qpos, jnp.where(selk > 0.5, 0.0, MASK_NEG), MASK_NEG)
        return carry

    lax.fori_loop(0, nvis, bias_chunk, 0)


def _nsa_window(qg, kw, vw, t0, k0, rows):
    span = kw.shape[0]
    kpos = k0 + lax.broadcasted_iota(jnp.int32, (rows, span), 1)
    qpos = t0 + lax.broadcasted_iota(jnp.int32, (rows, span), 0)
    bias = jnp.where(kpos <= qpos, jnp.where(kpos >= qpos - NSA_WINDOW, 0.0, MASK_NEG), MASK_NEG)
    sw = _dot_nt(qg, kw) + jnp.concatenate([bias] * NSA_HPG, axis=0)
    ew = jnp.exp(sw - jnp.max(sw, axis=-1, keepdims=True))
    pw = ew / jnp.sum(ew, axis=-1, keepdims=True)
    return jnp.dot(pw.astype(bf16), vw, preferred_element_type=f32)


def _nsa_kernel(q_ref, sm_ref, kc_ref, vc_ref, ks_ref, vs_ref, kw_ref, vw_ref, o_ref,
                bias_scr, m_scr, l_scr, acc_scr, *, nb, span):
    tq = ATT_TQ
    hpg = NSA_HPG
    g = pl.program_id(1)
    t0 = pl.program_id(2) * tq
    nvis = (t0 + tq - 1) // ATT_KC + 1
    qg = _stack_heads(q_ref[...], 0, hpg, HEAD_DIM ** -0.5)

    o_cmp, sel = _nsa_compressed(qg, kc_ref[...], vc_ref[...], t0, tq, nb)
    _nsa_selection_bias(sel, bias_scr, nvis, t0)
    o_slc = _flash_chunks(qg, ks_ref, vs_ref, bias_scr, nvis, m_scr, l_scr, acc_scr, hpg)

    start = pl.multiple_of(jnp.maximum(t0 + tq - span, 0), tq)
    o_win = _nsa_window(qg, kw_ref[pl.ds(start, span), :].astype(bf16), vw_ref[pl.ds(start, span), :].astype(bf16),
                        t0, start, tq)

    gates = jax.nn.sigmoid(sm_ref[...][:, SM_GA:SM_GA + 3 * NSA_HEADS])
    for h in range(hpg):
        gh = [jnp.where(g == 0, gates[:, h * 3 + j:h * 3 + j + 1],
                        gates[:, 3 * hpg + h * 3 + j:3 * hpg + h * 3 + j + 1]) for j in range(3)]
        rows = slice(h * tq, (h + 1) * tq)
        o_ref[:, h * HEAD_DIM:(h + 1) * HEAD_DIM] = gh[0] * o_cmp[rows] + gh[1] * o_slc[rows] + gh[2] * o_win[rows]


def _nsa_prompt(u, cmp_w):
    B, T, _ = u.shape
    tq = ATT_TQ
    nb = T // NSA_BLOCK
    nbp = _round_up(nb, 128)
    span = min(NSA_WINDOW + tq, T)
    kcv = _nsa_compress(u, cmp_w, nbp)
    gw = NSA_HPG * HEAD_DIM
    kva = lambda r: pl.BlockSpec((None, T, HEAD_DIM), lambda b, g, i: (b, 0, COL_KVA // HEAD_DIM + 2 * r + g))
    kc = lambda r: pl.BlockSpec((None, None, None, nbp, HEAD_DIM), lambda b, g, i: (b, r, g, 0, 0))
    nck = T // ATT_KC
    return pl.pallas_call(
        functools.partial(_nsa_kernel, nb=nb, span=span),
        grid=(B, NSA_KV_HEADS, T // tq),
        in_specs=[
            pl.BlockSpec((None, tq, gw), lambda b, g, i: (b, i, COL_QA // gw + g)),
            pl.BlockSpec((None, tq, 128), lambda b, g, i: (b, i, COL_SM // 128)),
            kc(0), kc(1), kva(2), kva(3), kva(4), kva(5),
        ],
        out_specs=pl.BlockSpec((None, tq, gw), lambda b, g, i: (b, i, g)),
        out_shape=jax.ShapeDtypeStruct((B, T, NSA_HEADS * HEAD_DIM), f32),
        scratch_shapes=[
            pltpu.VMEM((nck, tq, ATT_KC), f32),
            pltpu.VMEM((NSA_HPG * tq, 1), f32), pltpu.VMEM((NSA_HPG * tq, 1), f32),
            pltpu.VMEM((NSA_HPG * tq, HEAD_DIM), f32),
        ],
        compiler_params=pltpu.CompilerParams(
            dimension_semantics=("parallel", "parallel", "arbitrary"), vmem_limit_bytes=VMEM_LIMIT),
        name="nsa_prompt",
    )(u, u, kcv, kcv, u, u, u, u)


SMP_ROWS = 8
SMP_NPG = 8


def _pad_rows(x, n):
    return jnp.concatenate([x, jnp.zeros((n - x.shape[0], x.shape[1]), x.dtype)], axis=0)


def _dsa_sample_kernel(pt_ref, us_ref, *refs, n_steps, past, n_keep, idx_bits):
    npg = SMP_NPG
    idx_refs, kv_refs, o_ref = refs[:npg], refs[npg:2 * npg], refs[2 * npg]
    key_scr, bias_scr, cut_scr, k_scr, v_scr, m_scr, l_scr, acc_scr = refs[2 * npg + 1:]
    rows = SMP_ROWS
    cw = npg * PAGE_SIZE
    s = pl.program_id(1)
    us = us_ref[...]
    wi = us[:, COL_SM + SM_WI:COL_SM + SM_WI + IDX_HEADS] * (IDX_HEADS ** -0.5)
    qi_h = [us[:, COL_QI + h * IDX_DIM:COL_QI + (h + 1) * IDX_DIM].astype(bf16) for h in range(IDX_HEADS)]

    tiles = []
    for j in range(npg):
        tiles.append(_index_scores(qi_h, wi, idx_refs[j][...].astype(bf16)))
        r0 = pl.multiple_of((s * npg + j) * PAGE_SIZE, PAGE_SIZE)
        for g in range(DSA_KV_HEADS):
            kcol = g * HEAD_DIM
            vcol = (DSA_KV_HEADS + g) * HEAD_DIM
            k_scr[g, pl.ds(r0, PAGE_SIZE), :] = kv_refs[j][:, kcol:kcol + HEAD_DIM].astype(bf16)
            v_scr[g, pl.ds(r0, PAGE_SIZE), :] = kv_refs[j][:, vcol:vcol + HEAD_DIM].astype(bf16)
    key_scr[s] = _order_key(jnp.concatenate(tiles, axis=1))

    @pl.when(s == n_steps - 1)
    def _():
        row = lax.broadcasted_iota(jnp.int32, (rows, cw), 0)
        lane = lax.broadcasted_iota(jnp.int32, (rows, cw), 1)
        ki_new = _pad_rows(us[:, COL_SM + SM_KI:COL_SM + SM_KI + IDX_DIM], cw).astype(bf16)
        key_scr[n_steps] = jnp.where(lane <= row, _order_key(_index_scores(qi_h, wi, ki_new)), INT_MIN)
        for g in range(DSA_KV_HEADS):
            kcol = COL_KVB + g * HEAD_DIM
            vcol = COL_KVB + (DSA_KV_HEADS + g) * HEAD_DIM
            k_scr[g, pl.ds(past, cw), :] = _pad_rows(us[:, kcol:kcol + HEAD_DIM], cw).astype(bf16)
            v_scr[g, pl.ds(past, cw), :] = _pad_rows(us[:, vcol:vcol + HEAD_DIM], cw).astype(bf16)
        _topk_bias(key_scr, bias_scr, cut_scr, n_steps + 1, past + row, n_keep, idx_bits)
        for g in range(DSA_KV_HEADS):
            qg = _stack_heads(us[:, COL_QB:COL_KVA], g * DSA_HPG, DSA_HPG, HEAD_DIM ** -0.5)
            o = _flash_chunks(qg, k_scr.at[g], v_scr.at[g], bias_scr, n_steps + 1, m_scr, l_scr, acc_scr, DSA_HPG)
            for h in range(DSA_HPG):
                c0 = (g * DSA_HPG + h) * HEAD_DIM
                o_ref[:, c0:c0 + HEAD_DIM] = o[h * rows:(h + 1) * rows]


def _page_specs(width, layer):
    return [pl.BlockSpec((None, None, PAGE_SIZE, width),
                         lambda b, s, pt, j=j: (layer, pt[b, s * SMP_NPG + j], 0, 0)) for j in range(SMP_NPG)]


def _dsa_sample(us, idx_pool, kv_pool, page_table, layer, t_new):
    B = us.shape[0]
    rows = SMP_ROWS
    n_pages = page_table.shape[1]
    past = n_pages * PAGE_SIZE
    n_steps = n_pages // SMP_NPG
    cw = SMP_NPG * PAGE_SIZE
    width = DSA_HEADS * HEAD_DIM
    grid_spec = pltpu.PrefetchScalarGridSpec(
        num_scalar_prefetch=1,
        grid=(B, n_steps),
        in_specs=([pl.BlockSpec((None, rows, us.shape[-1]), lambda b, s, pt: (b, 0, 0))]
                  + _page_specs(IDX_DIM, layer) + _page_specs(2 * DSA_KV_HEADS * HEAD_DIM, layer)),
        out_specs=pl.BlockSpec((None, rows, width), lambda b, s, pt: (b, 0, 0)),
        scratch_shapes=[
            pltpu.VMEM((n_steps + 1, rows, cw), jnp.int32), pltpu.VMEM((n_steps + 1, rows, cw), f32),
            pltpu.VMEM((rows, 1), jnp.int32),
            pltpu.VMEM((DSA_KV_HEADS, past + cw, HEAD_DIM), bf16), pltpu.VMEM((DSA_KV_HEADS, past + cw, HEAD_DIM), bf16),
            pltpu.VMEM((DSA_HPG * rows, 1), f32), pltpu.VMEM((DSA_HPG * rows, 1), f32),
            pltpu.VMEM((DSA_HPG * rows, HEAD_DIM), f32),
        ])
    return pl.pallas_call(
        functools.partial(_dsa_sample_kernel, n_steps=n_steps, past=past,
                          n_keep=min(DSA_TOPK, (past + t_new) // 4), idx_bits=(past + cw - 1).bit_length()),
        grid_spec=grid_spec,
        out_shape=jax.ShapeDtypeStruct((B, rows, width), f32),
        compiler_params=pltpu.CompilerParams(
            dimension_semantics=("parallel", "arbitrary"), vmem_limit_bytes=VMEM_LIMIT),
        name="dsa_sample",
    )(page_table, us, *([idx_pool] * SMP_NPG), *([kv_pool] * SMP_NPG))


def _nsa_sample_kernel(pt_ref, us_ref, cw_ref, win_ref, *refs, n_steps, past, t_new):
    npg = SMP_NPG
    pages, o_ref = refs[:npg], refs[npg]
    kc_scr, vc_scr, ks_scr, vs_scr, bias_scr, m_scr, l_scr, acc_scr = refs[npg + 1:]
    rows = SMP_ROWS
    hpg = NSA_HPG
    G = NSA_KV_HEADS
    cw = npg * PAGE_SIZE
    bpp = PAGE_SIZE // NSA_BLOCK
    bpc = cw // NSA_BLOCK
    s = pl.program_id(1)
    w = cw_ref[...]

    @pl.when(s == 0)
    def _():
        kc_scr[...] = jnp.zeros_like(kc_scr)
        vc_scr[...] = jnp.zeros_like(vc_scr)

    def col(r, g):
        return slice((r * G + g) * HEAD_DIM, (r * G + g + 1) * HEAD_DIM)

    for g in range(G):
        kcs, vcs = [], []
        for j in range(npg):
            r0 = pl.multiple_of((s * npg + j) * PAGE_SIZE, PAGE_SIZE)
            kcs.append(jnp.sum(pages[j][:, col(0, g)].reshape(bpp, NSA_BLOCK, HEAD_DIM) * w[0][None], axis=1))
            vcs.append(jnp.sum(pages[j][:, col(1, g)].reshape(bpp, NSA_BLOCK, HEAD_DIM) * w[1][None], axis=1))
            ks_scr[g, pl.ds(r0, PAGE_SIZE), :] = pages[j][:, col(2, g)].astype(bf16)
            vs_scr[g, pl.ds(r0, PAGE_SIZE), :] = pages[j][:, col(3, g)].astype(bf16)
        b0 = pl.multiple_of(s * bpc, bpc)
        kc_scr[g, pl.ds(b0, bpc), :] = jnp.concatenate(kcs, axis=0)
        vc_scr[g, pl.ds(b0, bpc), :] = jnp.concatenate(vcs, axis=0)

    @pl.when(s == n_steps - 1)
    def _():
        us = us_ref[...]
        win = win_ref[...]
        wk = win.shape[0]
        span = _round_up(wk + rows, 128)
        valid = lax.broadcasted_iota(jnp.int32, (rows, HEAD_DIM), 0) < t_new
        gates = jax.nn.sigmoid(us[:, COL_SM + SM_GA:COL_SM + SM_GA + 3 * NSA_HEADS])
        for g in range(G):
            def new(r):
                c0 = COL_KVA + (r * G + g) * HEAD_DIM
                return us[:, c0:c0 + HEAD_DIM]
            kc_new = jnp.sum(jnp.where(valid, new(0) * w[0][0:rows], 0.0), axis=0, keepdims=True)
            vc_new = jnp.sum(jnp.where(valid, new(1) * w[1][0:rows], 0.0), axis=0, keepdims=True)
            kc_scr[g, pl.ds(n_steps * bpc, rows), :] = _pad_rows(kc_new, rows)
            vc_scr[g, pl.ds(n_steps * bpc, rows), :] = _pad_rows(vc_new, rows)
            ks_scr[g, pl.ds(past, cw), :] = _pad_rows(new(2), cw).astype(bf16)
            vs_scr[g, pl.ds(past, cw), :] = _pad_rows(new(3), cw).astype(bf16)
            qg = _stack_heads(us[:, COL_QA:COL_QB], g * hpg, hpg, HEAD_DIM ** -0.5)
            o_cmp, sel = _nsa_compressed(qg, kc_scr[g], vc_scr[g], past, rows, past // NSA_BLOCK + 1)
            _nsa_selection_bias(sel, bias_scr, n_steps + 1, past)
            o_slc = _flash_chunks(qg, ks_scr.at[g], vs_scr.at[g], bias_scr, n_steps + 1, m_scr, l_scr, acc_scr, hpg)
            kw = _pad_rows(jnp.concatenate([win[:, g * HEAD_DIM:(g + 1) * HEAD_DIM], new(4)], axis=0), span)
            vw = _pad_rows(jnp.concatenate([win[:, (G + g) * HEAD_DIM:(G + g + 1) * HEAD_DIM], new(5)], axis=0), span)
            o_win = _nsa_window(qg, kw.astype(bf16), vw.astype(bf16), past, past - wk, rows)
            for h in range(hpg):
                c = (g * hpg + h) * 3
                rws = slice(h * rows, (h + 1) * rows)
                o_ref[:, (g * hpg + h) * HEAD_DIM:(g * hpg + h + 1) * HEAD_DIM] = (
                    gates[:, c:c + 1] * o_cmp[rws] + gates[:, c + 1:c + 2] * o_slc[rws] + gates[:, c + 2:c + 3] * o_win[rws])


def _nsa_sample(us, cmp_w, win_buf, nsa_pool, page_table, layer, t_new):
    B = us.shape[0]
    rows = SMP_ROWS
    n_pages = page_table.shape[1]
    past = n_pages * PAGE_SIZE
    n_steps = n_pages // SMP_NPG
    cw = SMP_NPG * PAGE_SIZE
    nbp = _round_up((n_steps + 1) * (cw // NSA_BLOCK), 128)
    wk = win_buf.shape[2]
    width = NSA_HEADS * HEAD_DIM
    grid_spec = pltpu.PrefetchScalarGridSpec(
        num_scalar_prefetch=1,
        grid=(B, n_steps),
        in_specs=([pl.BlockSpec((None, rows, us.shape[-1]), lambda b, s, pt: (b, 0, 0)),
                   pl.BlockSpec((2, NSA_BLOCK, HEAD_DIM), lambda b, s, pt: (0, 0, 0)),
                   pl.BlockSpec((None, None, wk, win_buf.shape[-1]), lambda b, s, pt: (layer, b, 0, 0))]
                  + _page_specs(4 * NSA_KV_HEADS * HEAD_DIM, layer)),
        out_specs=pl.BlockSpec((None, rows, width), lambda b, s, pt: (b, 0, 0)),
        scratch_shapes=[
            pltpu.VMEM((NSA_KV_HEADS, nbp, HEAD_DIM), f32), pltpu.VMEM((NSA_KV_HEADS, nbp, HEAD_DIM), f32),
            pltpu.VMEM((NSA_KV_HEADS, past + cw, HEAD_DIM), bf16), pltpu.VMEM((NSA_KV_HEADS, past + cw, HEAD_DIM), bf16),
            pltpu.VMEM((n_steps + 1, rows, cw), f32),
            pltpu.VMEM((NSA_HPG * rows, 1), f32), pltpu.VMEM((NSA_HPG * rows, 1), f32),
            pltpu.VMEM((NSA_HPG * rows, HEAD_DIM), f32),
        ])
    return pl.pallas_call(
        functools.partial(_nsa_sample_kernel, n_steps=n_steps, past=past, t_new=t_new),
        grid_spec=grid_spec,
        out_shape=jax.ShapeDtypeStruct((B, rows, width), f32),
        compiler_params=pltpu.CompilerParams(
            dimension_semantics=("parallel", "arbitrary"), vmem_limit_bytes=VMEM_LIMIT),
        name="nsa_sample",
    )(page_table, us, cmp_w, win_buf, *([nsa_pool] * SMP_NPG))


REC_CHUNK = 64
REC_TC = 512
CONV_PAD = 8


def _dot_f32(a, b):
    return jnp.dot(a, b, precision=lax.Precision.HIGHEST, preferred_element_type=f32)


def _dot_nt_f32(a, b):
    return lax.dot_general(a, b, (((1,), (1,)), ((), ())), precision=lax.Precision.HIGHEST,
                           preferred_element_type=f32)


def _dot_tn_f32(a, b):
    return _dot_f32(a.T, b)


def _softplus(x):
    return jnp.maximum(x, 0.0) + jnp.log(1.0 + jnp.exp(-jnp.abs(x)))


def _silu(x):
    return x * jax.nn.sigmoid(x)


def _conv_block(xbuf_scr, raw, w, first, conv0):
    tc = raw.shape[0]
    nh = CONV_WIDTH - 1

    @pl.when(first)
    def _():
        xbuf_scr[CONV_PAD - nh:CONV_PAD, :] = conv0

    xbuf_scr[CONV_PAD:CONV_PAD + tc, :] = raw
    y = xbuf_scr[pl.ds(CONV_PAD - nh, tc), :] * w[0:1, :]
    for j in range(1, CONV_WIDTH):
        y = y + xbuf_scr[pl.ds(CONV_PAD - nh + j, tc), :] * w[j:j + 1, :]
    xbuf_scr[CONV_PAD - nh:CONV_PAD, :] = raw[tc - nh:tc, :]
    return y


def _lane_pick(x, idx):
    lane = lax.broadcasted_iota(jnp.int32, x.shape, 1)
    return jnp.sum(jnp.where(lane == idx, x, 0.0), axis=-1, keepdims=True)


def _chunk_cumsum(col, row):
    c = col.shape[0]
    i = lax.broadcasted_iota(jnp.int32, (c, c), 0)
    j = lax.broadcasted_iota(jnp.int32, (c, c), 1)
    ccol = jnp.sum(jnp.where(j <= i, jnp.broadcast_to(row, (c, c)), 0.0), axis=1, keepdims=True)
    crow = jnp.sum(jnp.where(i <= j, jnp.broadcast_to(col, (c, c)), 0.0), axis=0, keepdims=True)
    return ccol, crow


def _gdn_kernel(alog_ref, dtb_ref, q_ref, k_ref, v_ref, z_ref, sm_ref, smt_ref, wq_ref, wk_ref, wv_ref,
                cq_ref, ck_ref, cv_ref, s0_ref, ng_ref, o_ref, s_out_ref, xbuf_scr, s_scr, *, t_valid):
    tc = q_ref.shape[0]
    C = REC_CHUNK
    h = pl.program_id(1)
    blk = pl.program_id(2)
    first = blk == 0

    @pl.when(first)
    def _():
        s_scr[...] = s0_ref[...]

    raw = jnp.concatenate([q_ref[...], k_ref[...], v_ref[...]], axis=1)
    w = jnp.concatenate([wq_ref[...], wk_ref[...], wv_ref[...]], axis=1)
    conv0 = jnp.concatenate([cq_ref[...], ck_ref[...], cv_ref[...]], axis=1)
    y = _silu(_conv_block(xbuf_scr, raw, w, first, conv0))
    valid_col = (blk * tc + lax.broadcasted_iota(jnp.int32, (tc, 1), 0)) < t_valid
    valid_row = (blk * tc + lax.broadcasted_iota(jnp.int32, (1, tc), 1)) < t_valid
    qf, kf, vf = y[:, 0:GDN_DK], y[:, GDN_DK:2 * GDN_DK], y[:, 2 * GDN_DK:]
    q = jnp.where(valid_col, qf * lax.rsqrt(jnp.sum(qf * qf, axis=-1, keepdims=True) + 1e-6) * (GDN_DK ** -0.5), 0.0)
    k = jnp.where(valid_col, kf * lax.rsqrt(jnp.sum(kf * kf, axis=-1, keepdims=True) + 1e-6), 0.0)
    v = jnp.where(valid_col, vf, 0.0)

    a_scale = -jnp.exp(jnp.full((1, 1), alog_ref[h], f32))
    dtb = dtb_ref[h]
    sm = sm_ref[...]
    beta = jnp.where(valid_col, jax.nn.sigmoid(_lane_pick(sm, OSM_BETA + h)), 0.0)
    g_col = jnp.where(valid_col, a_scale * _softplus(_lane_pick(sm, OSM_A + h) + dtb), 0.0)
    g_row = jnp.where(valid_row, a_scale * _softplus(smt_ref[pl.ds(OSM_A + h, 1), :] + dtb), 0.0)

    ii = lax.broadcasted_iota(jnp.int32, (C, C), 0)
    jj = lax.broadcasted_iota(jnp.int32, (C, C), 1)
    eye = jnp.where(ii == jj, 1.0, 0.0)
    S = s_scr[...]
    outs = []
    for c in range(tc // C):
        r = slice(c * C, (c + 1) * C)
        qc, kc, vc, bc = q[r], k[r], v[r], beta[r]
        gc_col, gc_row = _chunk_cumsum(g_col[r], g_row[:, r])
        gam = jnp.where(jj <= ii, jnp.exp(jnp.where(jj <= ii, gc_col - gc_row, 0.0)), 0.0)
        g_last = gc_col[C - 1:C, :]
        kb = kc * bc
        x = jnp.where(jj < ii, -(_dot_nt_f32(kb, kc) * gam), 0.0)
        tm = eye + x
        for _ in range(max(1, (C - 1).bit_length()) - 1):
            x = _dot_f32(x, x)
            tm = tm + _dot_f32(tm, x)
        u0 = _dot_f32(tm, vc * bc)
        wd = _dot_f32(tm, kb * jnp.exp(gc_col))
        qk = _dot_nt_f32(qc, kc) * gam
        u = u0 - _dot_f32(wd, S)
        outs.append(_dot_f32(qc * jnp.exp(gc_col), S) + _dot_f32(qk, u))
        S = S * jnp.exp(g_last) + _dot_tn_f32(kc * jnp.exp(g_last - gc_col), u)
    s_scr[...] = S
    o = jnp.concatenate(outs, axis=0) if len(outs) > 1 else outs[0]
    o = o * lax.rsqrt(jnp.mean(o * o, axis=-1, keepdims=True) + NORM_EPS) * ng_ref[...]
    o_ref[...] = o * _silu(z_ref[...])

    @pl.when(blk == pl.num_programs(2) - 1)
    def _():
        s_out_ref[...] = S


def _smem_spec():
    return pl.BlockSpec(memory_space=pltpu.SMEM)


def _gdn(u, smt, conv_w, conv0, s0, a_log, dt_bias, norm_g, layer, t_valid):
    B, Tp, _ = u.shape
    tc = min(REC_TC, Tp)
    H = GDN_HEADS
    qkv = lambda part: pl.BlockSpec((None, tc, 128), lambda b, h, i: (b, i, OC_QKV // 128 + part * H + h))
    cw = lambda part: pl.BlockSpec((None, CONV_WIDTH, 128), lambda b, h, i: (layer, 0, part * H + h))
    c0 = lambda part: pl.BlockSpec((None, CONV_WIDTH - 1, 128), lambda b, h, i: (b, 0, part * H + h))
    return pl.pallas_call(
        functools.partial(_gdn_kernel, t_valid=t_valid),
        grid=(B, H, Tp // tc),
        in_specs=[
            _smem_spec(), _smem_spec(),
            qkv(0), qkv(1), qkv(2),
            pl.BlockSpec((None, tc, 128), lambda b, h, i: (b, i, OC_ZC // 128 + h)),
            pl.BlockSpec((None, tc, 128), lambda b, h, i: (b, i, OC_SM // 128)),
            pl.BlockSpec((None, 128, tc), lambda b, h, i: (b, 0, i)),
            cw(0), cw(1), cw(2), c0(0), c0(1), c0(2),
            pl.BlockSpec((None, None, GDN_DK, GDN_DV), lambda b, h, i: (b, h, 0, 0)),
            pl.BlockSpec((1, GDN_DV), lambda b, h, i: (0, 0)),
        ],
        out_specs=[
            pl.BlockSpec((None, tc, 128), lambda b, h, i: (b, i, h)),
            pl.BlockSpec((None, None, GDN_DK, GDN_DV), lambda b, h, i: (b, h, 0, 0)),
        ],
        out_shape=[jax.ShapeDtypeStruct((B, Tp, H * GDN_DV), f32),
                   jax.ShapeDtypeStruct((B, H, GDN_DK, GDN_DV), f32)],
        scratch_shapes=[pltpu.VMEM((CONV_PAD + tc, 3 * 128), f32), pltpu.VMEM((GDN_DK, GDN_DV), f32)],
        compiler_params=pltpu.CompilerParams(
            dimension_semantics=("parallel", "parallel", "arbitrary"), vmem_limit_bytes=VMEM_LIMIT),
        name="gdn_scan",
    )(a_log, dt_bias, u, u, u, u, u, smt, conv_w, conv_w, conv_w, conv0, conv0, conv0, s0, norm_g.reshape(1, -1))


def _ssd_kernel(alog_ref, dtb_ref, d_ref, x_ref, b_ref, c_ref, z_ref, sm_ref, smt_ref, wx_ref, wb_ref, wc_ref,
                bx_ref, bb_ref, bc_ref, cx_ref, cb_ref, cc_ref, h0_ref, o_ref, h_out_ref, xbuf_scr, h_scr,
                *, t_valid):
    tc = x_ref.shape[0]
    C = REC_CHUNK
    P = SSD_HEAD_DIM
    hp = pl.program_id(1)
    blk = pl.program_id(2)
    first = blk == 0

    @pl.when(first)
    def _():
        h_scr[...] = h0_ref[...]

    raw = jnp.concatenate([x_ref[...], b_ref[...], c_ref[...]], axis=1)
    w = jnp.concatenate([wx_ref[...], wb_ref[...], wc_ref[...]], axis=1)
    bias = jnp.concatenate([bx_ref[...], bb_ref[...], bc_ref[...]], axis=1)
    conv0 = jnp.concatenate([cx_ref[...], cb_ref[...], cc_ref[...]], axis=1)
    y = _silu(_conv_block(xbuf_scr, raw, w, first, conv0) + bias)
    valid_col = (blk * tc + lax.broadcasted_iota(jnp.int32, (tc, 1), 0)) < t_valid
    valid_row = (blk * tc + lax.broadcasted_iota(jnp.int32, (1, tc), 1)) < t_valid
    xs = jnp.where(valid_col, y[:, 0:128], 0.0)
    bm = jnp.where(valid_col, y[:, 128:256], 0.0)
    cm = jnp.where(valid_col, y[:, 256:384], 0.0)
    sm = sm_ref[...]
    z = z_ref[...]
    ii = lax.broadcasted_iota(jnp.int32, (C, C), 0)
    jj = lax.broadcasted_iota(jnp.int32, (C, C), 1)
    cbs = [_dot_nt_f32(cm[c * C:(c + 1) * C], bm[c * C:(c + 1) * C]) for c in range(tc // C)]
    halves = []
    for e in range(2):
        hh = 2 * hp + e
        a_neg = -jnp.exp(jnp.full((1, 1), alog_ref[hh], f32))
        dtb = dtb_ref[hh]
        dt_col = jnp.where(valid_col, _softplus(_lane_pick(sm, OSM_DT + hh) + dtb), 0.0)
        dt_row = jnp.where(valid_row, _softplus(smt_ref[pl.ds(OSM_DT + hh, 1), :] + dtb), 0.0)
        xh = xs[:, e * P:(e + 1) * P]
        xdt = xh * dt_col
        hst = h_scr[e]
        outs = []
        for c in range(tc // C):
            r = slice(c * C, (c + 1) * C)
            acs_col, acs_row = _chunk_cumsum(dt_col[r] * a_neg, dt_row[:, r] * a_neg)
            lm = jnp.where(jj <= ii, jnp.exp(jnp.where(jj <= ii, acs_col - acs_row, 0.0)), 0.0)
            a_last = acs_col[C - 1:C, :]
            y_diag = _dot_f32(cbs[c] * lm, xdt[r])
            y_off = _dot_nt_f32(cm[r] * jnp.exp(acs_col), hst)
            outs.append(y_diag + y_off)
            hst = hst * jnp.exp(a_last) + _dot_tn_f32(xdt[r] * jnp.exp(a_last - acs_col), bm[r])
        h_scr[e] = hst
        yh = jnp.concatenate(outs, axis=0) if len(outs) > 1 else outs[0]
        halves.append((yh + d_ref[hh] * xh) * _silu(z[:, e * P:(e + 1) * P]))
    o_ref[...] = jnp.concatenate(halves, axis=1)

    @pl.when(blk == pl.num_programs(2) - 1)
    def _():
        h_out_ref[...] = h_scr[...]


def _ssd(u, smt, conv_w, conv_b, conv0, h0, a_log, dt_bias, d_skip, layer, t_valid):
    B, Tp, _ = u.shape
    tc = min(REC_TC, Tp)
    HP = SSD_HEADS // 2
    hpg = SSD_HEADS // SSD_GROUPS // 2
    xcol = lambda hp: hp
    bcol = lambda hp: SSD_D_INNER // 128 + hp // hpg
    ccol = lambda hp: (SSD_D_INNER + SSD_GROUPS * SSD_STATE) // 128 + hp // hpg
    def tri(fn_col):
        return (pl.BlockSpec((None, tc, 128), lambda b, hp, i: (b, i, OC_XBC // 128 + fn_col(hp))),
                pl.BlockSpec((None, CONV_WIDTH, 128), lambda b, hp, i: (layer, 0, fn_col(hp))),
                pl.BlockSpec((None, 1, 128), lambda b, hp, i: (layer, 0, fn_col(hp))),
                pl.BlockSpec((None, CONV_WIDTH - 1, 128), lambda b, hp, i: (b, 0, fn_col(hp))))
    (xs, wx, bx, cx), (bs, wb, bb, cb), (cs, wc, bc, cc) = tri(xcol), tri(bcol), tri(ccol)
    return pl.pallas_call(
        functools.partial(_ssd_kernel, t_valid=t_valid),
        grid=(B, HP, Tp // tc),
        in_specs=[
            _smem_spec(), _smem_spec(), _smem_spec(),
            xs, bs, cs,
            pl.BlockSpec((None, tc, 128), lambda b, hp, i: (b, i, OC_ZD // 128 + hp)),
            pl.BlockSpec((None, tc, 128), lambda b, hp, i: (b, i, OC_SM // 128)),
            pl.BlockSpec((None, 128, tc), lambda b, hp, i: (b, 0, i)),
            wx, wb, wc, bx, bb, bc, cx, cb, cc,
            pl.BlockSpec((None, 2, SSD_HEAD_DIM, SSD_STATE), lambda b, hp, i: (b, hp, 0, 0)),
        ],
        out_specs=[
            pl.BlockSpec((None, tc, 128), lambda b, hp, i: (b, i, hp)),
            pl.BlockSpec((None, 2, SSD_HEAD_DIM, SSD_STATE), lambda b, hp, i: (b, hp, 0, 0)),
        ],
        out_shape=[jax.ShapeDtypeStruct((B, Tp, SSD_D_INNER), f32),
                   jax.ShapeDtypeStruct((B, SSD_HEADS, SSD_HEAD_DIM, SSD_STATE), f32)],
        scratch_shapes=[pltpu.VMEM((CONV_PAD + tc, 3 * 128), f32), pltpu.VMEM((2, SSD_HEAD_DIM, SSD_STATE), f32)],
        compiler_params=pltpu.CompilerParams(
            dimension_semantics=("parallel", "parallel", "arbitrary"), vmem_limit_bytes=VMEM_LIMIT),
        name="ssd_scan",
    )(a_log, dt_bias, d_skip, u, u, u, u, u, smt, conv_w, conv_w, conv_w, conv_b, conv_b, conv_b,
      conv0, conv0, conv0, h0)


def _split(u, sizes):
    return jnp.split(u, np.cumsum(sizes)[:-1].tolist(), axis=-1)


def _even_mixer(u, cmp_w, cache, i):
    B, T, _ = u.shape
    kva = u[..., COL_KVA:COL_KVB].reshape(B, T, 6, NSA_KV_HEADS, HEAD_DIM)
    kvb = u[..., COL_KVB:COL_QI].reshape(B, T, 2, DSA_KV_HEADS, HEAD_DIM)
    ki = u[..., COL_SM + SM_KI:COL_SM + SM_GA]
    nsa_rows, win_rows = kva[:, :, :4], kva[:, :, 4:]
    if cache is None:
        o_a = _nsa_prompt(u, cmp_w)
        o_b = _dsa_prompt(u, min(DSA_TOPK, T // 4))
        return (o_a, o_b), (nsa_rows, kvb, ki, win_rows[:, T - min(NSA_WINDOW, T):])
    flat = lambda a: a.reshape(a.shape[:3] + (-1,))
    us = jnp.pad(u, ((0, 0), (0, SMP_ROWS - T), (0, 0)))
    pt = cache['page_table']
    o_a = _nsa_sample(us, cmp_w, flat(cache['nsa_win']), flat(cache['nsa_kv']), pt, i, T)[:, :T]
    o_b = _dsa_sample(us, cache['dsa_idx_k'], flat(cache['dsa_kv']), pt, i, T)[:, :T]
    new_win = jnp.concatenate([cache['nsa_win'][i], win_rows], axis=1)[:, T:]
    return (o_a, o_b), (nsa_rows, kvb, ki, new_win)


def _odd_mixer(u, W, i, init, t_valid):
    S0, conv_c0, h0, conv_d0 = init
    nr = W['ssd_conv_b'].shape[0]
    smt = jnp.swapaxes(u[..., OC_SM:OC_SM + 128], 1, 2)
    o_c, S = _gdn(u, smt, W['gdn_conv_w'], conv_c0, S0, W['gdn_a_log'][i], W['gdn_dt_bias'][i],
                  W['gdn_norm_g'][i], i, t_valid)
    y, hN = _ssd(u, smt, W['ssd_conv_w'], W['ssd_conv_b'].reshape(nr, 1, -1), conv_d0, h0, W['ssd_a_log'][i],
                 W['ssd_dt_bias'][i], W['ssd_d'][i], i, t_valid)

    def conv_state(buf, lo, width):
        rows = u[:, :t_valid, lo:lo + width]
        nh = CONV_WIDTH - 1
        return rows[:, t_valid - nh:] if t_valid >= nh else jnp.concatenate([buf, rows], axis=1)[:, -nh:]

    return (o_c[:, :t_valid], y[:, :t_valid]), (S, conv_state(conv_c0, OC_QKV, GDN_CONV_DIM), hN,
                                               conv_state(conv_d0, OC_XBC, SSD_CONV_DIM))


def _trunk(x, W, cache):
    B, T, D = x.shape
    x = x.reshape(B * T, D)
    attn_new, rec_new = [], []
    for l in range(DEPTH):
        i = l // 2
        x = _ffn(x, W['norm_g'][l, 0], W['ffn_wg'], W['ffn_wu'], W['ffn_wd'], l, 0)
        if l % 2 == 0:
            u = _inproj(x, W['norm_g'][l, 1], W['attn_w_in'], i).reshape(B, T, -1)
            (o0, o1), st = _even_mixer(u, W['nsa_cmp_w'][i], cache, i)
            attn_new.append(st)
            x = _outproj(x, o0.reshape(B * T, -1), o1.reshape(B * T, -1), W['attn_w_out'], i)
        else:
            u = _inproj(x, W['norm_g'][l, 1], W['rec_w_in'], i).reshape(B, T, -1)
            u = jnp.pad(u, ((0, 0), (0, _round_up(T, REC_CHUNK) - T), (0, 0)))
            if cache is None:
                init = (jnp.zeros((B, GDN_HEADS, GDN_DK, GDN_DV), f32),
                        jnp.zeros((B, CONV_WIDTH - 1, GDN_CONV_DIM), f32),
                        jnp.zeros((B, SSD_HEADS, SSD_HEAD_DIM, SSD_STATE), f32),
                        jnp.zeros((B, CONV_WIDTH - 1, SSD_CONV_DIM), f32))
            else:
                init = (cache['gdn'][i], cache['gdn_conv'][i], cache['ssd'][i], cache['ssd_conv'][i])
            (o0, o1), st = _odd_mixer(u, W, i, init, T)
            rec_new.append(st)
            x = _outproj(x, o0.reshape(B * T, -1), o1.reshape(B * T, -1), W['rec_w_out'], i,
                         g1=W['ssd_norm_g'][i], norm_groups=SSD_GROUPS)
        x = _ffn(x, W['norm_g'][l, 2], W['ffn_wg'], W['ffn_wu'], W['ffn_wd'], l, 1,
                 final_g=W['final_norm_g'] if l == DEPTH - 1 else None)
    y = x.reshape(B, T, D)

    def stack(lst, j):
        return jnp.stack([s[j] for s in lst])

    return (y, stack(attn_new, 0), stack(attn_new, 1), stack(attn_new, 2), stack(attn_new, 3),
            stack(rec_new, 0), stack(rec_new, 1), stack(rec_new, 2), stack(rec_new, 3))


def _pad_cols(w, n):
    return jnp.pad(w, ((0, 0), (0, 0), (0, n - w.shape[-1])))


def _pack_even_w(w):
    qa, kva, ga, qb, kvb, qi, ki, wi = _split(w, EVEN_SIZES)
    return _pad_cols(jnp.concatenate([qa, qb, kva, kvb, qi, ki, ga, wi], axis=-1), _round_up(EVEN_IN, PROJ_TN))


def _pack_odd_w(w):
    qkv, beta, a, zc, zd, xbc, dt = _split(w, ODD_SIZES)
    return _pad_cols(jnp.concatenate([qkv, zc, zd, xbc, beta, a, dt], axis=-1), _round_up(OC_SM + 128, PROJ_TN))


def kernel(x_prompt, x_sample, cache_nsa_kv, cache_dsa_kv, cache_dsa_idx_k, page_table, state_nsa_win,
           state_gdn, state_gdn_conv, state_ssd, state_ssd_conv, norm_g, final_norm_g, ffn_w_gate, ffn_w_up,
           ffn_w_down, attn_w_in, attn_w_out, nsa_cmp_w, rec_w_in, rec_w_out, gdn_conv_w, gdn_a_log,
           gdn_dt_bias, gdn_norm_g, ssd_conv_w, ssd_conv_b, ssd_dt_bias, ssd_a_log, ssd_d, ssd_norm_g):
    W = {'norm_g': norm_g, 'final_norm_g': final_norm_g,
         'ffn_wg': ffn_w_gate.astype(bf16), 'ffn_wu': ffn_w_up.astype(bf16), 'ffn_wd': ffn_w_down.astype(bf16),
         'attn_w_in': _pack_even_w(attn_w_in).astype(bf16),
         'attn_w_out': attn_w_out.astype(bf16), 'nsa_cmp_w': nsa_cmp_w,
         'rec_w_in': _pack_odd_w(rec_w_in).astype(bf16),
         'rec_w_out': rec_w_out.astype(bf16), 'gdn_conv_w': gdn_conv_w, 'gdn_a_log': gdn_a_log,
         'gdn_dt_bias': gdn_dt_bias, 'gdn_norm_g': gdn_norm_g, 'ssd_conv_w': ssd_conv_w, 'ssd_conv_b': ssd_conv_b,
         'ssd_dt_bias': ssd_dt_bias, 'ssd_a_log': ssd_a_log, 'ssd_d': ssd_d, 'ssd_norm_g': ssd_norm_g}
    cache = {'nsa_kv': cache_nsa_kv, 'dsa_kv': cache_dsa_kv, 'dsa_idx_k': cache_dsa_idx_k,
             'page_table': page_table, 'nsa_win': state_nsa_win, 'gdn': state_gdn, 'gdn_conv': state_gdn_conv,
             'ssd': state_ssd, 'ssd_conv': state_ssd_conv}
    (y_prompt, p_nsa_kv, p_dsa_kv, p_dsa_idx_k, p_nsa_win,
     p_gdn, p_gdn_conv, p_ssd, p_ssd_conv) = _trunk(x_prompt, W, None)
    (y_sample, s_nsa_kv, s_dsa_kv, s_dsa_idx_k, s_nsa_win,
     s_gdn, s_gdn_conv, s_ssd, s_ssd_conv) = _trunk(x_sample, W, cache)
    return (y_prompt, y_sample, p_nsa_kv, p_dsa_kv, p_dsa_idx_k, p_nsa_win, p_gdn, p_gdn_conv, p_ssd, p_ssd_conv,
            s_nsa_kv, s_dsa_kv, s_dsa_idx_k, s_nsa_win, s_gdn, s_gdn_conv, s_ssd, s_ssd_conv)
```

```python
import functools
import math

import jax
import jax.numpy as jnp
import numpy as np
from jax import lax
from jax.experimental import pallas as pl
from jax.experimental.pallas import tpu as pltpu

D_MODEL = 2048
DEPTH = 4
PAGE_SIZE = 128
HEAD_DIM = 128
NSA_HEADS = 8
NSA_KV_HEADS = 2
NSA_HPG = NSA_HEADS // NSA_KV_HEADS
NSA_BLOCK = 64
NSA_N_SEL = 16
NSA_WINDOW = 512
NSA_FORCE = 1.0e4
SLC_Q_BLOCK = 64
DSA_HEADS = 8
DSA_KV_HEADS = 2
DSA_HPG = DSA_HEADS // DSA_KV_HEADS
IDX_HEADS = 8
IDX_DIM = 64
DSA_TOPK = 256
Q_BLOCK = 128
GDN_HEADS = 8
GDN_DK = 128
GDN_DV = 128
GDN_CHUNK = 64
CONV_WIDTH = 4
SSD_D_INNER = D_MODEL // 2
SSD_HEAD_DIM = 64
SSD_HEADS = SSD_D_INNER // SSD_HEAD_DIM
SSD_GROUPS = 2
SSD_STATE = 128
SSD_CHUNK = 64
D_FF = 5632
NORM_EPS = 1e-6

EVEN_SIZES = (NSA_HEADS * HEAD_DIM, 6 * NSA_KV_HEADS * HEAD_DIM, 3 * NSA_HEADS,
              DSA_HEADS * HEAD_DIM, 2 * DSA_KV_HEADS * HEAD_DIM,
              IDX_HEADS * IDX_DIM, IDX_DIM, IDX_HEADS)
EVEN_IN = sum(EVEN_SIZES)
GDN_CONV_DIM = GDN_HEADS * (2 * GDN_DK + GDN_DV)
SSD_CONV_DIM = SSD_D_INNER + 2 * SSD_GROUPS * SSD_STATE
ODD_SIZES = (GDN_CONV_DIM, GDN_HEADS, GDN_HEADS, GDN_HEADS * GDN_DV,
             SSD_D_INNER, SSD_CONV_DIM, SSD_HEADS)
ODD_IN = sum(ODD_SIZES)

COL_QA = 0
COL_QB = COL_QA + NSA_HEADS * HEAD_DIM
COL_KVA = COL_QB + DSA_HEADS * HEAD_DIM
COL_KVB = COL_KVA + 6 * NSA_KV_HEADS * HEAD_DIM
COL_QI = COL_KVB + 2 * DSA_KV_HEADS * HEAD_DIM
COL_SM = COL_QI + IDX_HEADS * IDX_DIM
SM_KI = 0
SM_GA = SM_KI + IDX_DIM
SM_WI = SM_GA + 3 * NSA_HEADS
OC_QKV = 0
OC_ZC = OC_QKV + GDN_CONV_DIM
OC_ZD = OC_ZC + GDN_HEADS * GDN_DV
OC_XBC = OC_ZD + SSD_D_INNER
OC_SM = OC_XBC + SSD_CONV_DIM
OSM_BETA = 0
OSM_A = OSM_BETA + GDN_HEADS
OSM_DT = OSM_A + GDN_HEADS

V7X_VMEM_BYTES = 64 * 1024 * 1024
VMEM_LIMIT = V7X_VMEM_BYTES - 12 * 1024 * 1024
PROJ_TN = 1024
FFN_TF = 512

bf16 = jnp.bfloat16
f32 = jnp.float32


def _round_up(n, m):
    return -(-n // m) * m


def _row_tile(m):
    return 512 if m % 512 == 0 else m


def _rms(x, g):
    return x * lax.rsqrt(jnp.mean(x * x, axis=-1, keepdims=True) + NORM_EPS) * g


def _ffn_kernel(x_ref, g_ref, wg_ref, wu_ref, wd_ref, fg_ref, o_ref, h_scr, acc_scr, *, final):
    f = pl.program_id(1)

    @pl.when(f == 0)
    def _():
        h_scr[...] = _rms(x_ref[...], g_ref[...]).astype(bf16)
        acc_scr[...] = jnp.zeros_like(acc_scr)

    h = h_scr[...]
    gate = jnp.dot(h, wg_ref[...], preferred_element_type=f32)
    up = jnp.dot(h, wu_ref[...], preferred_element_type=f32)
    act = (gate * jax.nn.sigmoid(gate) * up).astype(bf16)
    acc_scr[...] += jnp.dot(act, wd_ref[...], preferred_element_type=f32)

    @pl.when(f == pl.num_programs(1) - 1)
    def _():
        y = x_ref[...] + 0.5 * acc_scr[...]
        if final:
            y = _rms(y, fg_ref[...])
        o_ref[...] = y


def _ffn(x, g, wg, wu, wd, l, j, final_g=None):
    m, d = x.shape
    tm = _row_tile(m)
    final = final_g is not None
    fg = final_g if final else g
    return pl.pallas_call(
        functools.partial(_ffn_kernel, final=final),
        grid=(m // tm, D_FF // FFN_TF),
        in_specs=[
            pl.BlockSpec((tm, d), lambda i, f: (i, 0)),
            pl.BlockSpec((1, d), lambda i, f: (0, 0)),
            pl.BlockSpec((None, None, d, FFN_TF), lambda i, f: (l, j, 0, f)),
            pl.BlockSpec((None, None, d, FFN_TF), lambda i, f: (l, j, 0, f)),
            pl.BlockSpec((None, None, FFN_TF, d), lambda i, f: (l, j, f, 0)),
            pl.BlockSpec((1, d), lambda i, f: (0, 0)),
        ],
        out_specs=pl.BlockSpec((tm, d), lambda i, f: (i, 0)),
        out_shape=jax.ShapeDtypeStruct((m, d), f32),
        scratch_shapes=[pltpu.VMEM((tm, d), bf16), pltpu.VMEM((tm, d), f32)],
        compiler_params=pltpu.CompilerParams(
            dimension_semantics=("parallel", "arbitrary"), vmem_limit_bytes=VMEM_LIMIT),
        name="ffn_half",
    )(x, g.reshape(1, d), wg, wu, wd, fg.reshape(1, d))


def _inproj_kernel(x_ref, g_ref, w_ref, o_ref, h_scr):
    @pl.when(pl.program_id(1) == 0)
    def _():
        h_scr[...] = _rms(x_ref[...], g_ref[...]).astype(bf16)

    o_ref[...] = jnp.dot(h_scr[...], w_ref[...], preferred_element_type=f32)


def _inproj(x, g, w, i):
    m, d = x.shape
    n = w.shape[-1]
    tm = _row_tile(m)
    return pl.pallas_call(
        _inproj_kernel,
        grid=(m // tm, n // PROJ_TN),
        in_specs=[
            pl.BlockSpec((tm, d), lambda r, c: (r, 0)),
            pl.BlockSpec((1, d), lambda r, c: (0, 0)),
            pl.BlockSpec((None, d, PROJ_TN), lambda r, c: (i, 0, c)),
        ],
        out_specs=pl.BlockSpec((tm, PROJ_TN), lambda r, c: (r, c)),
        out_shape=jax.ShapeDtypeStruct((m, n), f32),
        scratch_shapes=[pltpu.VMEM((tm, d), bf16)],
        compiler_params=pltpu.CompilerParams(
            dimension_semantics=("parallel", "arbitrary"), vmem_limit_bytes=VMEM_LIMIT),
        name="mixer_in_proj",
    )(x, g.reshape(1, d), w)


def _outproj_kernel(x_ref, a0_ref, a1_ref, w0_ref, w1_ref, g1_ref, o_ref, *, norm_groups):
    a1 = a1_ref[...]
    if norm_groups:
        gw = a1.shape[-1] // norm_groups
        a1 = jnp.concatenate([_rms(a1[:, j * gw:(j + 1) * gw], g1_ref[:, j * gw:(j + 1) * gw])
                              for j in range(norm_groups)], axis=1)
    o_ref[...] = (x_ref[...]
                  + jnp.dot(a0_ref[...].astype(bf16), w0_ref[...], preferred_element_type=f32)
                  + jnp.dot(a1.astype(bf16), w1_ref[...], preferred_element_type=f32))


def _outproj(x, a0, a1, w, i, g1=None, norm_groups=0):
    m, d = x.shape
    k = a0.shape[-1]
    tm = _row_tile(m)
    g1 = jnp.ones((k,), f32) if g1 is None else g1
    return pl.pallas_call(
        functools.partial(_outproj_kernel, norm_groups=norm_groups),
        grid=(m // tm, d // PROJ_TN),
        in_specs=[
            pl.BlockSpec((tm, PROJ_TN), lambda r, c: (r, c)),
            pl.BlockSpec((tm, k), lambda r, c: (r, 0)),
            pl.BlockSpec((tm, k), lambda r, c: (r, 0)),
            pl.BlockSpec((None, k, PROJ_TN), lambda r, c: (i, 0, c)),
            pl.BlockSpec((None, k, PROJ_TN), lambda r, c: (i, 1, c)),
            pl.BlockSpec((1, k), lambda r, c: (0, 0)),
        ],
        out_specs=pl.BlockSpec((tm, PROJ_TN), lambda r, c: (r, c)),
        out_shape=jax.ShapeDtypeStruct((m, d), f32),
        compiler_params=pltpu.CompilerParams(
            dimension_semantics=("parallel", "arbitrary"), vmem_limit_bytes=VMEM_LIMIT),
        name="mixer_out_proj",
    )(x, a0, a1, w, w, g1.reshape(1, k))


ATT_TQ = 128
ATT_KC = 1024
MASK_NEG = -1e30
INT_MIN = -2 ** 31
INT_MAX = 2 ** 31 - 1


def _stack_heads(q, first, n, scale):
    rows = jnp.concatenate([q[:, (first + h) * HEAD_DIM:(first + h + 1) * HEAD_DIM] for h in range(n)], axis=0)
    return (rows * scale).astype(bf16)


def _dot_nt(a, b):
    return lax.dot_general(a, b, (((1,), (1,)), ((), ())), preferred_element_type=f32)


def _flash_chunks(qg, k_ref, v_ref, bias_scr, nvis, m_scr, l_scr, acc_scr, reps):
    m_scr[...] = jnp.full_like(m_scr, MASK_NEG)
    l_scr[...] = jnp.zeros_like(l_scr)
    acc_scr[...] = jnp.zeros_like(acc_scr)
    cw = bias_scr.shape[-1]

    def body(c, carry):
        r0 = pl.multiple_of(c * cw, cw)
        kc = k_ref[pl.ds(r0, cw), :].astype(bf16)
        vc = v_ref[pl.ds(r0, cw), :].astype(bf16)
        b = bias_scr[c]
        s = _dot_nt(qg, kc) + jnp.concatenate([b] * reps, axis=0)
        m_old = m_scr[...]
        m_new = jnp.maximum(m_old, jnp.max(s, axis=-1, keepdims=True))
        alpha = jnp.exp(m_old - m_new)
        p = jnp.exp(s - m_new)
        l_scr[...] = alpha * l_scr[...] + jnp.sum(p, axis=-1, keepdims=True)
        acc_scr[...] = alpha * acc_scr[...] + jnp.dot(p.astype(bf16), vc, preferred_element_type=f32)
        m_scr[...] = m_new
        return carry

    lax.fori_loop(0, nvis, body, 0)
    return acc_scr[...] / jnp.maximum(l_scr[...], 1e-30)


def _index_scores(qi_h, wi, ki, keys_on_lanes=False):
    n = ki.shape[1] if keys_on_lanes else ki.shape[0]
    acc = jnp.zeros((qi_h[0].shape[0], n), f32)
    for h in range(IDX_HEADS):
        qk = jnp.dot(qi_h[h], ki, preferred_element_type=f32) if keys_on_lanes else _dot_nt(qi_h[h], ki)
        rel = jnp.maximum(qk * (IDX_DIM ** -0.5), 0.0)
        acc = acc + rel * wi[:, h:h + 1]
    return acc


def _order_key(x):
    bits = lax.bitcast_convert_type(x, jnp.int32)
    return jnp.where(bits < 0, bits ^ INT_MAX, bits)


def _topk_bias(key_scr, bias_scr, cut_scr, nvis, qpos, n_keep, idx_bits):
    rows, cw = qpos.shape
    lane = lax.broadcasted_iota(jnp.int32, (rows, cw), 1)

    def count(indicator):
        def body(c, acc):
            one = indicator(key_scr[c], c)
            for j in range(cw // 128):
                acc = acc + one[:, j * 128:(j + 1) * 128]
            return acc
        acc = lax.fori_loop(0, nvis, body, jnp.zeros((rows, 128), jnp.int32))
        return jnp.sum(acc, axis=-1, keepdims=True)

    thr = jnp.where(count(lambda k, c: jnp.where(k >= 0, 1, 0)) >= n_keep, 0, INT_MIN)

    def bit_body(i, thr):
        cand = thr | lax.shift_left(jnp.int32(1), 30 - i)
        return jnp.where(count(lambda k, c: jnp.where(k >= cand, 1, 0)) >= n_keep, cand, thr)

    thr = lax.fori_loop(0, 31, bit_body, thr)
    n_gt = count(lambda k, c: jnp.where(k > thr, 1, 0))
    n_ge = count(lambda k, c: jnp.where(k >= thr, 1, 0))
    need = n_keep - n_gt
    tie_rows = jnp.where(n_ge - n_gt > need, jnp.where(thr > INT_MIN, 1, 0), 0)
    cut_scr[...] = jnp.full((rows, 1), INT_MAX, jnp.int32)

    @pl.when(jnp.max(tie_rows) > 0)
    def _():
        def idx_body(i, cut):
            cand = cut | lax.shift_left(jnp.int32(1), idx_bits - 1 - i)
            n = count(lambda k, c: jnp.where(k == thr, jnp.where(c * cw + lane < cand, 1, 0), 0))
            return jnp.where(n < need, cand, cut)
        cut_scr[...] = lax.fori_loop(0, idx_bits, idx_body, jnp.zeros((rows, 1), jnp.int32))

    cut = cut_scr[...]

    def bias_chunk(c, carry):
        k = key_scr[c]
        kpos = c * cw + lane
        tie = jnp.where(k == thr, jnp.where(kpos <= cut, 0.0, MASK_NEG), MASK_NEG)
        bias_scr[c] = jnp.where(kpos <= qpos, jnp.where(k > thr, 0.0, tie), MASK_NEG)
        return carry

    lax.fori_loop(0, nvis, bias_chunk, 0)


def _dsa_kernel(q_ref, qi_ref, sm_ref, ksm_ref, k0_ref, k1_ref, v0_ref, v1_ref, o_ref,
                key_scr, bias_scr, cut_scr, m_scr, l_scr, acc_scr, *, n_keep, idx_bits):
    tq = ATT_TQ
    t0 = pl.program_id(1) * tq
    nvis = (t0 + tq - 1) // ATT_KC + 1
    qpos = t0 + lax.broadcasted_iota(jnp.int32, (tq, ATT_KC), 0)
    lane = lax.broadcasted_iota(jnp.int32, (tq, ATT_KC), 1)
    wi = sm_ref[...][:, SM_WI:SM_WI + IDX_HEADS] * (IDX_HEADS ** -0.5)
    qi_all = qi_ref[...]
    qi_h = [qi_all[:, h * IDX_DIM:(h + 1) * IDX_DIM].astype(bf16) for h in range(IDX_HEADS)]

    def score_chunk(c, carry):
        r0 = pl.multiple_of(c * ATT_KC, ATT_KC)
        ki_c = ksm_ref[pl.ds(r0, ATT_KC), :][:, SM_KI:SM_KI + IDX_DIM].astype(bf16)
        key_scr[c] = jnp.where(r0 + lane <= qpos, _order_key(_index_scores(qi_h, wi, ki_c)), INT_MIN)
        return carry

    lax.fori_loop(0, nvis, score_chunk, 0)
    _topk_bias(key_scr, bias_scr, cut_scr, nvis, qpos, n_keep, idx_bits)

    q = q_ref[...]
    for g, (k_ref, v_ref) in enumerate(((k0_ref, v0_ref), (k1_ref, v1_ref))):
        qg = _stack_heads(q, g * DSA_HPG, DSA_HPG, HEAD_DIM ** -0.5)
        o = _flash_chunks(qg, k_ref, v_ref, bias_scr, nvis, m_scr, l_scr, acc_scr, DSA_HPG)
        for h in range(DSA_HPG):
            c0 = (g * DSA_HPG + h) * HEAD_DIM
            o_ref[:, c0:c0 + HEAD_DIM] = o[h * tq:(h + 1) * tq]


def _dsa_prompt(u, n_keep):
    B, T, _ = u.shape
    tq = ATT_TQ
    nck = T // ATT_KC
    col = lambda off, w: off // w
    kv = lambda r, g: pl.BlockSpec((None, T, HEAD_DIM), lambda b, i: (b, 0, col(COL_KVB, HEAD_DIM) + 2 * r + g))
    return pl.pallas_call(
        functools.partial(_dsa_kernel, n_keep=n_keep, idx_bits=max(1, (T - 1).bit_length())),
        grid=(B, T // tq),
        in_specs=[
            pl.BlockSpec((None, tq, DSA_HEADS * HEAD_DIM), lambda b, i: (b, i, col(COL_QB, DSA_HEADS * HEAD_DIM))),
            pl.BlockSpec((None, tq, IDX_HEADS * IDX_DIM), lambda b, i: (b, i, col(COL_QI, IDX_HEADS * IDX_DIM))),
            pl.BlockSpec((None, tq, 128), lambda b, i: (b, i, col(COL_SM, 128))),
            pl.BlockSpec((None, T, 128), lambda b, i: (b, 0, col(COL_SM, 128))),
            kv(0, 0), kv(0, 1), kv(1, 0), kv(1, 1),
        ],
        out_specs=pl.BlockSpec((None, tq, DSA_HEADS * HEAD_DIM), lambda b, i: (b, i, 0)),
        out_shape=jax.ShapeDtypeStruct((B, T, DSA_HEADS * HEAD_DIM), f32),
        scratch_shapes=[
            pltpu.VMEM((nck, tq, ATT_KC), jnp.int32), pltpu.VMEM((nck, tq, ATT_KC), f32),
            pltpu.VMEM((tq, 1), jnp.int32),
            pltpu.VMEM((DSA_HPG * tq, 1), f32), pltpu.VMEM((DSA_HPG * tq, 1), f32),
            pltpu.VMEM((DSA_HPG * tq, HEAD_DIM), f32),
        ],
        compiler_params=pltpu.CompilerParams(
            dimension_semantics=("parallel", "arbitrary"), vmem_limit_bytes=VMEM_LIMIT),
        name="dsa_prompt",
    )(u, u, u, u, u, u, u, u)


def _nsa_compress_kernel(x_ref, w_ref, o_ref, *, nb):
    x = x_ref[...].reshape(nb, NSA_BLOCK, HEAD_DIM)
    o_ref[...] = jnp.zeros_like(o_ref)
    o_ref[0:nb, :] = jnp.sum(x * w_ref[...][None], axis=1)


def _nsa_compress(u, cmp_w, nbp):
    B, T, _ = u.shape
    nb = T // NSA_BLOCK
    return pl.pallas_call(
        functools.partial(_nsa_compress_kernel, nb=nb),
        grid=(B, 2, NSA_KV_HEADS),
        in_specs=[
            pl.BlockSpec((None, T, HEAD_DIM), lambda b, r, g: (b, 0, COL_KVA // HEAD_DIM + 2 * r + g)),
            pl.BlockSpec((None, NSA_BLOCK, HEAD_DIM), lambda b, r, g: (r, 0, 0)),
        ],
        out_specs=pl.BlockSpec((None, None, None, nbp, HEAD_DIM), lambda b, r, g: (b, r, g, 0, 0)),
        out_shape=jax.ShapeDtypeStruct((B, 2, NSA_KV_HEADS, nbp, HEAD_DIM), f32),
        compiler_params=pltpu.CompilerParams(dimension_semantics=("parallel", "parallel", "parallel")),
        name="nsa_compress",
    )(u, cmp_w)


def _nsa_compressed(qg, kc, vc, t0, rows, nb):
    hpg = NSA_HPG
    nbp = kc.shape[0]
    blk4 = lax.broadcasted_iota(jnp.int32, (hpg * rows, nbp), 1)
    qpos4 = t0 + (lax.broadcasted_iota(jnp.int32, (hpg * rows, nbp), 0) & (rows - 1))
    vis4 = (blk4 + 1) * NSA_BLOCK - 1 <= qpos4
    s = jnp.where(vis4, _dot_nt(qg, kc.astype(bf16)), MASK_NEG)
    e = jnp.where(vis4, jnp.exp(s - jnp.max(s, axis=-1, keepdims=True)), 0.0)
    p = e / jnp.maximum(jnp.sum(e, axis=-1, keepdims=True), 1e-30)
    o_cmp = jnp.dot(p.astype(bf16), vc.astype(bf16), preferred_element_type=f32)
    imp = p[0:rows]
    for h in range(1, hpg):
        imp = imp + p[h * rows:(h + 1) * rows]
    blk = lax.broadcasted_iota(jnp.int32, (rows, nbp), 1)
    cur = (t0 + lax.broadcasted_iota(jnp.int32, (rows, nbp), 0)) // NSA_BLOCK
    forced = jnp.where(blk == 0, NSA_FORCE, jnp.where(blk == cur, NSA_FORCE, jnp.where(blk == cur - 1, NSA_FORCE, imp)))
    score = jnp.where(blk > cur, -jnp.inf, forced)
    rank = jnp.zeros((rows, nbp), jnp.int32)
    for b2 in range(nb):
        colv = score[:, b2:b2 + 1]
        rank = rank + jnp.where(colv > score, 1, jnp.where(colv == score, jnp.where(blk > b2, 1, 0), 0))
    sel = jnp.where(blk <= cur, jnp.where(rank < NSA_N_SEL, 1.0, 0.0), 0.0).astype(bf16)
    return o_cmp, sel


def _nsa_selection_bias(sel, bias_scr, nvis, t0):
    rows, nbp = sel.shape
    cw = bias_scr.shape[-1]
    qpos = t0 + lax.broadcasted_iota(jnp.int32, (rows, cw), 0)
    lane = lax.broadcasted_iota(jnp.int32, (rows, cw), 1)
    blk_row = lax.broadcasted_iota(jnp.int32, (nbp, cw), 0)
    key_blk = lax.broadcasted_iota(jnp.int32, (nbp, cw), 1) // NSA_BLOCK

    def bias_chunk(c, carry):
        expand = jnp.where(key_blk + c * (cw // NSA_BLOCK) == blk_row, 1.0, 0.0).astype(bf16)
        selk = jnp.dot(sel, expand, preferred_element_type=f32)
        bias_scr[c] = jnp.where(c * cw + lane <= qpos, jnp.where(selk > 0.5, 0.0, MASK_NEG), MASK_NEG)
        return carry

    lax.fori_loop(0, nvis, bias_chunk, 0)


def _nsa_window(qg, kw, vw, t0, k0, rows):
    span = kw.shape[0]
    kpos = k0 + lax.broadcasted_iota(jnp.int32, (rows, span), 1)
    qpos = t0 + lax.broadcasted_iota(jnp.int32, (rows, span), 0)
    bias = jnp.where(kpos <= qpos, jnp.where(kpos >= qpos - NSA_WINDOW, 0.0, MASK_NEG), MASK_NEG)
    sw = _dot_nt(qg, kw) + jnp.concatenate([bias] * NSA_HPG, axis=0)
    ew = jnp.exp(sw - jnp.max(sw, axis=-1, keepdims=True))
    pw = ew / jnp.sum(ew, axis=-1, keepdims=True)
    return jnp.dot(pw.astype(bf16), vw, preferred_element_type=f32)


def _nsa_kernel(q_ref, sm_ref, kc_ref, vc_ref, ks_ref, vs_ref, kw_ref, vw_ref, o_ref,
                bias_scr, m_scr, l_scr, acc_scr, *, nb, span):
    tq = ATT_TQ
    hpg = NSA_HPG
    g = pl.program_id(1)
    t0 = pl.program_id(2) * tq
    nvis = (t0 + tq - 1) // ATT_KC + 1
    qg = _stack_heads(q_ref[...], 0, hpg, HEAD_DIM ** -0.5)

    o_cmp, sel = _nsa_compressed(qg, kc_ref[...], vc_ref[...], t0, tq, nb)
    _nsa_selection_bias(sel, bias_scr, nvis, t0)
    o_slc = _flash_chunks(qg, ks_ref, vs_ref, bias_scr, nvis, m_scr, l_scr, acc_scr, hpg)

    start = pl.multiple_of(jnp.maximum(t0 + tq - span, 0), tq)
    o_win = _nsa_window(qg, kw_ref[pl.ds(start, span), :].astype(bf16), vw_ref[pl.ds(start, span), :].astype(bf16),
                        t0, start, tq)

    gates = jax.nn.sigmoid(sm_ref[...][:, SM_GA:SM_GA + 3 * NSA_HEADS])
    for h in range(hpg):
        gh = [jnp.where(g == 0, gates[:, h * 3 + j:h * 3 + j + 1],
                        gates[:, 3 * hpg + h * 3 + j:3 * hpg + h * 3 + j + 1]) for j in range(3)]
        rows = slice(h * tq, (h + 1) * tq)
        o_ref[:, h * HEAD_DIM:(h + 1) * HEAD_DIM] = gh[0] * o_cmp[rows] + gh[1] * o_slc[rows] + gh[2] * o_win[rows]


def _nsa_prompt(u, cmp_w):
    B, T, _ = u.shape
    tq = ATT_TQ
    nb = T // NSA_BLOCK
    nbp = _round_up(nb, 128)
    span = min(NSA_WINDOW + tq, T)
    kcv = _nsa_compress(u, cmp_w, nbp)
    gw = NSA_HPG * HEAD_DIM
    kva = lambda r: pl.BlockSpec((None, T, HEAD_DIM), lambda b, g, i: (b, 0, COL_KVA // HEAD_DIM + 2 * r + g))
    kc = lambda r: pl.BlockSpec((None, None, None, nbp, HEAD_DIM), lambda b, g, i: (b, r, g, 0, 0))
    nck = T // ATT_KC
    return pl.pallas_call(
        functools.partial(_nsa_kernel, nb=nb, span=span),
        grid=(B, NSA_KV_HEADS, T // tq),
        in_specs=[
            pl.BlockSpec((None, tq, gw), lambda b, g, i: (b, i, COL_QA // gw + g)),
            pl.BlockSpec((None, tq, 128), lambda b, g, i: (b, i, COL_SM // 128)),
            kc(0), kc(1), kva(2), kva(3), kva(4), kva(5),
        ],
        out_specs=pl.BlockSpec((None, tq, gw), lambda b, g, i: (b, i, g)),
        out_shape=jax.ShapeDtypeStruct((B, T, NSA_HEADS * HEAD_DIM), f32),
        scratch_shapes=[
            pltpu.VMEM((nck, tq, ATT_KC), f32),
            pltpu.VMEM((NSA_HPG * tq, 1), f32), pltpu.VMEM((NSA_HPG * tq, 1), f32),
            pltpu.VMEM((NSA_HPG * tq, HEAD_DIM), f32),
        ],
        compiler_params=pltpu.CompilerParams(
            dimension_semantics=("parallel", "parallel", "arbitrary"), vmem_limit_bytes=VMEM_LIMIT),
        name="nsa_prompt",
    )(u, u, kcv, kcv, u, u, u, u)


SMP_ROWS = 8
SMP_NPG = 8


def _pad_rows(x, n):
    return jnp.concatenate([x, jnp.zeros((n - x.shape[0], x.shape[1]), x.dtype)], axis=0)


def _dsa_sample_kernel(pt_ref, us_ref, *refs, n_steps, past, n_keep, idx_bits):
    npg = SMP_NPG
    idx_refs, kv_refs, o_ref = refs[:npg], refs[npg:2 * npg], refs[2 * npg]
    key_scr, bias_scr, cut_scr, k_scr, v_scr, m_scr, l_scr, acc_scr = refs[2 * npg + 1:]
    rows = SMP_ROWS
    cw = npg * PAGE_SIZE
    s = pl.program_id(1)
    us = us_ref[...]
    wi = us[:, COL_SM + SM_WI:COL_SM + SM_WI + IDX_HEADS] * (IDX_HEADS ** -0.5)
    qi_h = [us[:, COL_QI + h * IDX_DIM:COL_QI + (h + 1) * IDX_DIM].astype(bf16) for h in range(IDX_HEADS)]

    tiles = []
    for j in range(npg):
        tiles.append(_index_scores(qi_h, wi, idx_refs[j][...].astype(bf16), keys_on_lanes=True))
        r0 = pl.multiple_of((s * npg + j) * PAGE_SIZE, PAGE_SIZE)
        for g in range(DSA_KV_HEADS):
            per_pos = 2 * DSA_KV_HEADS
            k_scr[g, pl.ds(r0, PAGE_SIZE), :] = kv_refs[j][pl.ds(g, PAGE_SIZE, stride=per_pos), :].astype(bf16)
            v_scr[g, pl.ds(r0, PAGE_SIZE), :] = kv_refs[j][pl.ds(DSA_KV_HEADS + g, PAGE_SIZE, stride=per_pos), :].astype(bf16)
    key_scr[s] = _order_key(jnp.concatenate(tiles, axis=1))

    @pl.when(s == n_steps - 1)
    def _():
        row = lax.broadcasted_iota(jnp.int32, (rows, cw), 0)
        lane = lax.broadcasted_iota(jnp.int32, (rows, cw), 1)
        ki_new = _pad_rows(us[:, COL_SM + SM_KI:COL_SM + SM_KI + IDX_DIM], cw).astype(bf16)
        key_scr[n_steps] = jnp.where(lane <= row, _order_key(_index_scores(qi_h, wi, ki_new)), INT_MIN)
        for g in range(DSA_KV_HEADS):
            kcol = COL_KVB + g * HEAD_DIM
            vcol = COL_KVB + (DSA_KV_HEADS + g) * HEAD_DIM
            k_scr[g, pl.ds(past, cw), :] = _pad_rows(us[:, kcol:kcol + HEAD_DIM], cw).astype(bf16)
            v_scr[g, pl.ds(past, cw), :] = _pad_rows(us[:, vcol:vcol + HEAD_DIM], cw).astype(bf16)
        _topk_bias(key_scr, bias_scr, cut_scr, n_steps + 1, past + row, n_keep, idx_bits)
        for g in range(DSA_KV_HEADS):
            qg = _stack_heads(us[:, COL_QB:COL_KVA], g * DSA_HPG, DSA_HPG, HEAD_DIM ** -0.5)
            o = _flash_chunks(qg, k_scr.at[g], v_scr.at[g], bias_scr, n_steps + 1, m_scr, l_scr, acc_scr, DSA_HPG)
            for h in range(DSA_HPG):
                c0 = (g * DSA_HPG + h) * HEAD_DIM
                o_ref[:, c0:c0 + HEAD_DIM] = o[h * rows:(h + 1) * rows]


def _page_specs(rows, width, layer):
    return [pl.BlockSpec((None, None, rows, width),
                         lambda b, s, pt, j=j: (layer, pt[b, s * SMP_NPG + j], 0, 0)) for j in range(SMP_NPG)]


def _page_rows_view(pool):
    return pool.reshape(pool.shape[:2] + (-1, pool.shape[-1]))


def _dsa_sample(us, idx_pool, kv_pool, page_table, layer, t_new):
    B = us.shape[0]
    rows = SMP_ROWS
    n_pages = page_table.shape[1]
    past = n_pages * PAGE_SIZE
    n_steps = n_pages // SMP_NPG
    cw = SMP_NPG * PAGE_SIZE
    width = DSA_HEADS * HEAD_DIM
    grid_spec = pltpu.PrefetchScalarGridSpec(
        num_scalar_prefetch=1,
        grid=(B, n_steps),
        in_specs=([pl.BlockSpec((None, rows, us.shape[-1]), lambda b, s, pt: (b, 0, 0))]
                  + _page_specs(IDX_DIM, PAGE_SIZE, layer)
                  + _page_specs(PAGE_SIZE * 2 * DSA_KV_HEADS, HEAD_DIM, layer)),
        out_specs=pl.BlockSpec((None, rows, width), lambda b, s, pt: (b, 0, 0)),
        scratch_shapes=[
            pltpu.VMEM((n_steps + 1, rows, cw), jnp.int32), pltpu.VMEM((n_steps + 1, rows, cw), f32),
            pltpu.VMEM((rows, 1), jnp.int32),
            pltpu.VMEM((DSA_KV_HEADS, past + cw, HEAD_DIM), bf16), pltpu.VMEM((DSA_KV_HEADS, past + cw, HEAD_DIM), bf16),
            pltpu.VMEM((DSA_HPG * rows, 1), f32), pltpu.VMEM((DSA_HPG * rows, 1), f32),
            pltpu.VMEM((DSA_HPG * rows, HEAD_DIM), f32),
        ])
    return pl.pallas_call(
        functools.partial(_dsa_sample_kernel, n_steps=n_steps, past=past,
                          n_keep=min(DSA_TOPK, (past + t_new) // 4), idx_bits=(past + cw - 1).bit_length()),
        grid_spec=grid_spec,
        out_shape=jax.ShapeDtypeStruct((B, rows, width), f32),
        compiler_params=pltpu.CompilerParams(
            dimension_semantics=("parallel", "arbitrary"), vmem_limit_bytes=VMEM_LIMIT),
        name="dsa_sample",
    )(page_table, us, *([idx_pool] * SMP_NPG), *([kv_pool] * SMP_NPG))


def _nsa_sample_kernel(pt_ref, us_ref, cw_ref, win_ref, *refs, n_steps, past, t_new):
    npg = SMP_NPG
    pages, o_ref = refs[:npg], refs[npg]
    kc_scr, vc_scr, ks_scr, vs_scr, bias_scr, m_scr, l_scr, acc_scr = refs[npg + 1:]
    rows = SMP_ROWS
    hpg = NSA_HPG
    G = NSA_KV_HEADS
    cw = npg * PAGE_SIZE
    bpp = PAGE_SIZE // NSA_BLOCK
    bpc = cw // NSA_BLOCK
    s = pl.program_id(1)
    w = cw_ref[...]

    @pl.when(s == 0)
    def _():
        kc_scr[...] = jnp.zeros_like(kc_scr)
        vc_scr[...] = jnp.zeros_like(vc_scr)

    def page_rows(j, r, g):
        return pages[j][pl.ds(r * G + g, PAGE_SIZE, stride=4 * G), :]

    for g in range(G):
        kcs, vcs = [], []
        for j in range(npg):
            r0 = pl.multiple_of((s * npg + j) * PAGE_SIZE, PAGE_SIZE)
            kcs.append(jnp.sum(page_rows(j, 0, g).reshape(bpp, NSA_BLOCK, HEAD_DIM) * w[0][None], axis=1))
            vcs.append(jnp.sum(page_rows(j, 1, g).reshape(bpp, NSA_BLOCK, HEAD_DIM) * w[1][None], axis=1))
            ks_scr[g, pl.ds(r0, PAGE_SIZE), :] = page_rows(j, 2, g).astype(bf16)
            vs_scr[g, pl.ds(r0, PAGE_SIZE), :] = page_rows(j, 3, g).astype(bf16)
        b0 = pl.multiple_of(s * bpc, bpc)
        kc_scr[g, pl.ds(b0, bpc), :] = jnp.concatenate(kcs, axis=0)
        vc_scr[g, pl.ds(b0, bpc), :] = jnp.concatenate(vcs, axis=0)

    @pl.when(s == n_steps - 1)
    def _():
        us = us_ref[...]
        win = win_ref[...]
        wk = win.shape[0]
        span = _round_up(wk + rows, 128)
        valid = lax.broadcasted_iota(jnp.int32, (rows, HEAD_DIM), 0) < t_new
        gates = jax.nn.sigmoid(us[:, COL_SM + SM_GA:COL_SM + SM_GA + 3 * NSA_HEADS])
        for g in range(G):
            def new(r):
                c0 = COL_KVA + (r * G + g) * HEAD_DIM
                return us[:, c0:c0 + HEAD_DIM]
            kc_new = jnp.sum(jnp.where(valid, new(0) * w[0][0:rows], 0.0), axis=0, keepdims=True)
            vc_new = jnp.sum(jnp.where(valid, new(1) * w[1][0:rows], 0.0), axis=0, keepdims=True)
            kc_scr[g, pl.ds(n_steps * bpc, rows), :] = _pad_rows(kc_new, rows)
            vc_scr[g, pl.ds(n_steps * bpc, rows), :] = _pad_rows(vc_new, rows)
            ks_scr[g, pl.ds(past, cw), :] = _pad_rows(new(2), cw).astype(bf16)
            vs_scr[g, pl.ds(past, cw), :] = _pad_rows(new(3), cw).astype(bf16)
            qg = _stack_heads(us[:, COL_QA:COL_QB], g * hpg, hpg, HEAD_DIM ** -0.5)
            o_cmp, sel = _nsa_compressed(qg, kc_scr[g], vc_scr[g], past, rows, past // NSA_BLOCK + 1)
            _nsa_selection_bias(sel, bias_scr, n_steps + 1, past)
            o_slc = _flash_chunks(qg, ks_scr.at[g], vs_scr.at[g], bias_scr, n_steps + 1, m_scr, l_scr, acc_scr, hpg)
            kw = _pad_rows(jnp.concatenate([win[:, g * HEAD_DIM:(g + 1) * HEAD_DIM], new(4)], axis=0), span)
            vw = _pad_rows(jnp.concatenate([win[:, (G + g) * HEAD_DIM:(G + g + 1) * HEAD_DIM], new(5)], axis=0), span)
            o_win = _nsa_window(qg, kw.astype(bf16), vw.astype(bf16), past, past - wk, rows)
            for h in range(hpg):
                c = (g * hpg + h) * 3
                rws = slice(h * rows, (h + 1) * rows)
                o_ref[:, (g * hpg + h) * HEAD_DIM:(g * hpg + h + 1) * HEAD_DIM] = (
                    gates[:, c:c + 1] * o_cmp[rws] + gates[:, c + 1:c + 2] * o_slc[rws] + gates[:, c + 2:c + 3] * o_win[rws])


def _nsa_sample(us, cmp_w, win_buf, nsa_pool, page_table, layer, t_new):
    B = us.shape[0]
    rows = SMP_ROWS
    n_pages = page_table.shape[1]
    past = n_pages * PAGE_SIZE
    n_steps = n_pages // SMP_NPG
    cw = SMP_NPG * PAGE_SIZE
    nbp = _round_up((n_steps + 1) * (cw // NSA_BLOCK), 128)
    wk = win_buf.shape[2]
    width = NSA_HEADS * HEAD_DIM
    grid_spec = pltpu.PrefetchScalarGridSpec(
        num_scalar_prefetch=1,
        grid=(B, n_steps),
        in_specs=([pl.BlockSpec((None, rows, us.shape[-1]), lambda b, s, pt: (b, 0, 0)),
                   pl.BlockSpec((2, NSA_BLOCK, HEAD_DIM), lambda b, s, pt: (0, 0, 0)),
                   pl.BlockSpec((None, None, wk, win_buf.shape[-1]), lambda b, s, pt: (layer, b, 0, 0))]
                  + _page_specs(PAGE_SIZE * 4 * NSA_KV_HEADS, HEAD_DIM, layer)),
        out_specs=pl.BlockSpec((None, rows, width), lambda b, s, pt: (b, 0, 0)),
        scratch_shapes=[
            pltpu.VMEM((NSA_KV_HEADS, nbp, HEAD_DIM), f32), pltpu.VMEM((NSA_KV_HEADS, nbp, HEAD_DIM), f32),
            pltpu.VMEM((NSA_KV_HEADS, past + cw, HEAD_DIM), bf16), pltpu.VMEM((NSA_KV_HEADS, past + cw, HEAD_DIM), bf16),
            pltpu.VMEM((n_steps + 1, rows, cw), f32),
            pltpu.VMEM((NSA_HPG * rows, 1), f32), pltpu.VMEM((NSA_HPG * rows, 1), f32),
            pltpu.VMEM((NSA_HPG * rows, HEAD_DIM), f32),
        ])
    return pl.pallas_call(
        functools.partial(_nsa_sample_kernel, n_steps=n_steps, past=past, t_new=t_new),
        grid_spec=grid_spec,
        out_shape=jax.ShapeDtypeStruct((B, rows, width), f32),
        compiler_params=pltpu.CompilerParams(
            dimension_semantics=("parallel", "arbitrary"), vmem_limit_bytes=VMEM_LIMIT),
        name="nsa_sample",
    )(page_table, us, cmp_w, win_buf, *([nsa_pool] * SMP_NPG))


REC_CHUNK = 64
REC_TC = 512
CONV_PAD = 8


def _dot_b(a, b):
    return jnp.dot(a.astype(bf16), b.astype(bf16), preferred_element_type=f32)


def _dot_nt_b(a, b):
    return _dot_nt(a.astype(bf16), b.astype(bf16))


def _dot_tn_b(a, b):
    return _dot_b(a.T, b)


def _dot_3(a, b):
    ah, bh = a.astype(bf16), b.astype(bf16)
    al, bl = (a - ah.astype(f32)).astype(bf16), (b - bh.astype(f32)).astype(bf16)
    return (jnp.dot(ah, bh, preferred_element_type=f32) + jnp.dot(ah, bl, preferred_element_type=f32)
            + jnp.dot(al, bh, preferred_element_type=f32))


def _softplus(x):
    return jnp.maximum(x, 0.0) + jnp.log(1.0 + jnp.exp(-jnp.abs(x)))


def _silu(x):
    return x * jax.nn.sigmoid(x)


def _conv_block(xbuf_scr, raw, w, first, conv0):
    tc = raw.shape[0]
    nh = CONV_WIDTH - 1

    @pl.when(first)
    def _():
        xbuf_scr[CONV_PAD - nh:CONV_PAD, :] = conv0

    xbuf_scr[CONV_PAD:CONV_PAD + tc, :] = raw
    y = xbuf_scr[pl.ds(CONV_PAD - nh, tc), :] * w[0:1, :]
    for j in range(1, CONV_WIDTH):
        y = y + xbuf_scr[pl.ds(CONV_PAD - nh + j, tc), :] * w[j:j + 1, :]
    xbuf_scr[CONV_PAD - nh:CONV_PAD, :] = raw[tc - nh:tc, :]
    return y


def _lane_pick(x, idx):
    lane = lax.broadcasted_iota(jnp.int32, x.shape, 1)
    return jnp.sum(jnp.where(lane == idx, x, 0.0), axis=-1, keepdims=True)


def _chunk_cumsum(col, row):
    c = col.shape[0]
    i = lax.broadcasted_iota(jnp.int32, (c, c), 0)
    j = lax.broadcasted_iota(jnp.int32, (c, c), 1)
    ccol = jnp.sum(jnp.where(j <= i, jnp.broadcast_to(row, (c, c)), 0.0), axis=1, keepdims=True)
    crow = jnp.sum(jnp.where(i <= j, jnp.broadcast_to(col, (c, c)), 0.0), axis=0, keepdims=True)
    return ccol, crow


def _gdn_kernel(alog_ref, dtb_ref, q_ref, k_ref, v_ref, z_ref, sm_ref, smt_ref, wq_ref, wk_ref, wv_ref,
                cq_ref, ck_ref, cv_ref, s0_ref, ng_ref, o_ref, s_out_ref, xbuf_scr, s_scr, *, t_valid):
    tc = q_ref.shape[0]
    C = REC_CHUNK
    nc = tc // C
    hp = pl.program_id(1)
    blk = pl.program_id(2)
    first = blk == 0

    @pl.when(first)
    def _():
        s_scr[...] = s0_ref[...]

    raw = jnp.concatenate([q_ref[...], k_ref[...], v_ref[...]], axis=1)
    w = jnp.concatenate([wq_ref[...], wk_ref[...], wv_ref[...]], axis=1)
    conv0 = jnp.concatenate([cq_ref[...], ck_ref[...], cv_ref[...]], axis=1)
    y = _silu(_conv_block(xbuf_scr, raw, w, first, conv0))
    valid_col = (blk * tc + lax.broadcasted_iota(jnp.int32, (tc, 1), 0)) < t_valid
    valid_row = (blk * tc + lax.broadcasted_iota(jnp.int32, (1, tc), 1)) < t_valid
    sm = sm_ref[...]
    ii = lax.broadcasted_iota(jnp.int32, (C, C), 0)
    jj = lax.broadcasted_iota(jnp.int32, (C, C), 1)
    eye = jnp.where(ii == jj, 1.0, 0.0)

    pieces = []
    for e in range(2):
        h = 2 * hp + e
        qf = y[:, e * GDN_DK:(e + 1) * GDN_DK]
        kf = y[:, (2 + e) * GDN_DK:(3 + e) * GDN_DK]
        vf = y[:, (4 + e) * GDN_DK:(5 + e) * GDN_DK]
        q = jnp.where(valid_col, qf * lax.rsqrt(jnp.sum(qf * qf, axis=-1, keepdims=True) + 1e-6) * (GDN_DK ** -0.5), 0.0)
        k = jnp.where(valid_col, kf * lax.rsqrt(jnp.sum(kf * kf, axis=-1, keepdims=True) + 1e-6), 0.0)
        v = jnp.where(valid_col, vf, 0.0)
        a_scale = -jnp.exp(jnp.full((1, 1), alog_ref[h], f32))
        dtb = dtb_ref[h]
        beta = jnp.where(valid_col, jax.nn.sigmoid(_lane_pick(sm, OSM_BETA + h)), 0.0)
        g_col = jnp.where(valid_col, a_scale * _softplus(_lane_pick(sm, OSM_A + h) + dtb), 0.0)
        g_row = jnp.where(valid_row, a_scale * _softplus(smt_ref[pl.ds(OSM_A + h, 1), :] + dtb), 0.0)
        for c in range(nc):
            r = slice(c * C, (c + 1) * C)
            gc_col, gc_row = _chunk_cumsum(g_col[r], g_row[:, r])
            gam = jnp.where(jj <= ii, jnp.exp(jnp.where(jj <= ii, gc_col - gc_row, 0.0)), 0.0)
            kb = k[r] * beta[r]
            pieces.append(dict(q=q[r], k=k[r], vb=v[r] * beta[r], kb=kb, gc=gc_col, gam=gam,
                               x=jnp.where(jj < ii, -(_dot_nt_b(kb, k[r]) * gam), 0.0)))

    xs = [p['x'] for p in pieces]
    tms = [eye + x for x in xs]
    for _ in range(max(1, (C - 1).bit_length()) - 1):
        xs = [_dot_3(x, x) for x in xs]
        tms = [tm + _dot_3(tm, x) for tm, x in zip(tms, xs)]
    for p, tm in zip(pieces, tms):
        p['u0'] = _dot_b(tm, p['vb'])
        p['wd'] = _dot_b(tm, p['kb'] * jnp.exp(p['gc']))
        p['qk'] = _dot_nt_b(p['q'], p['k']) * p['gam']

    S = [s_scr[0], s_scr[1]]
    outs = [[], []]
    for c in range(nc):
        for e in range(2):
            p = pieces[e * nc + c]
            g_last = p['gc'][C - 1:C, :]
            u = p['u0'] - _dot_b(p['wd'], S[e])
            outs[e].append(_dot_b(p['q'] * jnp.exp(p['gc']), S[e]) + _dot_b(p['qk'], u))
            S[e] = S[e] * jnp.exp(g_last) + _dot_tn_b(p['k'] * jnp.exp(g_last - p['gc']), u)
    z = z_ref[...]
    halves = []
    for e in range(2):
        s_scr[e] = S[e]
        o = jnp.concatenate(outs[e], axis=0) if nc > 1 else outs[e][0]
        o = o * lax.rsqrt(jnp.mean(o * o, axis=-1, keepdims=True) + NORM_EPS) * ng_ref[...]
        halves.append(o * _silu(z[:, e * GDN_DV:(e + 1) * GDN_DV]))
    o_ref[...] = jnp.concatenate(halves, axis=1)

    @pl.when(blk == pl.num_programs(2) - 1)
    def _():
        s_out_ref[...] = s_scr[...]


def _smem_spec():
    return pl.BlockSpec(memory_space=pltpu.SMEM)


def _gdn(u, smt, conv_w, conv0, s0, a_log, dt_bias, norm_g, layer, t_valid):
    B, Tp, _ = u.shape
    tc = min(REC_TC, Tp)
    H = GDN_HEADS
    HP = H // 2
    pw = 2 * GDN_DK
    qkv = lambda part: pl.BlockSpec((None, tc, pw), lambda b, hp, i: (b, i, OC_QKV // pw + part * HP + hp))
    cw = lambda part: pl.BlockSpec((None, CONV_WIDTH, pw), lambda b, hp, i: (layer, 0, part * HP + hp))
    c0 = lambda part: pl.BlockSpec((None, CONV_WIDTH - 1, pw), lambda b, hp, i: (b, 0, part * HP + hp))
    return pl.pallas_call(
        functools.partial(_gdn_kernel, t_valid=t_valid),
        grid=(B, HP, Tp // tc),
        in_specs=[
            _smem_spec(), _smem_spec(),
            qkv(0), qkv(1), qkv(2),
            pl.BlockSpec((None, tc, pw), lambda b, hp, i: (b, i, OC_ZC // pw + hp)),
            pl.BlockSpec((None, tc, 128), lambda b, hp, i: (b, i, OC_SM // 128)),
            pl.BlockSpec((None, 128, tc), lambda b, hp, i: (b, 0, i)),
            cw(0), cw(1), cw(2), c0(0), c0(1), c0(2),
            pl.BlockSpec((None, 2, GDN_DK, GDN_DV), lambda b, hp, i: (b, hp, 0, 0)),
            pl.BlockSpec((1, GDN_DV), lambda b, hp, i: (0, 0)),
        ],
        out_specs=[
            pl.BlockSpec((None, tc, pw), lambda b, hp, i: (b, i, hp)),
            pl.BlockSpec((None, 2, GDN_DK, GDN_DV), lambda b, hp, i: (b, hp, 0, 0)),
        ],
        out_shape=[jax.ShapeDtypeStruct((B, Tp, H * GDN_DV), f32),
                   jax.ShapeDtypeStruct((B, H, GDN_DK, GDN_DV), f32)],
        scratch_shapes=[pltpu.VMEM((CONV_PAD + tc, 3 * pw), f32), pltpu.VMEM((2, GDN_DK, GDN_DV), f32)],
        compiler_params=pltpu.CompilerParams(
            dimension_semantics=("parallel", "parallel", "arbitrary"), vmem_limit_bytes=VMEM_LIMIT),
        name="gdn_scan",
    )(a_log, dt_bias, u, u, u, u, u, smt, conv_w, conv_w, conv_w, conv0, conv0, conv0, s0, norm_g.reshape(1, -1))


def _ssd_kernel(alog_ref, dtb_ref, d_ref, x_ref, b_ref, c_ref, z_ref, sm_ref, smt_ref, wx_ref, wb_ref, wc_ref,
                bx_ref, bb_ref, bc_ref, cx_ref, cb_ref, cc_ref, h0_ref, o_ref, h_out_ref, xbuf_scr, h_scr,
                *, t_valid):
    tc = x_ref.shape[0]
    C = REC_CHUNK
    P = SSD_HEAD_DIM
    hp = pl.program_id(1)
    blk = pl.program_id(2)
    first = blk == 0

    @pl.when(first)
    def _():
        h_scr[...] = h0_ref[...]

    raw = jnp.concatenate([x_ref[...], b_ref[...], c_ref[...]], axis=1)
    w = jnp.concatenate([wx_ref[...], wb_ref[...], wc_ref[...]], axis=1)
    bias = jnp.concatenate([bx_ref[...], bb_ref[...], bc_ref[...]], axis=1)
    conv0 = jnp.concatenate([cx_ref[...], cb_ref[...], cc_ref[...]], axis=1)
    y = _silu(_conv_block(xbuf_scr, raw, w, first, conv0) + bias)
    valid_col = (blk * tc + lax.broadcasted_iota(jnp.int32, (tc, 1), 0)) < t_valid
    valid_row = (blk * tc + lax.broadcasted_iota(jnp.int32, (1, tc), 1)) < t_valid
    xs = jnp.where(valid_col, y[:, 0:128], 0.0)
    bm = jnp.where(valid_col, y[:, 128:256], 0.0)
    cm = jnp.where(valid_col, y[:, 256:384], 0.0)
    sm = sm_ref[...]
    z = z_ref[...]
    ii = lax.broadcasted_iota(jnp.int32, (C, C), 0)
    jj = lax.broadcasted_iota(jnp.int32, (C, C), 1)
    cbs = [_dot_nt_b(cm[c * C:(c + 1) * C], bm[c * C:(c + 1) * C]) for c in range(tc // C)]
    halves = []
    for e in range(2):
        hh = 2 * hp + e
        a_neg = -jnp.exp(jnp.full((1, 1), alog_ref[hh], f32))
        dtb = dtb_ref[hh]
        dt_col = jnp.where(valid_col, _softplus(_lane_pick(sm, OSM_DT + hh) + dtb), 0.0)
        dt_row = jnp.where(valid_row, _softplus(smt_ref[pl.ds(OSM_DT + hh, 1), :] + dtb), 0.0)
        xh = xs[:, e * P:(e + 1) * P]
        xdt = xh * dt_col
        hst = h_scr[e]
        outs = []
        for c in range(tc // C):
            r = slice(c * C, (c + 1) * C)
            acs_col, acs_row = _chunk_cumsum(dt_col[r] * a_neg, dt_row[:, r] * a_neg)
            lm = jnp.where(jj <= ii, jnp.exp(jnp.where(jj <= ii, acs_col - acs_row, 0.0)), 0.0)
            a_last = acs_col[C - 1:C, :]
            y_diag = _dot_b(cbs[c] * lm, xdt[r])
            y_off = _dot_nt_b(cm[r] * jnp.exp(acs_col), hst)
            outs.append(y_diag + y_off)
            hst = hst * jnp.exp(a_last) + _dot_tn_b(xdt[r] * jnp.exp(a_last - acs_col), bm[r])
        h_scr[e] = hst
        yh = jnp.concatenate(outs, axis=0) if len(outs) > 1 else outs[0]
        halves.append((yh + d_ref[hh] * xh) * _silu(z[:, e * P:(e + 1) * P]))
    o_ref[...] = jnp.concatenate(halves, axis=1)

    @pl.when(blk == pl.num_programs(2) - 1)
    def _():
        h_out_ref[...] = h_scr[...]


def _ssd(u, smt, conv_w, conv_b, conv0, h0, a_log, dt_bias, d_skip, layer, t_valid):
    B, Tp, _ = u.shape
    tc = min(REC_TC, Tp)
    HP = SSD_HEADS // 2
    hpg = SSD_HEADS // SSD_GROUPS // 2
    xcol = lambda hp: hp
    bcol = lambda hp: SSD_D_INNER // 128 + hp // hpg
    ccol = lambda hp: (SSD_D_INNER + SSD_GROUPS * SSD_STATE) // 128 + hp // hpg
    def tri(fn_col):
        return (pl.BlockSpec((None, tc, 128), lambda b, hp, i: (b, i, OC_XBC // 128 + fn_col(hp))),
                pl.BlockSpec((None, CONV_WIDTH, 128), lambda b, hp, i: (layer, 0, fn_col(hp))),
                pl.BlockSpec((None, 1, 128), lambda b, hp, i: (layer, 0, fn_col(hp))),
                pl.BlockSpec((None, CONV_WIDTH - 1, 128), lambda b, hp, i: (b, 0, fn_col(hp))))
    (xs, wx, bx, cx), (bs, wb, bb, cb), (cs, wc, bc, cc) = tri(xcol), tri(bcol), tri(ccol)
    return pl.pallas_call(
        functools.partial(_ssd_kernel, t_valid=t_valid),
        grid=(B, HP, Tp // tc),
        in_specs=[
            _smem_spec(), _smem_spec(), _smem_spec(),
            xs, bs, cs,
            pl.BlockSpec((None, tc, 128), lambda b, hp, i: (b, i, OC_ZD // 128 + hp)),
            pl.BlockSpec((None, tc, 128), lambda b, hp, i: (b, i, OC_SM // 128)),
            pl.BlockSpec((None, 128, tc), lambda b, hp, i: (b, 0, i)),
            wx, wb, wc, bx, bb, bc, cx, cb, cc,
            pl.BlockSpec((None, 2, SSD_HEAD_DIM, SSD_STATE), lambda b, hp, i: (b, hp, 0, 0)),
        ],
        out_specs=[
            pl.BlockSpec((None, tc, 128), lambda b, hp, i: (b, i, hp)),
            pl.BlockSpec((None, 2, SSD_HEAD_DIM, SSD_STATE), lambda b, hp, i: (b, hp, 0, 0)),
        ],
        out_shape=[jax.ShapeDtypeStruct((B, Tp, SSD_D_INNER), f32),
                   jax.ShapeDtypeStruct((B, SSD_HEADS, SSD_HEAD_DIM, SSD_STATE), f32)],
        scratch_shapes=[pltpu.VMEM((CONV_PAD + tc, 3 * 128), f32), pltpu.VMEM((2, SSD_HEAD_DIM, SSD_STATE), f32)],
        compiler_params=pltpu.CompilerParams(
            dimension_semantics=("parallel", "parallel", "arbitrary"), vmem_limit_bytes=VMEM_LIMIT),
        name="ssd_scan",
    )(a_log, dt_bias, d_skip, u, u, u, u, u, smt, conv_w, conv_w, conv_w, conv_b, conv_b, conv_b,
      conv0, conv0, conv0, h0)


def _split(u, sizes):
    return jnp.split(u, np.cumsum(sizes)[:-1].tolist(), axis=-1)


def _even_mixer(u, cmp_w, cache, i):
    B, T, _ = u.shape
    kva = u[..., COL_KVA:COL_KVB].reshape(B, T, 6, NSA_KV_HEADS, HEAD_DIM)
    kvb = u[..., COL_KVB:COL_QI].reshape(B, T, 2, DSA_KV_HEADS, HEAD_DIM)
    ki = u[..., COL_SM + SM_KI:COL_SM + SM_GA]
    nsa_rows, win_rows = kva[:, :, :4], kva[:, :, 4:]
    if cache is None:
        o_a = _nsa_prompt(u, cmp_w)
        o_b = _dsa_prompt(u, min(DSA_TOPK, T // 4))
        return (o_a, o_b), (nsa_rows, kvb, ki, win_rows[:, T - min(NSA_WINDOW, T):])
    flat = lambda a: a.reshape(a.shape[:3] + (-1,))
    us = jnp.pad(u, ((0, 0), (0, SMP_ROWS - T), (0, 0)))
    pt = cache['page_table']
    o_a = _nsa_sample(us, cmp_w, flat(cache['nsa_win']), _page_rows_view(cache['nsa_kv']), pt, i, T)[:, :T]
    idx_t = jnp.swapaxes(cache['dsa_idx_k'], 2, 3)
    o_b = _dsa_sample(us, idx_t, _page_rows_view(cache['dsa_kv']), pt, i, T)[:, :T]
    new_win = jnp.concatenate([cache['nsa_win'][i], win_rows], axis=1)[:, T:]
    return (o_a, o_b), (nsa_rows, kvb, ki, new_win)


def _odd_mixer(u, W, i, init, t_valid):
    S0, conv_c0, h0, conv_d0 = init
    nr = W['ssd_conv_b'].shape[0]
    smt = jnp.swapaxes(u[..., OC_SM:OC_SM + 128], 1, 2)
    o_c, S = _gdn(u, smt, W['gdn_conv_w'], conv_c0, S0, W['gdn_a_log'][i], W['gdn_dt_bias'][i],
                  W['gdn_norm_g'][i], i, t_valid)
    y, hN = _ssd(u, smt, W['ssd_conv_w'], W['ssd_conv_b'].reshape(nr, 1, -1), conv_d0, h0, W['ssd_a_log'][i],
                 W['ssd_dt_bias'][i], W['ssd_d'][i], i, t_valid)

    def conv_state(buf, lo, width):
        rows = u[:, :t_valid, lo:lo + width]
        nh = CONV_WIDTH - 1
        return rows[:, t_valid - nh:] if t_valid >= nh else jnp.concatenate([buf, rows], axis=1)[:, -nh:]

    return (o_c[:, :t_valid], y[:, :t_valid]), (S, conv_state(conv_c0, OC_QKV, GDN_CONV_DIM), hN,
                                               conv_state(conv_d0, OC_XBC, SSD_CONV_DIM))


def _trunk(x, W, cache):
    B, T, D = x.shape
    x = x.reshape(B * T, D)
    attn_new, rec_new = [], []
    for l in range(DEPTH):
        i = l // 2
        x = _ffn(x, W['norm_g'][l, 0], W['ffn_wg'], W['ffn_wu'], W['ffn_wd'], l, 0)
        if l % 2 == 0:
            u = _inproj(x, W['norm_g'][l, 1], W['attn_w_in'], i).reshape(B, T, -1)
            (o0, o1), st = _even_mixer(u, W['nsa_cmp_w'][i], cache, i)
            attn_new.append(st)
            x = _outproj(x, o0.reshape(B * T, -1), o1.reshape(B * T, -1), W['attn_w_out'], i)
        else:
            u = _inproj(x, W['norm_g'][l, 1], W['rec_w_in'], i).reshape(B, T, -1)
            u = jnp.pad(u, ((0, 0), (0, _round_up(T, REC_CHUNK) - T), (0, 0)))
            if cache is None:
                init = (jnp.zeros((B, GDN_HEADS, GDN_DK, GDN_DV), f32),
                        jnp.zeros((B, CONV_WIDTH - 1, GDN_CONV_DIM), f32),
                        jnp.zeros((B, SSD_HEADS, SSD_HEAD_DIM, SSD_STATE), f32),
                        jnp.zeros((B, CONV_WIDTH - 1, SSD_CONV_DIM), f32))
            else:
                init = (cache['gdn'][i], cache['gdn_conv'][i], cache['ssd'][i], cache['ssd_conv'][i])
            (o0, o1), st = _odd_mixer(u, W, i, init, T)
            rec_new.append(st)
            x = _outproj(x, o0.reshape(B * T, -1), o1.reshape(B * T, -1), W['rec_w_out'], i,
                         g1=W['ssd_norm_g'][i], norm_groups=SSD_GROUPS)
        x = _ffn(x, W['norm_g'][l, 2], W['ffn_wg'], W['ffn_wu'], W['ffn_wd'], l, 1,
                 final_g=W['final_norm_g'] if l == DEPTH - 1 else None)
    y = x.reshape(B, T, D)

    def stack(lst, j):
        return jnp.stack([s[j] for s in lst])

    return (y, stack(attn_new, 0), stack(attn_new, 1), stack(attn_new, 2), stack(attn_new, 3),
            stack(rec_new, 0), stack(rec_new, 1), stack(rec_new, 2), stack(rec_new, 3))


def _pad_cols(w, n):
    return jnp.pad(w, ((0, 0), (0, 0), (0, n - w.shape[-1])))


def _pack_even_w(w):
    qa, kva, ga, qb, kvb, qi, ki, wi = _split(w, EVEN_SIZES)
    return _pad_cols(jnp.concatenate([qa, qb, kva, kvb, qi, ki, ga, wi], axis=-1), _round_up(EVEN_IN, PROJ_TN))


def _pack_odd_w(w):
    qkv, beta, a, zc, zd, xbc, dt = _split(w, ODD_SIZES)
    return _pad_cols(jnp.concatenate([qkv, zc, zd, xbc, beta, a, dt], axis=-1), _round_up(OC_SM + 128, PROJ_TN))


def kernel(x_prompt, x_sample, cache_nsa_kv, cache_dsa_kv, cache_dsa_idx_k, page_table, state_nsa_win,
           state_gdn, state_gdn_conv, state_ssd, state_ssd_conv, norm_g, final_norm_g, ffn_w_gate, ffn_w_up,
           ffn_w_down, attn_w_in, attn_w_out, nsa_cmp_w, rec_w_in, rec_w_out, gdn_conv_w, gdn_a_log,
           gdn_dt_bias, gdn_norm_g, ssd_conv_w, ssd_conv_b, ssd_dt_bias, ssd_a_log, ssd_d, ssd_norm_g):
    W = {'norm_g': norm_g, 'final_norm_g': final_norm_g,
         'ffn_wg': ffn_w_gate.astype(bf16), 'ffn_wu': ffn_w_up.astype(bf16), 'ffn_wd': ffn_w_down.astype(bf16),
         'attn_w_in': _pack_even_w(attn_w_in).astype(bf16),
         'attn_w_out': attn_w_out.astype(bf16), 'nsa_cmp_w': nsa_cmp_w,
         'rec_w_in': _pack_odd_w(rec_w_in).astype(bf16),
         'rec_w_out': rec_w_out.astype(bf16), 'gdn_conv_w': gdn_conv_w, 'gdn_a_log': gdn_a_log,
         'gdn_dt_bias': gdn_dt_bias, 'gdn_norm_g': gdn_norm_g, 'ssd_conv_w': ssd_conv_w, 'ssd_conv_b': ssd_conv_b,
         'ssd_dt_bias': ssd_dt_bias, 'ssd_a_log': ssd_a_log, 'ssd_d': ssd_d, 'ssd_norm_g': ssd_norm_g}
    cache = {'nsa_kv': cache_nsa_kv, 'dsa_kv': cache_dsa_kv, 'dsa_idx_k': cache_dsa_idx_k,
             'page_table': page_table, 'nsa_win': state_nsa_win, 'gdn': state_gdn, 'gdn_conv': state_gdn_conv,
             'ssd': state_ssd, 'ssd_conv': state_ssd_conv}
    (y_prompt, p_nsa_kv, p_dsa_kv, p_dsa_idx_k, p_nsa_win,
     p_gdn, p_gdn_conv, p_ssd, p_ssd_conv) = _trunk(x_prompt, W, None)
    (y_sample, s_nsa_kv, s_dsa_kv, s_dsa_idx_k, s_nsa_win,
     s_gdn, s_gdn_conv, s_ssd, s_ssd_conv) = _trunk(x_sample, W, cache)
    return (y_prompt, y_sample, p_nsa_kv, p_dsa_kv, p_dsa_idx_k, p_nsa_win, p_gdn, p_gdn_conv, p_ssd, p_ssd_conv,
            s_nsa_kv, s_dsa_kv, s_dsa_idx_k, s_nsa_win, s_gdn, s_gdn_conv, s_ssd, s_ssd_conv)
```

```python
import functools
import math

import jax
import jax.numpy as jnp
import numpy as np
from jax import lax
from jax.experimental import pallas as pl
from jax.experimental.pallas import tpu as pltpu

D_MODEL = 2048
DEPTH = 4
PAGE_SIZE = 128
HEAD_DIM = 128
NSA_HEADS = 8
NSA_KV_HEADS = 2
NSA_HPG = NSA_HEADS // NSA_KV_HEADS
NSA_BLOCK = 64
NSA_N_SEL = 16
NSA_WINDOW = 512
NSA_FORCE = 1.0e4
SLC_Q_BLOCK = 64
DSA_HEADS = 8
DSA_KV_HEADS = 2
DSA_HPG = DSA_HEADS // DSA_KV_HEADS
IDX_HEADS = 8
IDX_DIM = 64
DSA_TOPK = 256
Q_BLOCK = 128
GDN_HEADS = 8
GDN_DK = 128
GDN_DV = 128
GDN_CHUNK = 64
CONV_WIDTH = 4
SSD_D_INNER = D_MODEL // 2
SSD_HEAD_DIM = 64
SSD_HEADS = SSD_D_INNER // SSD_HEAD_DIM
SSD_GROUPS = 2
SSD_STATE = 128
SSD_CHUNK = 64
D_FF = 5632
NORM_EPS = 1e-6

EVEN_SIZES = (NSA_HEADS * HEAD_DIM, 6 * NSA_KV_HEADS * HEAD_DIM, 3 * NSA_HEADS,
              DSA_HEADS * HEAD_DIM, 2 * DSA_KV_HEADS * HEAD_DIM,
              IDX_HEADS * IDX_DIM, IDX_DIM, IDX_HEADS)
EVEN_IN = sum(EVEN_SIZES)
GDN_CONV_DIM = GDN_HEADS * (2 * GDN_DK + GDN_DV)
SSD_CONV_DIM = SSD_D_INNER + 2 * SSD_GROUPS * SSD_STATE
ODD_SIZES = (GDN_CONV_DIM, GDN_HEADS, GDN_HEADS, GDN_HEADS * GDN_DV,
             SSD_D_INNER, SSD_CONV_DIM, SSD_HEADS)
ODD_IN = sum(ODD_SIZES)

COL_QA = 0
COL_QB = COL_QA + NSA_HEADS * HEAD_DIM
COL_KVA = COL_QB + DSA_HEADS * HEAD_DIM
COL_KVB = COL_KVA + 6 * NSA_KV_HEADS * HEAD_DIM
COL_QI = COL_KVB + 2 * DSA_KV_HEADS * HEAD_DIM
COL_SM = COL_QI + IDX_HEADS * IDX_DIM
SM_KI = 0
SM_GA = SM_KI + IDX_DIM
SM_WI = SM_GA + 3 * NSA_HEADS
OC_QKV = 0
OC_ZC = OC_QKV + GDN_CONV_DIM
OC_ZD = OC_ZC + GDN_HEADS * GDN_DV
OC_XBC = OC_ZD + SSD_D_INNER
OC_SM = OC_XBC + SSD_CONV_DIM
OSM_BETA = 0
OSM_A = OSM_BETA + GDN_HEADS
OSM_DT = OSM_A + GDN_HEADS

V7X_VMEM_BYTES = 64 * 1024 * 1024
VMEM_LIMIT = V7X_VMEM_BYTES - 12 * 1024 * 1024
PROJ_TN = 1024
FFN_TF = 512

bf16 = jnp.bfloat16
f32 = jnp.float32


def _round_up(n, m):
    return -(-n // m) * m


def _row_tile(m):
    return 512 if m % 512 == 0 else m


def _rms(x, g):
    return x * lax.rsqrt(jnp.mean(x * x, axis=-1, keepdims=True) + NORM_EPS) * g


def _ffn_kernel(x_ref, g_ref, wg_ref, wu_ref, wd_ref, fg_ref, o_ref, h_scr, acc_scr, *, final):
    f = pl.program_id(1)

    @pl.when(f == 0)
    def _():
        h_scr[...] = _rms(x_ref[...], g_ref[...]).astype(bf16)
        acc_scr[...] = jnp.zeros_like(acc_scr)

    h = h_scr[...]
    gate = jnp.dot(h, wg_ref[...], preferred_element_type=f32)
    up = jnp.dot(h, wu_ref[...], preferred_element_type=f32)
    act = (gate * jax.nn.sigmoid(gate) * up).astype(bf16)
    acc_scr[...] += jnp.dot(act, wd_ref[...], preferred_element_type=f32)

    @pl.when(f == pl.num_programs(1) - 1)
    def _():
        y = x_ref[...] + 0.5 * acc_scr[...]
        if final:
            y = _rms(y, fg_ref[...])
        o_ref[...] = y


def _ffn(x, g, wg, wu, wd, l, j, final_g=None):
    m, d = x.shape
    tm = _row_tile(m)
    final = final_g is not None
    fg = final_g if final else g
    return pl.pallas_call(
        functools.partial(_ffn_kernel, final=final),
        grid=(m // tm, D_FF // FFN_TF),
        in_specs=[
            pl.BlockSpec((tm, d), lambda i, f: (i, 0)),
            pl.BlockSpec((1, d), lambda i, f: (0, 0)),
            pl.BlockSpec((None, None, d, FFN_TF), lambda i, f: (l, j, 0, f)),
            pl.BlockSpec((None, None, d, FFN_TF), lambda i, f: (l, j, 0, f)),
            pl.BlockSpec((None, None, FFN_TF, d), lambda i, f: (l, j, f, 0)),
            pl.BlockSpec((1, d), lambda i, f: (0, 0)),
        ],
        out_specs=pl.BlockSpec((tm, d), lambda i, f: (i, 0)),
        out_shape=jax.ShapeDtypeStruct((m, d), f32),
        scratch_shapes=[pltpu.VMEM((tm, d), bf16), pltpu.VMEM((tm, d), f32)],
        compiler_params=pltpu.CompilerParams(
            dimension_semantics=("parallel", "arbitrary"), vmem_limit_bytes=VMEM_LIMIT),
        name="ffn_half",
    )(x, g.reshape(1, d), wg, wu, wd, fg.reshape(1, d))


def _inproj_kernel(x_ref, g_ref, w_ref, o_ref, h_scr):
    @pl.when(pl.program_id(1) == 0)
    def _():
        h_scr[...] = _rms(x_ref[...], g_ref[...]).astype(bf16)

    o_ref[...] = jnp.dot(h_scr[...], w_ref[...], preferred_element_type=f32)


def _inproj(x, g, w, i):
    m, d = x.shape
    n = w.shape[-1]
    tm = _row_tile(m)
    return pl.pallas_call(
        _inproj_kernel,
        grid=(m // tm, n // PROJ_TN),
        in_specs=[
            pl.BlockSpec((tm, d), lambda r, c: (r, 0)),
            pl.BlockSpec((1, d), lambda r, c: (0, 0)),
            pl.BlockSpec((None, d, PROJ_TN), lambda r, c: (i, 0, c)),
        ],
        out_specs=pl.BlockSpec((tm, PROJ_TN), lambda r, c: (r, c)),
        out_shape=jax.ShapeDtypeStruct((m, n), f32),
        scratch_shapes=[pltpu.VMEM((tm, d), bf16)],
        compiler_params=pltpu.CompilerParams(
            dimension_semantics=("parallel", "arbitrary"), vmem_limit_bytes=VMEM_LIMIT),
        name="mixer_in_proj",
    )(x, g.reshape(1, d), w)


def _outproj_kernel(x_ref, a0_ref, a1_ref, w0_ref, w1_ref, g1_ref, o_ref, *, norm_groups):
    a1 = a1_ref[...]
    if norm_groups:
        gw = a1.shape[-1] // norm_groups
        a1 = jnp.concatenate([_rms(a1[:, j * gw:(j + 1) * gw], g1_ref[:, j * gw:(j + 1) * gw])
                              for j in range(norm_groups)], axis=1)
    o_ref[...] = (x_ref[...]
                  + jnp.dot(a0_ref[...].astype(bf16), w0_ref[...], preferred_element_type=f32)
                  + jnp.dot(a1.astype(bf16), w1_ref[...], preferred_element_type=f32))


def _outproj(x, a0, a1, w, i, g1=None, norm_groups=0):
    m, d = x.shape
    k = a0.shape[-1]
    tm = _row_tile(m)
    g1 = jnp.ones((k,), f32) if g1 is None else g1
    return pl.pallas_call(
        functools.partial(_outproj_kernel, norm_groups=norm_groups),
        grid=(m // tm, d // PROJ_TN),
        in_specs=[
            pl.BlockSpec((tm, PROJ_TN), lambda r, c: (r, c)),
            pl.BlockSpec((tm, k), lambda r, c: (r, 0)),
            pl.BlockSpec((tm, k), lambda r, c: (r, 0)),
            pl.BlockSpec((None, k, PROJ_TN), lambda r, c: (i, 0, c)),
            pl.BlockSpec((None, k, PROJ_TN), lambda r, c: (i, 1, c)),
            pl.BlockSpec((1, k), lambda r, c: (0, 0)),
        ],
        out_specs=pl.BlockSpec((tm, PROJ_TN), lambda r, c: (r, c)),
        out_shape=jax.ShapeDtypeStruct((m, d), f32),
        compiler_params=pltpu.CompilerParams(
            dimension_semantics=("parallel", "arbitrary"), vmem_limit_bytes=VMEM_LIMIT),
        name="mixer_out_proj",
    )(x, a0, a1, w, w, g1.reshape(1, k))


ATT_TQ = 128
ATT_KC = 1024
MASK_NEG = -1e30
INT_MIN = -2 ** 31
INT_MAX = 2 ** 31 - 1


def _stack_heads(q, first, n, scale):
    rows = jnp.concatenate([q[:, (first + h) * HEAD_DIM:(first + h + 1) * HEAD_DIM] for h in range(n)], axis=0)
    return (rows * scale).astype(bf16)


def _dot_nt(a, b):
    return lax.dot_general(a, b, (((1,), (1,)), ((), ())), preferred_element_type=f32)


def _flash_chunks(qg, k_ref, v_ref, bias_scr, nvis, m_scr, l_scr, acc_scr, reps):
    m_scr[...] = jnp.full_like(m_scr, MASK_NEG)
    l_scr[...] = jnp.zeros_like(l_scr)
    acc_scr[...] = jnp.zeros_like(acc_scr)
    cw = bias_scr.shape[-1]

    def scores(c):
        r0 = pl.multiple_of(c * cw, cw)
        return _dot_nt(qg, k_ref[pl.ds(r0, cw), :].astype(bf16))

    def update(c, s):
        r0 = pl.multiple_of(c * cw, cw)
        vc = v_ref[pl.ds(r0, cw), :].astype(bf16)
        s = s + jnp.concatenate([bias_scr[c]] * reps, axis=0)
        m_old = m_scr[...]
        m_new = jnp.maximum(m_old, jnp.max(s, axis=-1, keepdims=True))
        alpha = jnp.exp(m_old - m_new)
        p = jnp.exp(s - m_new)
        l_scr[...] = alpha * l_scr[...] + jnp.sum(p, axis=-1, keepdims=True)
        acc_scr[...] = alpha * acc_scr[...] + jnp.dot(p.astype(bf16), vc, preferred_element_type=f32)
        m_scr[...] = m_new

    def body(c, s):
        s_next = scores(c + 1)
        update(c, s)
        return s_next

    s_last = lax.fori_loop(0, nvis - 1, body, scores(0))
    update(nvis - 1, s_last)
    return acc_scr[...] / jnp.maximum(l_scr[...], 1e-30)


def _index_scores(qi_h, wi, ki, keys_on_lanes=False):
    n = ki.shape[1] if keys_on_lanes else ki.shape[0]
    acc = jnp.zeros((qi_h[0].shape[0], n), f32)
    for h in range(IDX_HEADS):
        qk = jnp.dot(qi_h[h], ki, preferred_element_type=f32) if keys_on_lanes else _dot_nt(qi_h[h], ki)
        rel = jnp.maximum(qk * (IDX_DIM ** -0.5), 0.0)
        acc = acc + rel * wi[:, h:h + 1]
    return acc


def _order_key(x):
    bits = lax.bitcast_convert_type(x, jnp.int32)
    return jnp.where(bits < 0, bits ^ INT_MAX, bits)


def _topk_bias(key_scr, bias_scr, cut_scr, nvis, qpos, n_keep, idx_bits):
    rows, cw = qpos.shape
    lane = lax.broadcasted_iota(jnp.int32, (rows, cw), 1)

    def count(indicator):
        def body(c, acc):
            one = indicator(key_scr[c], c)
            for j in range(cw // 128):
                acc = acc + one[:, j * 128:(j + 1) * 128]
            return acc
        acc = lax.fori_loop(0, nvis, body, jnp.zeros((rows, 128), jnp.int32))
        return jnp.sum(acc, axis=-1, keepdims=True)

    thr = jnp.where(count(lambda k, c: jnp.where(k >= 0, 1, 0)) >= n_keep, 0, INT_MIN)

    def bit_body(i, thr):
        cand = thr | lax.shift_left(jnp.int32(1), 30 - i)
        return jnp.where(count(lambda k, c: jnp.where(k >= cand, 1, 0)) >= n_keep, cand, thr)

    thr = lax.fori_loop(0, 31, bit_body, thr)
    n_gt = count(lambda k, c: jnp.where(k > thr, 1, 0))
    n_ge = count(lambda k, c: jnp.where(k >= thr, 1, 0))
    need = n_keep - n_gt
    tie_rows = jnp.where(n_ge - n_gt > need, jnp.where(thr > INT_MIN, 1, 0), 0)
    cut_scr[...] = jnp.full((rows, 1), INT_MAX, jnp.int32)

    @pl.when(jnp.max(tie_rows) > 0)
    def _():
        def idx_body(i, cut):
            cand = cut | lax.shift_left(jnp.int32(1), idx_bits - 1 - i)
            n = count(lambda k, c: jnp.where(k == thr, jnp.where(c * cw + lane < cand, 1, 0), 0))
            return jnp.where(n < need, cand, cut)
        cut_scr[...] = lax.fori_loop(0, idx_bits, idx_body, jnp.zeros((rows, 1), jnp.int32))

    cut = cut_scr[...]

    def bias_chunk(c, carry):
        k = key_scr[c]
        kpos = c * cw + lane
        tie = jnp.where(k == thr, jnp.where(kpos <= cut, 0.0, MASK_NEG), MASK_NEG)
        bias_scr[c] = jnp.where(kpos <= qpos, jnp.where(k > thr, 0.0, tie), MASK_NEG)
        return carry

    lax.fori_loop(0, nvis, bias_chunk, 0)


def _dsa_kernel(q_ref, qi_ref, sm_ref, ksm_ref, k0_ref, k1_ref, v0_ref, v1_ref, o_ref,
                key_scr, bias_scr, cut_scr, m_scr, l_scr, acc_scr, *, n_keep, idx_bits):
    tq = ATT_TQ
    t0 = pl.program_id(1) * tq
    nvis = (t0 + tq - 1) // ATT_KC + 1
    qpos = t0 + lax.broadcasted_iota(jnp.int32, (tq, ATT_KC), 0)
    lane = lax.broadcasted_iota(jnp.int32, (tq, ATT_KC), 1)
    wi = sm_ref[...][:, SM_WI:SM_WI + IDX_HEADS] * (IDX_HEADS ** -0.5)
    qi_all = qi_ref[...]
    qi_h = [qi_all[:, h * IDX_DIM:(h + 1) * IDX_DIM].astype(bf16) for h in range(IDX_HEADS)]

    def score_chunk(c, carry):
        r0 = pl.multiple_of(c * ATT_KC, ATT_KC)
        ki_c = ksm_ref[pl.ds(r0, ATT_KC), :][:, SM_KI:SM_KI + IDX_DIM].astype(bf16)
        key_scr[c] = jnp.where(r0 + lane <= qpos, _order_key(_index_scores(qi_h, wi, ki_c)), INT_MIN)
        return carry

    lax.fori_loop(0, nvis, score_chunk, 0)
    _topk_bias(key_scr, bias_scr, cut_scr, nvis, qpos, n_keep, idx_bits)

    q = q_ref[...]
    for g, (k_ref, v_ref) in enumerate(((k0_ref, v0_ref), (k1_ref, v1_ref))):
        qg = _stack_heads(q, g * DSA_HPG, DSA_HPG, HEAD_DIM ** -0.5)
        o = _flash_chunks(qg, k_ref, v_ref, bias_scr, nvis, m_scr, l_scr, acc_scr, DSA_HPG)
        for h in range(DSA_HPG):
            c0 = (g * DSA_HPG + h) * HEAD_DIM
            o_ref[:, c0:c0 + HEAD_DIM] = o[h * tq:(h + 1) * tq]


def _dsa_prompt(u, n_keep):
    B, T, _ = u.shape
    tq = ATT_TQ
    nck = T // ATT_KC
    col = lambda off, w: off // w
    kv = lambda r, g: pl.BlockSpec((None, T, HEAD_DIM), lambda b, i: (b, 0, col(COL_KVB, HEAD_DIM) + 2 * r + g))
    return pl.pallas_call(
        functools.partial(_dsa_kernel, n_keep=n_keep, idx_bits=max(1, (T - 1).bit_length())),
        grid=(B, T // tq),
        in_specs=[
            pl.BlockSpec((None, tq, DSA_HEADS * HEAD_DIM), lambda b, i: (b, i, col(COL_QB, DSA_HEADS * HEAD_DIM))),
            pl.BlockSpec((None, tq, IDX_HEADS * IDX_DIM), lambda b, i: (b, i, col(COL_QI, IDX_HEADS * IDX_DIM))),
            pl.BlockSpec((None, tq, 128), lambda b, i: (b, i, col(COL_SM, 128))),
            pl.BlockSpec((None, T, 128), lambda b, i: (b, 0, col(COL_SM, 128))),
            kv(0, 0), kv(0, 1), kv(1, 0), kv(1, 1),
        ],
        out_specs=pl.BlockSpec((None, tq, DSA_HEADS * HEAD_DIM), lambda b, i: (b, i, 0)),
        out_shape=jax.ShapeDtypeStruct((B, T, DSA_HEADS * HEAD_DIM), f32),
        scratch_shapes=[
            pltpu.VMEM((nck, tq, ATT_KC), jnp.int32), pltpu.VMEM((nck, tq, ATT_KC), f32),
            pltpu.VMEM((tq, 1), jnp.int32),
            pltpu.VMEM((DSA_HPG * tq, 1), f32), pltpu.VMEM((DSA_HPG * tq, 1), f32),
            pltpu.VMEM((DSA_HPG * tq, HEAD_DIM), f32),
        ],
        compiler_params=pltpu.CompilerParams(
            dimension_semantics=("parallel", "arbitrary"), vmem_limit_bytes=VMEM_LIMIT),
        name="dsa_prompt",
    )(u, u, u, u, u, u, u, u)


def _nsa_compress_kernel(x_ref, w_ref, o_ref, *, nb):
    x = x_ref[...].reshape(nb, NSA_BLOCK, HEAD_DIM)
    o_ref[...] = jnp.zeros_like(o_ref)
    o_ref[0:nb, :] = jnp.sum(x * w_ref[...][None], axis=1)


def _nsa_compress(u, cmp_w, nbp):
    B, T, _ = u.shape
    nb = T // NSA_BLOCK
    return pl.pallas_call(
        functools.partial(_nsa_compress_kernel, nb=nb),
        grid=(B, 2, NSA_KV_HEADS),
        in_specs=[
            pl.BlockSpec((None, T, HEAD_DIM), lambda b, r, g: (b, 0, COL_KVA // HEAD_DIM + 2 * r + g)),
            pl.BlockSpec((None, NSA_BLOCK, HEAD_DIM), lambda b, r, g: (r, 0, 0)),
        ],
        out_specs=pl.BlockSpec((None, None, None, nbp, HEAD_DIM), lambda b, r, g: (b, r, g, 0, 0)),
        out_shape=jax.ShapeDtypeStruct((B, 2, NSA_KV_HEADS, nbp, HEAD_DIM), f32),
        compiler_params=pltpu.CompilerParams(dimension_semantics=("parallel", "parallel", "parallel")),
        name="nsa_compress",
    )(u, cmp_w)


def _nsa_compressed(qg, kc, vc, t0, rows, nb):
    hpg = NSA_HPG
    nbp = kc.shape[0]
    blk4 = lax.broadcasted_iota(jnp.int32, (hpg * rows, nbp), 1)
    qpos4 = t0 + (lax.broadcasted_iota(jnp.int32, (hpg * rows, nbp), 0) & (rows - 1))
    vis4 = (blk4 + 1) * NSA_BLOCK - 1 <= qpos4
    s = jnp.where(vis4, _dot_nt(qg, kc.astype(bf16)), MASK_NEG)
    e = jnp.where(vis4, jnp.exp(s - jnp.max(s, axis=-1, keepdims=True)), 0.0)
    p = e / jnp.maximum(jnp.sum(e, axis=-1, keepdims=True), 1e-30)
    o_cmp = jnp.dot(p.astype(bf16), vc.astype(bf16), preferred_element_type=f32)
    imp = p[0:rows]
    for h in range(1, hpg):
        imp = imp + p[h * rows:(h + 1) * rows]
    blk = lax.broadcasted_iota(jnp.int32, (rows, nbp), 1)
    cur = (t0 + lax.broadcasted_iota(jnp.int32, (rows, nbp), 0)) // NSA_BLOCK
    forced = jnp.where(blk == 0, NSA_FORCE, jnp.where(blk == cur, NSA_FORCE, jnp.where(blk == cur - 1, NSA_FORCE, imp)))
    score = jnp.where(blk > cur, -jnp.inf, forced)
    if rows % 128 == 0 and nbp == 128:
        nbr = _round_up(nb, 8)
        st = score.T[0:nbr]
        blk_t = lax.broadcasted_iota(jnp.int32, (nbr, rows), 0)
        cur_t = (t0 + lax.broadcasted_iota(jnp.int32, (nbr, rows), 1)) // NSA_BLOCK
        rank_t = jnp.zeros((nbr, rows), jnp.int32)
        for b2 in range(nb):
            rowv = st[b2:b2 + 1, :]
            rank_t = rank_t + jnp.where(rowv > st, 1, jnp.where(rowv == st, jnp.where(blk_t > b2, 1, 0), 0))
        sel_t = jnp.where(blk_t <= cur_t, jnp.where(rank_t < NSA_N_SEL, 1.0, 0.0), 0.0)
        sel = jnp.concatenate([sel_t, jnp.zeros((nbp - nbr, rows), f32)], axis=0).T.astype(bf16)
        return o_cmp, sel
    rank = jnp.zeros((rows, nbp), jnp.int32)
    for b2 in range(nb):
        colv = score[:, b2:b2 + 1]
        rank = rank + jnp.where(colv > score, 1, jnp.where(colv == score, jnp.where(blk > b2, 1, 0), 0))
    sel = jnp.where(blk <= cur, jnp.where(rank < NSA_N_SEL, 1.0, 0.0), 0.0).astype(bf16)
    return o_cmp, sel


def _nsa_selection_bias(sel, bias_scr, nvis, t0):
    rows, nbp = sel.shape
    cw = bias_scr.shape[-1]
    qpos = t0 + lax.broadcasted_iota(jnp.int32, (rows, cw), 0)
    lane = lax.broadcasted_iota(jnp.int32, (rows, cw), 1)
    blk_row = lax.broadcasted_iota(jnp.int32, (nbp, cw), 0)
    key_blk = lax.broadcasted_iota(jnp.int32, (nbp, cw), 1) // NSA_BLOCK

    def bias_chunk(c, carry):
        expand = jnp.where(key_blk + c * (cw // NSA_BLOCK) == blk_row, 1.0, 0.0).astype(bf16)
        selk = jnp.dot(sel, expand, preferred_element_type=f32)
        bias_scr[c] = jnp.where(c * cw + lane <= qpos, jnp.where(selk > 0.5, 0.0, MASK_NEG), MASK_NEG)
        return carry

    lax.fori_loop(0, nvis, bias_chunk, 0)


def _nsa_window(qg, kw, vw, t0, k0, rows):
    span = kw.shape[0]
    kpos = k0 + lax.broadcasted_iota(jnp.int32, (rows, span), 1)
    qpos = t0 + lax.broadcasted_iota(jnp.int32, (rows, span), 0)
    bias = jnp.where(kpos <= qpos, jnp.where(kpos >= qpos - NSA_WINDOW, 0.0, MASK_NEG), MASK_NEG)
    sw = _dot_nt(qg, kw) + jnp.concatenate([bias] * NSA_HPG, axis=0)
    ew = jnp.exp(sw - jnp.max(sw, axis=-1, keepdims=True))
    pw = ew / jnp.sum(ew, axis=-1, keepdims=True)
    return jnp.dot(pw.astype(bf16), vw, preferred_element_type=f32)


def _nsa_kernel(q_ref, sm_ref, kc_ref, vc_ref, ks_ref, vs_ref, kw_ref, vw_ref, o_ref,
                bias_scr, m_scr, l_scr, acc_scr, *, nb, span):
    tq = ATT_TQ
    hpg = NSA_HPG
    g = pl.program_id(1)
    t0 = pl.program_id(2) * tq
    nvis = (t0 + tq - 1) // ATT_KC + 1
    qg = _stack_heads(q_ref[...], 0, hpg, HEAD_DIM ** -0.5)

    o_cmp, sel = _nsa_compressed(qg, kc_ref[...], vc_ref[...], t0, tq, nb)
    _nsa_selection_bias(sel, bias_scr, nvis, t0)
    o_slc = _flash_chunks(qg, ks_ref, vs_ref, bias_scr, nvis, m_scr, l_scr, acc_scr, hpg)

    start = pl.multiple_of(jnp.maximum(t0 + tq - span, 0), tq)
    o_win = _nsa_window(qg, kw_ref[pl.ds(start, span), :].astype(bf16), vw_ref[pl.ds(start, span), :].astype(bf16),
                        t0, start, tq)

    gates = jax.nn.sigmoid(sm_ref[...][:, SM_GA:SM_GA + 3 * NSA_HEADS])
    for h in range(hpg):
        gh = [jnp.where(g == 0, gates[:, h * 3 + j:h * 3 + j + 1],
                        gates[:, 3 * hpg + h * 3 + j:3 * hpg + h * 3 + j + 1]) for j in range(3)]
        rows = slice(h * tq, (h + 1) * tq)
        o_ref[:, h * HEAD_DIM:(h + 1) * HEAD_DIM] = gh[0] * o_cmp[rows] + gh[1] * o_slc[rows] + gh[2] * o_win[rows]


def _nsa_prompt(u, cmp_w):
    B, T, _ = u.shape
    tq = ATT_TQ
    nb = T // NSA_BLOCK
    nbp = _round_up(nb, 128)
    span = min(NSA_WINDOW + tq, T)
    kcv = _nsa_compress(u, cmp_w, nbp)
    gw = NSA_HPG * HEAD_DIM
    kva = lambda r: pl.BlockSpec((None, T, HEAD_DIM), lambda b, g, i: (b, 0, COL_KVA // HEAD_DIM + 2 * r + g))
    kc = lambda r: pl.BlockSpec((None, None, None, nbp, HEAD_DIM), lambda b, g, i: (b, r, g, 0, 0))
    nck = T // ATT_KC
    return pl.pallas_call(
        functools.partial(_nsa_kernel, nb=nb, span=span),
        grid=(B, NSA_KV_HEADS, T // tq),
        in_specs=[
            pl.BlockSpec((None, tq, gw), lambda b, g, i: (b, i, COL_QA // gw + g)),
            pl.BlockSpec((None, tq, 128), lambda b, g, i: (b, i, COL_SM // 128)),
            kc(0), kc(1), kva(2), kva(3), kva(4), kva(5),
        ],
        out_specs=pl.BlockSpec((None, tq, gw), lambda b, g, i: (b, i, g)),
        out_shape=jax.ShapeDtypeStruct((B, T, NSA_HEADS * HEAD_DIM), f32),
        scratch_shapes=[
            pltpu.VMEM((nck, tq, ATT_KC), f32),
            pltpu.VMEM((NSA_HPG * tq, 1), f32), pltpu.VMEM((NSA_HPG * tq, 1), f32),
            pltpu.VMEM((NSA_HPG * tq, HEAD_DIM), f32),
        ],
        compiler_params=pltpu.CompilerParams(
            dimension_semantics=("parallel", "parallel", "arbitrary"), vmem_limit_bytes=VMEM_LIMIT),
        name="nsa_prompt",
    )(u, u, kcv, kcv, u, u, u, u)


SMP_ROWS = 8
SMP_NPG = 8


def _pad_rows(x, n):
    return jnp.concatenate([x, jnp.zeros((n - x.shape[0], x.shape[1]), x.dtype)], axis=0)


def _dsa_sample_kernel(pt_ref, us_ref, *refs, n_steps, past, n_keep, idx_bits):
    npg = SMP_NPG
    idx_refs, kv_refs, o_ref = refs[:npg], refs[npg:2 * npg], refs[2 * npg]
    key_scr, bias_scr, cut_scr, k_scr, v_scr, m_scr, l_scr, acc_scr = refs[2 * npg + 1:]
    rows = SMP_ROWS
    cw = npg * PAGE_SIZE
    s = pl.program_id(1)
    us = us_ref[...]
    wi = us[:, COL_SM + SM_WI:COL_SM + SM_WI + IDX_HEADS] * (IDX_HEADS ** -0.5)
    qi_h = [us[:, COL_QI + h * IDX_DIM:COL_QI + (h + 1) * IDX_DIM].astype(bf16) for h in range(IDX_HEADS)]

    ki_t = jnp.concatenate([idx_refs[j][...] for j in range(npg)], axis=1).astype(bf16)
    key_scr[s] = _order_key(_index_scores(qi_h, wi, ki_t, keys_on_lanes=True))
    for j in range(npg):
        r0 = pl.multiple_of((s * npg + j) * PAGE_SIZE, PAGE_SIZE)
        for g in range(DSA_KV_HEADS):
            per_pos = 2 * DSA_KV_HEADS
            k_scr[g, pl.ds(r0, PAGE_SIZE), :] = kv_refs[j][pl.ds(g, PAGE_SIZE, stride=per_pos), :].astype(bf16)
            v_scr[g, pl.ds(r0, PAGE_SIZE), :] = kv_refs[j][pl.ds(DSA_KV_HEADS + g, PAGE_SIZE, stride=per_pos), :].astype(bf16)

    @pl.when(s == n_steps - 1)
    def _():
        row = lax.broadcasted_iota(jnp.int32, (rows, cw), 0)
        lane = lax.broadcasted_iota(jnp.int32, (rows, cw), 1)
        ki_new = _pad_rows(us[:, COL_SM + SM_KI:COL_SM + SM_KI + IDX_DIM], cw).astype(bf16)
        key_scr[n_steps] = jnp.where(lane <= row, _order_key(_index_scores(qi_h, wi, ki_new)), INT_MIN)
        for g in range(DSA_KV_HEADS):
            kcol = COL_KVB + g * HEAD_DIM
            vcol = COL_KVB + (DSA_KV_HEADS + g) * HEAD_DIM
            k_scr[g, pl.ds(past, cw), :] = _pad_rows(us[:, kcol:kcol + HEAD_DIM], cw).astype(bf16)
            v_scr[g, pl.ds(past, cw), :] = _pad_rows(us[:, vcol:vcol + HEAD_DIM], cw).astype(bf16)
        _topk_bias(key_scr, bias_scr, cut_scr, n_steps + 1, past + row, n_keep, idx_bits)
        for g in range(DSA_KV_HEADS):
            qg = _stack_heads(us[:, COL_QB:COL_KVA], g * DSA_HPG, DSA_HPG, HEAD_DIM ** -0.5)
            o = _flash_chunks(qg, k_scr.at[g], v_scr.at[g], bias_scr, n_steps + 1, m_scr, l_scr, acc_scr, DSA_HPG)
            for h in range(DSA_HPG):
                c0 = (g * DSA_HPG + h) * HEAD_DIM
                o_ref[:, c0:c0 + HEAD_DIM] = o[h * rows:(h + 1) * rows]


def _page_specs(rows, width, layer):
    return [pl.BlockSpec((None, None, rows, width),
                         lambda b, s, pt, j=j: (layer, pt[b, s * SMP_NPG + j], 0, 0)) for j in range(SMP_NPG)]


def _page_rows_view(pool):
    return pool.reshape(pool.shape[:2] + (-1, pool.shape[-1]))


def _dsa_sample(us, idx_pool, kv_pool, page_table, layer, t_new):
    B = us.shape[0]
    rows = SMP_ROWS
    n_pages = page_table.shape[1]
    past = n_pages * PAGE_SIZE
    n_steps = n_pages // SMP_NPG
    cw = SMP_NPG * PAGE_SIZE
    width = DSA_HEADS * HEAD_DIM
    grid_spec = pltpu.PrefetchScalarGridSpec(
        num_scalar_prefetch=1,
        grid=(B, n_steps),
        in_specs=([pl.BlockSpec((None, rows, us.shape[-1]), lambda b, s, pt: (b, 0, 0))]
                  + _page_specs(IDX_DIM, PAGE_SIZE, layer)
                  + _page_specs(PAGE_SIZE * 2 * DSA_KV_HEADS, HEAD_DIM, layer)),
        out_specs=pl.BlockSpec((None, rows, width), lambda b, s, pt: (b, 0, 0)),
        scratch_shapes=[
            pltpu.VMEM((n_steps + 1, rows, cw), jnp.int32), pltpu.VMEM((n_steps + 1, rows, cw), f32),
            pltpu.VMEM((rows, 1), jnp.int32),
            pltpu.VMEM((DSA_KV_HEADS, past + cw, HEAD_DIM), bf16), pltpu.VMEM((DSA_KV_HEADS, past + cw, HEAD_DIM), bf16),
            pltpu.VMEM((DSA_HPG * rows, 1), f32), pltpu.VMEM((DSA_HPG * rows, 1), f32),
            pltpu.VMEM((DSA_HPG * rows, HEAD_DIM), f32),
        ])
    return pl.pallas_call(
        functools.partial(_dsa_sample_kernel, n_steps=n_steps, past=past,
                          n_keep=min(DSA_TOPK, (past + t_new) // 4), idx_bits=(past + cw - 1).bit_length()),
        grid_spec=grid_spec,
        out_shape=jax.ShapeDtypeStruct((B, rows, width), f32),
        compiler_params=pltpu.CompilerParams(
            dimension_semantics=("parallel", "arbitrary"), vmem_limit_bytes=VMEM_LIMIT),
        name="dsa_sample",
    )(page_table, us, *([idx_pool] * SMP_NPG), *([kv_pool] * SMP_NPG))


def _nsa_sample_kernel(pt_ref, us_ref, cw_ref, win_ref, *refs, n_steps, past, t_new):
    npg = SMP_NPG
    pages, o_ref = refs[:npg], refs[npg]
    kc_scr, vc_scr, ks_scr, vs_scr, bias_scr, m_scr, l_scr, acc_scr = refs[npg + 1:]
    rows = SMP_ROWS
    hpg = NSA_HPG
    G = NSA_KV_HEADS
    cw = npg * PAGE_SIZE
    bpp = PAGE_SIZE // NSA_BLOCK
    bpc = cw // NSA_BLOCK
    s = pl.program_id(1)
    w = cw_ref[...]

    @pl.when(s == 0)
    def _():
        kc_scr[...] = jnp.zeros_like(kc_scr)
        vc_scr[...] = jnp.zeros_like(vc_scr)

    def page_rows(j, r, g):
        return pages[j][pl.ds(r * G + g, PAGE_SIZE, stride=4 * G), :]

    for g in range(G):
        kcs, vcs = [], []
        for j in range(npg):
            r0 = pl.multiple_of((s * npg + j) * PAGE_SIZE, PAGE_SIZE)
            kcs.append(jnp.sum(page_rows(j, 0, g).reshape(bpp, NSA_BLOCK, HEAD_DIM) * w[0][None], axis=1))
            vcs.append(jnp.sum(page_rows(j, 1, g).reshape(bpp, NSA_BLOCK, HEAD_DIM) * w[1][None], axis=1))
            ks_scr[g, pl.ds(r0, PAGE_SIZE), :] = page_rows(j, 2, g).astype(bf16)
            vs_scr[g, pl.ds(r0, PAGE_SIZE), :] = page_rows(j, 3, g).astype(bf16)
        b0 = pl.multiple_of(s * bpc, bpc)
        kc_scr[g, pl.ds(b0, bpc), :] = jnp.concatenate(kcs, axis=0)
        vc_scr[g, pl.ds(b0, bpc), :] = jnp.concatenate(vcs, axis=0)

    @pl.when(s == n_steps - 1)
    def _():
        us = us_ref[...]
        win = win_ref[...]
        wk = win.shape[0]
        span = _round_up(wk + rows, 128)
        valid = lax.broadcasted_iota(jnp.int32, (rows, HEAD_DIM), 0) < t_new
        gates = jax.nn.sigmoid(us[:, COL_SM + SM_GA:COL_SM + SM_GA + 3 * NSA_HEADS])
        for g in range(G):
            def new(r):
                c0 = COL_KVA + (r * G + g) * HEAD_DIM
                return us[:, c0:c0 + HEAD_DIM]
            kc_new = jnp.sum(jnp.where(valid, new(0) * w[0][0:rows], 0.0), axis=0, keepdims=True)
            vc_new = jnp.sum(jnp.where(valid, new(1) * w[1][0:rows], 0.0), axis=0, keepdims=True)
            kc_scr[g, pl.ds(n_steps * bpc, rows), :] = _pad_rows(kc_new, rows)
            vc_scr[g, pl.ds(n_steps * bpc, rows), :] = _pad_rows(vc_new, rows)
            ks_scr[g, pl.ds(past, cw), :] = _pad_rows(new(2), cw).astype(bf16)
            vs_scr[g, pl.ds(past, cw), :] = _pad_rows(new(3), cw).astype(bf16)
            qg = _stack_heads(us[:, COL_QA:COL_QB], g * hpg, hpg, HEAD_DIM ** -0.5)
            o_cmp, sel = _nsa_compressed(qg, kc_scr[g], vc_scr[g], past, rows, past // NSA_BLOCK + 1)
            _nsa_selection_bias(sel, bias_scr, n_steps + 1, past)
            o_slc = _flash_chunks(qg, ks_scr.at[g], vs_scr.at[g], bias_scr, n_steps + 1, m_scr, l_scr, acc_scr, hpg)
            kw = _pad_rows(jnp.concatenate([win[:, g * HEAD_DIM:(g + 1) * HEAD_DIM], new(4)], axis=0), span)
            vw = _pad_rows(jnp.concatenate([win[:, (G + g) * HEAD_DIM:(G + g + 1) * HEAD_DIM], new(5)], axis=0), span)
            o_win = _nsa_window(qg, kw.astype(bf16), vw.astype(bf16), past, past - wk, rows)
            for h in range(hpg):
                c = (g * hpg + h) * 3
                rws = slice(h * rows, (h + 1) * rows)
                o_ref[:, (g * hpg + h) * HEAD_DIM:(g * hpg + h + 1) * HEAD_DIM] = (
                    gates[:, c:c + 1] * o_cmp[rws] + gates[:, c + 1:c + 2] * o_slc[rws] + gates[:, c + 2:c + 3] * o_win[rws])


def _nsa_sample(us, cmp_w, win_buf, nsa_pool, page_table, layer, t_new):
    B = us.shape[0]
    rows = SMP_ROWS
    n_pages = page_table.shape[1]
    past = n_pages * PAGE_SIZE
    n_steps = n_pages // SMP_NPG
    cw = SMP_NPG * PAGE_SIZE
    nbp = _round_up((n_steps + 1) * (cw // NSA_BLOCK), 128)
    wk = win_buf.shape[2]
    width = NSA_HEADS * HEAD_DIM
    grid_spec = pltpu.PrefetchScalarGridSpec(
        num_scalar_prefetch=1,
        grid=(B, n_steps),
        in_specs=([pl.BlockSpec((None, rows, us.shape[-1]), lambda b, s, pt: (b, 0, 0)),
                   pl.BlockSpec((2, NSA_BLOCK, HEAD_DIM), lambda b, s, pt: (0, 0, 0)),
                   pl.BlockSpec((None, None, wk, win_buf.shape[-1]), lambda b, s, pt: (layer, b, 0, 0))]
                  + _page_specs(PAGE_SIZE * 4 * NSA_KV_HEADS, HEAD_DIM, layer)),
        out_specs=pl.BlockSpec((None, rows, width), lambda b, s, pt: (b, 0, 0)),
        scratch_shapes=[
            pltpu.VMEM((NSA_KV_HEADS, nbp, HEAD_DIM), f32), pltpu.VMEM((NSA_KV_HEADS, nbp, HEAD_DIM), f32),
            pltpu.VMEM((NSA_KV_HEADS, past + cw, HEAD_DIM), bf16), pltpu.VMEM((NSA_KV_HEADS, past + cw, HEAD_DIM), bf16),
            pltpu.VMEM((n_steps + 1, rows, cw), f32),
            pltpu.VMEM((NSA_HPG * rows, 1), f32), pltpu.VMEM((NSA_HPG * rows, 1), f32),
            pltpu.VMEM((NSA_HPG * rows, HEAD_DIM), f32),
        ])
    return pl.pallas_call(
        functools.partial(_nsa_sample_kernel, n_steps=n_steps, past=past, t_new=t_new),
        grid_spec=grid_spec,
        out_shape=jax.ShapeDtypeStruct((B, rows, width), f32),
        compiler_params=pltpu.CompilerParams(
            dimension_semantics=("parallel", "arbitrary"), vmem_limit_bytes=VMEM_LIMIT),
        name="nsa_sample",
    )(page_table, us, cmp_w, win_buf, *([nsa_pool] * SMP_NPG))


REC_CHUNK = 64
REC_TC = 512
CONV_PAD = 8


def _dot_b(a, b):
    return jnp.dot(a.astype(bf16), b.astype(bf16), preferred_element_type=f32)


def _dot_nt_b(a, b):
    return _dot_nt(a.astype(bf16), b.astype(bf16))


def _dot_tn_b(a, b):
    return _dot_b(a.T, b)


def _dot_3(a, b):
    ah, bh = a.astype(bf16), b.astype(bf16)
    al, bl = (a - ah.astype(f32)).astype(bf16), (b - bh.astype(f32)).astype(bf16)
    return (jnp.dot(ah, bh, preferred_element_type=f32) + jnp.dot(ah, bl, preferred_element_type=f32)
            + jnp.dot(al, bh, preferred_element_type=f32))


def _softplus(x):
    return jnp.maximum(x, 0.0) + jnp.log(1.0 + jnp.exp(-jnp.abs(x)))


def _silu(x):
    return x * jax.nn.sigmoid(x)


def _conv_block(xbuf_scr, raw, w, first, conv0):
    tc = raw.shape[0]
    nh = CONV_WIDTH - 1

    @pl.when(first)
    def _():
        xbuf_scr[CONV_PAD - nh:CONV_PAD, :] = conv0

    xbuf_scr[CONV_PAD:CONV_PAD + tc, :] = raw
    y = xbuf_scr[pl.ds(CONV_PAD - nh, tc), :] * w[0:1, :]
    for j in range(1, CONV_WIDTH):
        y = y + xbuf_scr[pl.ds(CONV_PAD - nh + j, tc), :] * w[j:j + 1, :]
    xbuf_scr[CONV_PAD - nh:CONV_PAD, :] = raw[tc - nh:tc, :]
    return y


def _lane_pick(x, idx):
    lane = lax.broadcasted_iota(jnp.int32, x.shape, 1)
    return jnp.sum(jnp.where(lane == idx, x, 0.0), axis=-1, keepdims=True)


def _chunk_cumsum(col, row):
    c = col.shape[0]
    i = lax.broadcasted_iota(jnp.int32, (c, c), 0)
    j = lax.broadcasted_iota(jnp.int32, (c, c), 1)
    ccol = jnp.sum(jnp.where(j <= i, jnp.broadcast_to(row, (c, c)), 0.0), axis=1, keepdims=True)
    crow = jnp.sum(jnp.where(i <= j, jnp.broadcast_to(col, (c, c)), 0.0), axis=0, keepdims=True)
    return ccol, crow


def _gdn_kernel(alog_ref, dtb_ref, q_ref, k_ref, v_ref, z_ref, sm_ref, smt_ref, wq_ref, wk_ref, wv_ref,
                cq_ref, ck_ref, cv_ref, s0_ref, ng_ref, o_ref, s_out_ref, xbuf_scr, s_scr, *, t_valid):
    tc = q_ref.shape[0]
    C = REC_CHUNK
    nc = tc // C
    hp = pl.program_id(1)
    blk = pl.program_id(2)
    first = blk == 0

    @pl.when(first)
    def _():
        s_scr[...] = s0_ref[...]

    raw = jnp.concatenate([q_ref[...], k_ref[...], v_ref[...]], axis=1)
    w = jnp.concatenate([wq_ref[...], wk_ref[...], wv_ref[...]], axis=1)
    conv0 = jnp.concatenate([cq_ref[...], ck_ref[...], cv_ref[...]], axis=1)
    y = _silu(_conv_block(xbuf_scr, raw, w, first, conv0))
    valid_col = (blk * tc + lax.broadcasted_iota(jnp.int32, (tc, 1), 0)) < t_valid
    valid_row = (blk * tc + lax.broadcasted_iota(jnp.int32, (1, tc), 1)) < t_valid
    sm = sm_ref[...]
    ii = lax.broadcasted_iota(jnp.int32, (C, C), 0)
    jj = lax.broadcasted_iota(jnp.int32, (C, C), 1)
    eye = jnp.where(ii == jj, 1.0, 0.0)

    pieces = []
    for e in range(2):
        h = 2 * hp + e
        qf = y[:, e * GDN_DK:(e + 1) * GDN_DK]
        kf = y[:, (2 + e) * GDN_DK:(3 + e) * GDN_DK]
        vf = y[:, (4 + e) * GDN_DK:(5 + e) * GDN_DK]
        q = jnp.where(valid_col, qf * lax.rsqrt(jnp.sum(qf * qf, axis=-1, keepdims=True) + 1e-6) * (GDN_DK ** -0.5), 0.0)
        k = jnp.where(valid_col, kf * lax.rsqrt(jnp.sum(kf * kf, axis=-1, keepdims=True) + 1e-6), 0.0)
        v = jnp.where(valid_col, vf, 0.0)
        a_scale = -jnp.exp(jnp.full((1, 1), alog_ref[h], f32))
        dtb = dtb_ref[h]
        beta = jnp.where(valid_col, jax.nn.sigmoid(_lane_pick(sm, OSM_BETA + h)), 0.0)
        g_col = jnp.where(valid_col, a_scale * _softplus(_lane_pick(sm, OSM_A + h) + dtb), 0.0)
        g_row = jnp.where(valid_row, a_scale * _softplus(smt_ref[pl.ds(OSM_A + h, 1), :] + dtb), 0.0)
        for c in range(nc):
            r = slice(c * C, (c + 1) * C)
            gc_col, gc_row = _chunk_cumsum(g_col[r], g_row[:, r])
            gam = jnp.where(jj <= ii, jnp.exp(jnp.where(jj <= ii, gc_col - gc_row, 0.0)), 0.0)
            kb = k[r] * beta[r]
            pieces.append(dict(q=q[r], k=k[r], vb=v[r] * beta[r], kb=kb, gc=gc_col, gam=gam,
                               x=jnp.where(jj < ii, -(_dot_nt_b(kb, k[r]) * gam), 0.0)))

    xs = [p['x'] for p in pieces]
    tms = [eye + x for x in xs]
    for _ in range(max(1, (C - 1).bit_length()) - 1):
        xs = [_dot_3(x, x) for x in xs]
        tms = [tm + _dot_3(tm, x) for tm, x in zip(tms, xs)]
    for p, tm in zip(pieces, tms):
        p['u0'] = _dot_b(tm, p['vb'])
        p['wd'] = _dot_b(tm, p['kb'] * jnp.exp(p['gc']))
        p['qk'] = _dot_nt_b(p['q'], p['k']) * p['gam']

    S = [s_scr[0], s_scr[1]]
    outs = [[], []]
    for c in range(nc):
        for e in range(2):
            p = pieces[e * nc + c]
            g_last = p['gc'][C - 1:C, :]
            u = p['u0'] - _dot_b(p['wd'], S[e])
            outs[e].append(_dot_b(p['q'] * jnp.exp(p['gc']), S[e]) + _dot_b(p['qk'], u))
            S[e] = S[e] * jnp.exp(g_last) + _dot_tn_b(p['k'] * jnp.exp(g_last - p['gc']), u)
    z = z_ref[...]
    halves = []
    for e in range(2):
        s_scr[e] = S[e]
        o = jnp.concatenate(outs[e], axis=0) if nc > 1 else outs[e][0]
        o = o * lax.rsqrt(jnp.mean(o * o, axis=-1, keepdims=True) + NORM_EPS) * ng_ref[...]
        halves.append(o * _silu(z[:, e * GDN_DV:(e + 1) * GDN_DV]))
    o_ref[...] = jnp.concatenate(halves, axis=1)

    @pl.when(blk == pl.num_programs(2) - 1)
    def _():
        s_out_ref[...] = s_scr[...]


def _smem_spec():
    return pl.BlockSpec(memory_space=pltpu.SMEM)


def _gdn(u, smt, conv_w, conv0, s0, a_log, dt_bias, norm_g, layer, t_valid):
    B, Tp, _ = u.shape
    tc = min(REC_TC, Tp)
    H = GDN_HEADS
    HP = H // 2
    pw = 2 * GDN_DK
    qkv = lambda part: pl.BlockSpec((None, tc, pw), lambda b, hp, i: (b, i, OC_QKV // pw + part * HP + hp))
    cw = lambda part: pl.BlockSpec((None, CONV_WIDTH, pw), lambda b, hp, i: (layer, 0, part * HP + hp))
    c0 = lambda part: pl.BlockSpec((None, CONV_WIDTH - 1, pw), lambda b, hp, i: (b, 0, part * HP + hp))
    return pl.pallas_call(
        functools.partial(_gdn_kernel, t_valid=t_valid),
        grid=(B, HP, Tp // tc),
        in_specs=[
            _smem_spec(), _smem_spec(),
            qkv(0), qkv(1), qkv(2),
            pl.BlockSpec((None, tc, pw), lambda b, hp, i: (b, i, OC_ZC // pw + hp)),
            pl.BlockSpec((None, tc, 128), lambda b, hp, i: (b, i, OC_SM // 128)),
            pl.BlockSpec((None, 128, tc), lambda b, hp, i: (b, 0, i)),
            cw(0), cw(1), cw(2), c0(0), c0(1), c0(2),
            pl.BlockSpec((None, 2, GDN_DK, GDN_DV), lambda b, hp, i: (b, hp, 0, 0)),
            pl.BlockSpec((1, GDN_DV), lambda b, hp, i: (0, 0)),
        ],
        out_specs=[
            pl.BlockSpec((None, tc, pw), lambda b, hp, i: (b, i, hp)),
            pl.BlockSpec((None, 2, GDN_DK, GDN_DV), lambda b, hp, i: (b, hp, 0, 0)),
        ],
        out_shape=[jax.ShapeDtypeStruct((B, Tp, H * GDN_DV), f32),
                   jax.ShapeDtypeStruct((B, H, GDN_DK, GDN_DV), f32)],
        scratch_shapes=[pltpu.VMEM((CONV_PAD + tc, 3 * pw), f32), pltpu.VMEM((2, GDN_DK, GDN_DV), f32)],
        compiler_params=pltpu.CompilerParams(
            dimension_semantics=("parallel", "parallel", "arbitrary"), vmem_limit_bytes=VMEM_LIMIT),
        name="gdn_scan",
    )(a_log, dt_bias, u, u, u, u, u, smt, conv_w, conv_w, conv_w, conv0, conv0, conv0, s0, norm_g.reshape(1, -1))


def _ssd_kernel(alog_ref, dtb_ref, d_ref, x_ref, b_ref, c_ref, z_ref, sm_ref, smt_ref, wx_ref, wb_ref, wc_ref,
                bx_ref, bb_ref, bc_ref, cx_ref, cb_ref, cc_ref, h0_ref, o_ref, h_out_ref, xbuf_scr, h_scr,
                *, t_valid):
    tc = x_ref.shape[0]
    C = REC_CHUNK
    P = SSD_HEAD_DIM
    hp = pl.program_id(1)
    blk = pl.program_id(2)
    first = blk == 0

    @pl.when(first)
    def _():
        h_scr[...] = h0_ref[...]

    raw = jnp.concatenate([x_ref[...], b_ref[...], c_ref[...]], axis=1)
    w = jnp.concatenate([wx_ref[...], wb_ref[...], wc_ref[...]], axis=1)
    bias = jnp.concatenate([bx_ref[...], bb_ref[...], bc_ref[...]], axis=1)
    conv0 = jnp.concatenate([cx_ref[...], cb_ref[...], cc_ref[...]], axis=1)
    y = _silu(_conv_block(xbuf_scr, raw, w, first, conv0) + bias)
    valid_col = (blk * tc + lax.broadcasted_iota(jnp.int32, (tc, 1), 0)) < t_valid
    valid_row = (blk * tc + lax.broadcasted_iota(jnp.int32, (1, tc), 1)) < t_valid
    xs = jnp.where(valid_col, y[:, 0:128], 0.0)
    bm = jnp.where(valid_col, y[:, 128:256], 0.0)
    cm = jnp.where(valid_col, y[:, 256:384], 0.0)
    sm = sm_ref[...]
    z = z_ref[...]
    ii = lax.broadcasted_iota(jnp.int32, (C, C), 0)
    jj = lax.broadcasted_iota(jnp.int32, (C, C), 1)
    cbs = [_dot_nt_b(cm[c * C:(c + 1) * C], bm[c * C:(c + 1) * C]) for c in range(tc // C)]
    halves = []
    for e in range(2):
        hh = 2 * hp + e
        a_neg = -jnp.exp(jnp.full((1, 1), alog_ref[hh], f32))
        dtb = dtb_ref[hh]
        dt_col = jnp.where(valid_col, _softplus(_lane_pick(sm, OSM_DT + hh) + dtb), 0.0)
        dt_row = jnp.where(valid_row, _softplus(smt_ref[pl.ds(OSM_DT + hh, 1), :] + dtb), 0.0)
        xh = xs[:, e * P:(e + 1) * P]
        xdt = xh * dt_col
        hst = h_scr[e]
        outs = []
        for c in range(tc // C):
            r = slice(c * C, (c + 1) * C)
            acs_col, acs_row = _chunk_cumsum(dt_col[r] * a_neg, dt_row[:, r] * a_neg)
            lm = jnp.where(jj <= ii, jnp.exp(jnp.where(jj <= ii, acs_col - acs_row, 0.0)), 0.0)
            a_last = acs_col[C - 1:C, :]
            y_diag = _dot_b(cbs[c] * lm, xdt[r])
            y_off = _dot_nt_b(cm[r] * jnp.exp(acs_col), hst)
            outs.append(y_diag + y_off)
            hst = hst * jnp.exp(a_last) + _dot_tn_b(xdt[r] * jnp.exp(a_last - acs_col), bm[r])
        h_scr[e] = hst
        yh = jnp.concatenate(outs, axis=0) if len(outs) > 1 else outs[0]
        halves.append((yh + d_ref[hh] * xh) * _silu(z[:, e * P:(e + 1) * P]))
    o_ref[...] = jnp.concatenate(halves, axis=1)

    @pl.when(blk == pl.num_programs(2) - 1)
    def _():
        h_out_ref[...] = h_scr[...]


def _ssd(u, smt, conv_w, conv_b, conv0, h0, a_log, dt_bias, d_skip, layer, t_valid):
    B, Tp, _ = u.shape
    tc = min(REC_TC, Tp)
    HP = SSD_HEADS // 2
    hpg = SSD_HEADS // SSD_GROUPS // 2
    xcol = lambda hp: hp
    bcol = lambda hp: SSD_D_INNER // 128 + hp // hpg
    ccol = lambda hp: (SSD_D_INNER + SSD_GROUPS * SSD_STATE) // 128 + hp // hpg
    def tri(fn_col):
        return (pl.BlockSpec((None, tc, 128), lambda b, hp, i: (b, i, OC_XBC // 128 + fn_col(hp))),
                pl.BlockSpec((None, CONV_WIDTH, 128), lambda b, hp, i: (layer, 0, fn_col(hp))),
                pl.BlockSpec((None, 1, 128), lambda b, hp, i: (layer, 0, fn_col(hp))),
                pl.BlockSpec((None, CONV_WIDTH - 1, 128), lambda b, hp, i: (b, 0, fn_col(hp))))
    (xs, wx, bx, cx), (bs, wb, bb, cb), (cs, wc, bc, cc) = tri(xcol), tri(bcol), tri(ccol)
    return pl.pallas_call(
        functools.partial(_ssd_kernel, t_valid=t_valid),
        grid=(B, HP, Tp // tc),
        in_specs=[
            _smem_spec(), _smem_spec(), _smem_spec(),
            xs, bs, cs,
            pl.BlockSpec((None, tc, 128), lambda b, hp, i: (b, i, OC_ZD // 128 + hp)),
            pl.BlockSpec((None, tc, 128), lambda b, hp, i: (b, i, OC_SM // 128)),
            pl.BlockSpec((None, 128, tc), lambda b, hp, i: (b, 0, i)),
            wx, wb, wc, bx, bb, bc, cx, cb, cc,
            pl.BlockSpec((None, 2, SSD_HEAD_DIM, SSD_STATE), lambda b, hp, i: (b, hp, 0, 0)),
        ],
        out_specs=[
            pl.BlockSpec((None, tc, 128), lambda b, hp, i: (b, i, hp)),
            pl.BlockSpec((None, 2, SSD_HEAD_DIM, SSD_STATE), lambda b, hp, i: (b, hp, 0, 0)),
        ],
        out_shape=[jax.ShapeDtypeStruct((B, Tp, SSD_D_INNER), f32),
                   jax.ShapeDtypeStruct((B, SSD_HEADS, SSD_HEAD_DIM, SSD_STATE), f32)],
        scratch_shapes=[pltpu.VMEM((CONV_PAD + tc, 3 * 128), f32), pltpu.VMEM((2, SSD_HEAD_DIM, SSD_STATE), f32)],
        compiler_params=pltpu.CompilerParams(
            dimension_semantics=("parallel", "parallel", "arbitrary"), vmem_limit_bytes=VMEM_LIMIT),
        name="ssd_scan",
    )(a_log, dt_bias, d_skip, u, u, u, u, u, smt, conv_w, conv_w, conv_w, conv_b, conv_b, conv_b,
      conv0, conv0, conv0, h0)


def _split(u, sizes):
    return jnp.split(u, np.cumsum(sizes)[:-1].tolist(), axis=-1)


def _even_mixer(u, cmp_w, cache, i):
    B, T, _ = u.shape
    kva = u[..., COL_KVA:COL_KVB].reshape(B, T, 6, NSA_KV_HEADS, HEAD_DIM)
    kvb = u[..., COL_KVB:COL_QI].reshape(B, T, 2, DSA_KV_HEADS, HEAD_DIM)
    ki = u[..., COL_SM + SM_KI:COL_SM + SM_GA]
    nsa_rows, win_rows = kva[:, :, :4], kva[:, :, 4:]
    if cache is None:
        o_a = _nsa_prompt(u, cmp_w)
        o_b = _dsa_prompt(u, min(DSA_TOPK, T // 4))
        return (o_a, o_b), (nsa_rows, kvb, ki, win_rows[:, T - min(NSA_WINDOW, T):])
    flat = lambda a: a.reshape(a.shape[:3] + (-1,))
    us = jnp.pad(u, ((0, 0), (0, SMP_ROWS - T), (0, 0)))
    pt = cache['page_table']
    o_a = _nsa_sample(us, cmp_w, flat(cache['nsa_win']), _page_rows_view(cache['nsa_kv']), pt, i, T)[:, :T]
    idx_t = jnp.swapaxes(cache['dsa_idx_k'], 2, 3)
    o_b = _dsa_sample(us, idx_t, _page_rows_view(cache['dsa_kv']), pt, i, T)[:, :T]
    new_win = jnp.concatenate([cache['nsa_win'][i], win_rows], axis=1)[:, T:]
    return (o_a, o_b), (nsa_rows, kvb, ki, new_win)


def _odd_mixer(u, W, i, init, t_valid):
    S0, conv_c0, h0, conv_d0 = init
    nr = W['ssd_conv_b'].shape[0]
    smt = jnp.swapaxes(u[..., OC_SM:OC_SM + 128], 1, 2)
    o_c, S = _gdn(u, smt, W['gdn_conv_w'], conv_c0, S0, W['gdn_a_log'][i], W['gdn_dt_bias'][i],
                  W['gdn_norm_g'][i], i, t_valid)
    y, hN = _ssd(u, smt, W['ssd_conv_w'], W['ssd_conv_b'].reshape(nr, 1, -1), conv_d0, h0, W['ssd_a_log'][i],
                 W['ssd_dt_bias'][i], W['ssd_d'][i], i, t_valid)

    def conv_state(buf, lo, width):
        rows = u[:, :t_valid, lo:lo + width]
        nh = CONV_WIDTH - 1
        return rows[:, t_valid - nh:] if t_valid >= nh else jnp.concatenate([buf, rows], axis=1)[:, -nh:]

    return (o_c[:, :t_valid], y[:, :t_valid]), (S, conv_state(conv_c0, OC_QKV, GDN_CONV_DIM), hN,
                                               conv_state(conv_d0, OC_XBC, SSD_CONV_DIM))


def _trunk(x, W, cache):
    B, T, D = x.shape
    x = x.reshape(B * T, D)
    attn_new, rec_new = [], []
    for l in range(DEPTH):
        i = l // 2
        x = _ffn(x, W['norm_g'][l, 0], W['ffn_wg'], W['ffn_wu'], W['ffn_wd'], l, 0)
        if l % 2 == 0:
            u = _inproj(x, W['norm_g'][l, 1], W['attn_w_in'], i).reshape(B, T, -1)
            (o0, o1), st = _even_mixer(u, W['nsa_cmp_w'][i], cache, i)
            attn_new.append(st)
            x = _outproj(x, o0.reshape(B * T, -1), o1.reshape(B * T, -1), W['attn_w_out'], i)
        else:
            u = _inproj(x, W['norm_g'][l, 1], W['rec_w_in'], i).reshape(B, T, -1)
            u = jnp.pad(u, ((0, 0), (0, _round_up(T, REC_CHUNK) - T), (0, 0)))
            if cache is None:
                init = (jnp.zeros((B, GDN_HEADS, GDN_DK, GDN_DV), f32),
                        jnp.zeros((B, CONV_WIDTH - 1, GDN_CONV_DIM), f32),
                        jnp.zeros((B, SSD_HEADS, SSD_HEAD_DIM, SSD_STATE), f32),
                        jnp.zeros((B, CONV_WIDTH - 1, SSD_CONV_DIM), f32))
            else:
                init = (cache['gdn'][i], cache['gdn_conv'][i], cache['ssd'][i], cache['ssd_conv'][i])
            (o0, o1), st = _odd_mixer(u, W, i, init, T)
            rec_new.append(st)
            x = _outproj(x, o0.reshape(B * T, -1), o1.reshape(B * T, -1), W['rec_w_out'], i,
                         g1=W['ssd_norm_g'][i], norm_groups=SSD_GROUPS)
        x = _ffn(x, W['norm_g'][l, 2], W['ffn_wg'], W['ffn_wu'], W['ffn_wd'], l, 1,
                 final_g=W['final_norm_g'] if l == DEPTH - 1 else None)
    y = x.reshape(B, T, D)

    def stack(lst, j):
        return jnp.stack([s[j] for s in lst])

    return (y, stack(attn_new, 0), stack(attn_new, 1), stack(attn_new, 2), stack(attn_new, 3),
            stack(rec_new, 0), stack(rec_new, 1), stack(rec_new, 2), stack(rec_new, 3))


def _pad_cols(w, n):
    return jnp.pad(w, ((0, 0), (0, 0), (0, n - w.shape[-1])))


def _pack_even_w(w):
    qa, kva, ga, qb, kvb, qi, ki, wi = _split(w, EVEN_SIZES)
    return _pad_cols(jnp.concatenate([qa, qb, kva, kvb, qi, ki, ga, wi], axis=-1), _round_up(EVEN_IN, PROJ_TN))


def _pack_odd_w(w):
    qkv, beta, a, zc, zd, xbc, dt = _split(w, ODD_SIZES)
    return _pad_cols(jnp.concatenate([qkv, zc, zd, xbc, beta, a, dt], axis=-1), _round_up(OC_SM + 128, PROJ_TN))


def kernel(x_prompt, x_sample, cache_nsa_kv, cache_dsa_kv, cache_dsa_idx_k, page_table, state_nsa_win,
           state_gdn, state_gdn_conv, state_ssd, state_ssd_conv, norm_g, final_norm_g, ffn_w_gate, ffn_w_up,
           ffn_w_down, attn_w_in, attn_w_out, nsa_cmp_w, rec_w_in, rec_w_out, gdn_conv_w, gdn_a_log,
           gdn_dt_bias, gdn_norm_g, ssd_conv_w, ssd_conv_b, ssd_dt_bias, ssd_a_log, ssd_d, ssd_norm_g):
    W = {'norm_g': norm_g, 'final_norm_g': final_norm_g,
         'ffn_wg': ffn_w_gate.astype(bf16), 'ffn_wu': ffn_w_up.astype(bf16), 'ffn_wd': ffn_w_down.astype(bf16),
         'attn_w_in': _pack_even_w(attn_w_in).astype(bf16),
         'attn_w_out': attn_w_out.astype(bf16), 'nsa_cmp_w': nsa_cmp_w,
         'rec_w_in': _pack_odd_w(rec_w_in).astype(bf16),
         'rec_w_out': rec_w_out.astype(bf16), 'gdn_conv_w': gdn_conv_w, 'gdn_a_log': gdn_a_log,
         'gdn_dt_bias': gdn_dt_bias, 'gdn_norm_g': gdn_norm_g, 'ssd_conv_w': ssd_conv_w, 'ssd_conv_b': ssd_conv_b,
         'ssd_dt_bias': ssd_dt_bias, 'ssd_a_log': ssd_a_log, 'ssd_d': ssd_d, 'ssd_norm_g': ssd_norm_g}
    cache = {'nsa_kv': cache_nsa_kv, 'dsa_kv': cache_dsa_kv, 'dsa_idx_k': cache_dsa_idx_k,
             'page_table': page_table, 'nsa_win': state_nsa_win, 'gdn': state_gdn, 'gdn_conv': state_gdn_conv,
             'ssd': state_ssd, 'ssd_conv': state_ssd_conv}
    (y_prompt, p_nsa_kv, p_dsa_kv, p_dsa_idx_k, p_nsa_win,
     p_gdn, p_gdn_conv, p_ssd, p_ssd_conv) = _trunk(x_prompt, W, None)
    (y_sample, s_nsa_kv, s_dsa_kv, s_dsa_idx_k, s_nsa_win,
     s_gdn, s_gdn_conv, s_ssd, s_ssd_conv) = _trunk(x_sample, W, cache)
    return (y_prompt, y_sample, p_nsa_kv, p_dsa_kv, p_dsa_idx_k, p_nsa_win, p_gdn, p_gdn_conv, p_ssd, p_ssd_conv,
            s_nsa_kv, s_dsa_kv, s_dsa_idx_k, s_nsa_win, s_gdn, s_gdn_conv, s_ssd, s_ssd_conv)
```

```python
import functools
import math

import jax
import jax.numpy as jnp
import numpy as np
from jax import lax
from jax.experimental import pallas as pl
from jax.experimental.pallas import tpu as pltpu

D_MODEL = 2048
DEPTH = 4
PAGE_SIZE = 128
HEAD_DIM = 128
NSA_HEADS = 8
NSA_KV_HEADS = 2
NSA_HPG = NSA_HEADS // NSA_KV_HEADS
NSA_BLOCK = 64
NSA_N_SEL = 16
NSA_WINDOW = 512
NSA_FORCE = 1.0e4
SLC_Q_BLOCK = 64
DSA_HEADS = 8
DSA_KV_HEADS = 2
DSA_HPG = DSA_HEADS // DSA_KV_HEADS
IDX_HEADS = 8
IDX_DIM = 64
DSA_TOPK = 256
Q_BLOCK = 128
GDN_HEADS = 8
GDN_DK = 128
GDN_DV = 128
GDN_CHUNK = 64
CONV_WIDTH = 4
SSD_D_INNER = D_MODEL // 2
SSD_HEAD_DIM = 64
SSD_HEADS = SSD_D_INNER // SSD_HEAD_DIM
SSD_GROUPS = 2
SSD_STATE = 128
SSD_CHUNK = 64
D_FF = 5632
NORM_EPS = 1e-6

EVEN_SIZES = (NSA_HEADS * HEAD_DIM, 6 * NSA_KV_HEADS * HEAD_DIM, 3 * NSA_HEADS,
              DSA_HEADS * HEAD_DIM, 2 * DSA_KV_HEADS * HEAD_DIM,
              IDX_HEADS * IDX_DIM, IDX_DIM, IDX_HEADS)
EVEN_IN = sum(EVEN_SIZES)
GDN_CONV_DIM = GDN_HEADS * (2 * GDN_DK + GDN_DV)
SSD_CONV_DIM = SSD_D_INNER + 2 * SSD_GROUPS * SSD_STATE
ODD_SIZES = (GDN_CONV_DIM, GDN_HEADS, GDN_HEADS, GDN_HEADS * GDN_DV,
             SSD_D_INNER, SSD_CONV_DIM, SSD_HEADS)
ODD_IN = sum(ODD_SIZES)

COL_QA = 0
COL_QB = COL_QA + NSA_HEADS * HEAD_DIM
COL_KVA = COL_QB + DSA_HEADS * HEAD_DIM
COL_KVB = COL_KVA + 6 * NSA_KV_HEADS * HEAD_DIM
COL_QI = COL_KVB + 2 * DSA_KV_HEADS * HEAD_DIM
COL_SM = COL_QI + IDX_HEADS * IDX_DIM
SM_KI = 0
SM_GA = SM_KI + IDX_DIM
SM_WI = SM_GA + 3 * NSA_HEADS
OC_QKV = 0
OC_ZC = OC_QKV + GDN_CONV_DIM
OC_ZD = OC_ZC + GDN_HEADS * GDN_DV
OC_XBC = OC_ZD + SSD_D_INNER
OC_SM = OC_XBC + SSD_CONV_DIM
OSM_BETA = 0
OSM_A = OSM_BETA + GDN_HEADS
OSM_DT = OSM_A + GDN_HEADS

V7X_VMEM_BYTES = 64 * 1024 * 1024
VMEM_LIMIT = V7X_VMEM_BYTES - 12 * 1024 * 1024
PROJ_TN = 1024
FFN_TF = 512

bf16 = jnp.bfloat16
f32 = jnp.float32


def _round_up(n, m):
    return -(-n // m) * m


def _row_tile(m):
    return 512 if m % 512 == 0 else m


def _rms(x, g):
    return x * lax.rsqrt(jnp.mean(x * x, axis=-1, keepdims=True) + NORM_EPS) * g


def _ffn_kernel(x_ref, g_ref, wg_ref, wu_ref, wd_ref, fg_ref, o_ref, h_scr, acc_scr, *, final):
    f = pl.program_id(1)

    @pl.when(f == 0)
    def _():
        h_scr[...] = _rms(x_ref[...], g_ref[...]).astype(bf16)
        acc_scr[...] = jnp.zeros_like(acc_scr)

    h = h_scr[...]
    gate = jnp.dot(h, wg_ref[...], preferred_element_type=f32)
    up = jnp.dot(h, wu_ref[...], preferred_element_type=f32)
    act = (gate * jax.nn.sigmoid(gate) * up).astype(bf16)
    acc_scr[...] += jnp.dot(act, wd_ref[...], preferred_element_type=f32)

    @pl.when(f == pl.num_programs(1) - 1)
    def _():
        y = x_ref[...] + 0.5 * acc_scr[...]
        if final:
            y = _rms(y, fg_ref[...])
        o_ref[...] = y


def _ffn(x, g, wg, wu, wd, l, j, final_g=None):
    m, d = x.shape
    tm = _row_tile(m)
    final = final_g is not None
    fg = final_g if final else g
    return pl.pallas_call(
        functools.partial(_ffn_kernel, final=final),
        grid=(m // tm, D_FF // FFN_TF),
        in_specs=[
            pl.BlockSpec((tm, d), lambda i, f: (i, 0)),
            pl.BlockSpec((1, d), lambda i, f: (0, 0)),
            pl.BlockSpec((None, None, d, FFN_TF), lambda i, f: (l, j, 0, f)),
            pl.BlockSpec((None, None, d, FFN_TF), lambda i, f: (l, j, 0, f)),
            pl.BlockSpec((None, None, FFN_TF, d), lambda i, f: (l, j, f, 0)),
            pl.BlockSpec((1, d), lambda i, f: (0, 0)),
        ],
        out_specs=pl.BlockSpec((tm, d), lambda i, f: (i, 0)),
        out_shape=jax.ShapeDtypeStruct((m, d), f32),
        scratch_shapes=[pltpu.VMEM((tm, d), bf16), pltpu.VMEM((tm, d), f32)],
        compiler_params=pltpu.CompilerParams(
            dimension_semantics=("parallel", "arbitrary"), vmem_limit_bytes=VMEM_LIMIT),
        name="ffn_half",
    )(x, g.reshape(1, d), wg, wu, wd, fg.reshape(1, d))


def _inproj_kernel(x_ref, g_ref, w_ref, o_ref, h_scr):
    @pl.when(pl.program_id(1) == 0)
    def _():
        h_scr[...] = _rms(x_ref[...], g_ref[...]).astype(bf16)

    o_ref[...] = jnp.dot(h_scr[...], w_ref[...], preferred_element_type=f32)


def _inproj(x, g, w, i):
    m, d = x.shape
    n = w.shape[-1]
    tm = _row_tile(m)
    return pl.pallas_call(
        _inproj_kernel,
        grid=(m // tm, n // PROJ_TN),
        in_specs=[
            pl.BlockSpec((tm, d), lambda r, c: (r, 0)),
            pl.BlockSpec((1, d), lambda r, c: (0, 0)),
            pl.BlockSpec((None, d, PROJ_TN), lambda r, c: (i, 0, c)),
        ],
        out_specs=pl.BlockSpec((tm, PROJ_TN), lambda r, c: (r, c)),
        out_shape=jax.ShapeDtypeStruct((m, n), f32),
        scratch_shapes=[pltpu.VMEM((tm, d), bf16)],
        compiler_params=pltpu.CompilerParams(
            dimension_semantics=("parallel", "arbitrary"), vmem_limit_bytes=VMEM_LIMIT),
        name="mixer_in_proj",
    )(x, g.reshape(1, d), w)


def _outproj_kernel(x_ref, a0_ref, a1_ref, w0_ref, w1_ref, g1_ref, o_ref, *, norm_groups):
    a1 = a1_ref[...]
    if norm_groups:
        gw = a1.shape[-1] // norm_groups
        a1 = jnp.concatenate([_rms(a1[:, j * gw:(j + 1) * gw], g1_ref[:, j * gw:(j + 1) * gw])
                              for j in range(norm_groups)], axis=1)
    o_ref[...] = (x_ref[...]
                  + jnp.dot(a0_ref[...].astype(bf16), w0_ref[...], preferred_element_type=f32)
                  + jnp.dot(a1.astype(bf16), w1_ref[...], preferred_element_type=f32))


def _outproj(x, a0, a1, w, i, g1=None, norm_groups=0):
    m, d = x.shape
    k = a0.shape[-1]
    tm = _row_tile(m)
    g1 = jnp.ones((k,), f32) if g1 is None else g1
    return pl.pallas_call(
        functools.partial(_outproj_kernel, norm_groups=norm_groups),
        grid=(m // tm, d // PROJ_TN),
        in_specs=[
            pl.BlockSpec((tm, PROJ_TN), lambda r, c: (r, c)),
            pl.BlockSpec((tm, k), lambda r, c: (r, 0)),
            pl.BlockSpec((tm, k), lambda r, c: (r, 0)),
            pl.BlockSpec((None, k, PROJ_TN), lambda r, c: (i, 0, c)),
            pl.BlockSpec((None, k, PROJ_TN), lambda r, c: (i, 1, c)),
            pl.BlockSpec((1, k), lambda r, c: (0, 0)),
        ],
        out_specs=pl.BlockSpec((tm, PROJ_TN), lambda r, c: (r, c)),
        out_shape=jax.ShapeDtypeStruct((m, d), f32),
        compiler_params=pltpu.CompilerParams(
            dimension_semantics=("parallel", "arbitrary"), vmem_limit_bytes=VMEM_LIMIT),
        name="mixer_out_proj",
    )(x, a0, a1, w, w, g1.reshape(1, k))


ATT_TQ = 256
ATT_KC = 1024
MASK_NEG = -1e30
INT_MIN = -2 ** 31
INT_MAX = 2 ** 31 - 1


def _stack_heads(q, first, n, scale):
    rows = jnp.concatenate([q[:, (first + h) * HEAD_DIM:(first + h + 1) * HEAD_DIM] for h in range(n)], axis=0)
    return (rows * scale).astype(bf16)


def _dot_nt(a, b):
    return lax.dot_general(a, b, (((1,), (1,)), ((), ())), preferred_element_type=f32)


def _flash_chunks(qg, k_ref, v_ref, bias_scr, nvis, m_scr, l_scr, acc_scr, reps):
    m_scr[...] = jnp.full_like(m_scr, MASK_NEG)
    l_scr[...] = jnp.zeros_like(l_scr)
    acc_scr[...] = jnp.zeros_like(acc_scr)
    cw = bias_scr.shape[-1]

    def scores(c):
        r0 = pl.multiple_of(c * cw, cw)
        return _dot_nt(qg, k_ref[pl.ds(r0, cw), :].astype(bf16))

    def update(c, s):
        r0 = pl.multiple_of(c * cw, cw)
        vc = v_ref[pl.ds(r0, cw), :].astype(bf16)
        s = s + jnp.concatenate([bias_scr[c]] * reps, axis=0)
        m_old = m_scr[...]
        m_new = jnp.maximum(m_old, jnp.max(s, axis=-1, keepdims=True))
        alpha = jnp.exp(m_old - m_new)
        p = jnp.exp(s - m_new)
        l_scr[...] = alpha * l_scr[...] + jnp.sum(p, axis=-1, keepdims=True)
        acc_scr[...] = alpha * acc_scr[...] + jnp.dot(p.astype(bf16), vc, preferred_element_type=f32)
        m_scr[...] = m_new

    def body(c, s):
        s_next = scores(c + 1)
        update(c, s)
        return s_next

    s_last = lax.fori_loop(0, nvis - 1, body, scores(0))
    update(nvis - 1, s_last)
    return acc_scr[...] / jnp.maximum(l_scr[...], 1e-30)


def _index_scores(qi_h, wi, ki, keys_on_lanes=False):
    n = ki.shape[1] if keys_on_lanes else ki.shape[0]
    acc = jnp.zeros((qi_h[0].shape[0], n), f32)
    for h in range(IDX_HEADS):
        qk = jnp.dot(qi_h[h], ki, preferred_element_type=f32) if keys_on_lanes else _dot_nt(qi_h[h], ki)
        rel = jnp.maximum(qk * (IDX_DIM ** -0.5), 0.0)
        acc = acc + rel * wi[:, h:h + 1]
    return acc


def _order_key(x):
    bits = lax.bitcast_convert_type(x, jnp.int32)
    return jnp.where(bits < 0, bits ^ INT_MAX, bits)


def _topk_bias(key_scr, bias_scr, cut_scr, nvis, qpos, n_keep, idx_bits):
    rows, cw = qpos.shape
    lane = lax.broadcasted_iota(jnp.int32, (rows, cw), 1)

    def count(indicator):
        def body(c, acc):
            one = indicator(key_scr[c], c)
            for j in range(cw // 128):
                acc = acc + one[:, j * 128:(j + 1) * 128]
            return acc
        acc = lax.fori_loop(0, nvis, body, jnp.zeros((rows, 128), jnp.int32))
        return jnp.sum(acc, axis=-1, keepdims=True)

    thr = jnp.where(count(lambda k, c: jnp.where(k >= 0, 1, 0)) >= n_keep, 0, INT_MIN)

    def bit_body(i, thr):
        cand = thr | lax.shift_left(jnp.int32(1), 30 - i)
        return jnp.where(count(lambda k, c: jnp.where(k >= cand, 1, 0)) >= n_keep, cand, thr)

    thr = lax.fori_loop(0, 31, bit_body, thr)
    n_gt = count(lambda k, c: jnp.where(k > thr, 1, 0))
    n_ge = count(lambda k, c: jnp.where(k >= thr, 1, 0))
    need = n_keep - n_gt
    tie_rows = jnp.where(n_ge - n_gt > need, jnp.where(thr > INT_MIN, 1, 0), 0)
    cut_scr[...] = jnp.full((rows, 1), INT_MAX, jnp.int32)

    @pl.when(jnp.max(tie_rows) > 0)
    def _():
        def idx_body(i, cut):
            cand = cut | lax.shift_left(jnp.int32(1), idx_bits - 1 - i)
            n = count(lambda k, c: jnp.where(k == thr, jnp.where(c * cw + lane < cand, 1, 0), 0))
            return jnp.where(n < need, cand, cut)
        cut_scr[...] = lax.fori_loop(0, idx_bits, idx_body, jnp.zeros((rows, 1), jnp.int32))

    cut = cut_scr[...]

    def bias_chunk(c, carry):
        k = key_scr[c]
        kpos = c * cw + lane
        tie = jnp.where(k == thr, jnp.where(kpos <= cut, 0.0, MASK_NEG), MASK_NEG)
        bias_scr[c] = jnp.where(kpos <= qpos, jnp.where(k > thr, 0.0, tie), MASK_NEG)
        return carry

    lax.fori_loop(0, nvis, bias_chunk, 0)


def _dsa_kernel(q_ref, qi_ref, sm_ref, ksm_ref, k0_ref, k1_ref, v0_ref, v1_ref, o_ref,
                key_scr, bias_scr, cut_scr, m_scr, l_scr, acc_scr, *, n_keep, idx_bits):
    tq = ATT_TQ
    t0 = pl.program_id(1) * tq
    nvis = (t0 + tq - 1) // ATT_KC + 1
    qpos = t0 + lax.broadcasted_iota(jnp.int32, (tq, ATT_KC), 0)
    lane = lax.broadcasted_iota(jnp.int32, (tq, ATT_KC), 1)
    wi = sm_ref[...][:, SM_WI:SM_WI + IDX_HEADS] * (IDX_HEADS ** -0.5)
    qi_all = qi_ref[...]
    qi_h = [qi_all[:, h * IDX_DIM:(h + 1) * IDX_DIM].astype(bf16) for h in range(IDX_HEADS)]

    def score_chunk(c, carry):
        r0 = pl.multiple_of(c * ATT_KC, ATT_KC)
        ki_c = ksm_ref[pl.ds(r0, ATT_KC), :][:, SM_KI:SM_KI + IDX_DIM].astype(bf16)
        key_scr[c] = jnp.where(r0 + lane <= qpos, _order_key(_index_scores(qi_h, wi, ki_c)), INT_MIN)
        return carry

    lax.fori_loop(0, nvis, score_chunk, 0)
    _topk_bias(key_scr, bias_scr, cut_scr, nvis, qpos, n_keep, idx_bits)

    q = q_ref[...]
    for g, (k_ref, v_ref) in enumerate(((k0_ref, v0_ref), (k1_ref, v1_ref))):
        qg = _stack_heads(q, g * DSA_HPG, DSA_HPG, HEAD_DIM ** -0.5)
        o = _flash_chunks(qg, k_ref, v_ref, bias_scr, nvis, m_scr, l_scr, acc_scr, DSA_HPG)
        for h in range(DSA_HPG):
            c0 = (g * DSA_HPG + h) * HEAD_DIM
            o_ref[:, c0:c0 + HEAD_DIM] = o[h * tq:(h + 1) * tq]


def _dsa_prompt(u, n_keep):
    B, T, _ = u.shape
    tq = ATT_TQ
    nck = T // ATT_KC
    col = lambda off, w: off // w
    kv = lambda r, g: pl.BlockSpec((None, T, HEAD_DIM), lambda b, i: (b, 0, col(COL_KVB, HEAD_DIM) + 2 * r + g))
    return pl.pallas_call(
        functools.partial(_dsa_kernel, n_keep=n_keep, idx_bits=max(1, (T - 1).bit_length())),
        grid=(B, T // tq),
        in_specs=[
            pl.BlockSpec((None, tq, DSA_HEADS * HEAD_DIM), lambda b, i: (b, i, col(COL_QB, DSA_HEADS * HEAD_DIM))),
            pl.BlockSpec((None, tq, IDX_HEADS * IDX_DIM), lambda b, i: (b, i, col(COL_QI, IDX_HEADS * IDX_DIM))),
            pl.BlockSpec((None, tq, 128), lambda b, i: (b, i, col(COL_SM, 128))),
            pl.BlockSpec((None, T, 128), lambda b, i: (b, 0, col(COL_SM, 128))),
            kv(0, 0), kv(0, 1), kv(1, 0), kv(1, 1),
        ],
        out_specs=pl.BlockSpec((None, tq, DSA_HEADS * HEAD_DIM), lambda b, i: (b, i, 0)),
        out_shape=jax.ShapeDtypeStruct((B, T, DSA_HEADS * HEAD_DIM), f32),
        scratch_shapes=[
            pltpu.VMEM((nck, tq, ATT_KC), jnp.int32), pltpu.VMEM((nck, tq, ATT_KC), f32),
            pltpu.VMEM((tq, 1), jnp.int32),
            pltpu.VMEM((DSA_HPG * tq, 1), f32), pltpu.VMEM((DSA_HPG * tq, 1), f32),
            pltpu.VMEM((DSA_HPG * tq, HEAD_DIM), f32),
        ],
        compiler_params=pltpu.CompilerParams(
            dimension_semantics=("parallel", "arbitrary"), vmem_limit_bytes=VMEM_LIMIT),
        name="dsa_prompt",
    )(u, u, u, u, u, u, u, u)


def _nsa_compress_kernel(x_ref, w_ref, o_ref, *, nb):
    x = x_ref[...].reshape(nb, NSA_BLOCK, HEAD_DIM)
    o_ref[...] = jnp.zeros_like(o_ref)
    o_ref[0:nb, :] = jnp.sum(x * w_ref[...][None], axis=1)


def _nsa_compress(u, cmp_w, nbp):
    B, T, _ = u.shape
    nb = T // NSA_BLOCK
    return pl.pallas_call(
        functools.partial(_nsa_compress_kernel, nb=nb),
        grid=(B, 2, NSA_KV_HEADS),
        in_specs=[
            pl.BlockSpec((None, T, HEAD_DIM), lambda b, r, g: (b, 0, COL_KVA // HEAD_DIM + 2 * r + g)),
            pl.BlockSpec((None, NSA_BLOCK, HEAD_DIM), lambda b, r, g: (r, 0, 0)),
        ],
        out_specs=pl.BlockSpec((None, None, None, nbp, HEAD_DIM), lambda b, r, g: (b, r, g, 0, 0)),
        out_shape=jax.ShapeDtypeStruct((B, 2, NSA_KV_HEADS, nbp, HEAD_DIM), f32),
        compiler_params=pltpu.CompilerParams(dimension_semantics=("parallel", "parallel", "parallel")),
        name="nsa_compress",
    )(u, cmp_w)


def _nsa_compressed(qg, kc, vc, t0, rows, nb):
    hpg = NSA_HPG
    nbp = kc.shape[0]
    blk4 = lax.broadcasted_iota(jnp.int32, (hpg * rows, nbp), 1)
    qpos4 = t0 + (lax.broadcasted_iota(jnp.int32, (hpg * rows, nbp), 0) & (rows - 1))
    vis4 = (blk4 + 1) * NSA_BLOCK - 1 <= qpos4
    s = jnp.where(vis4, _dot_nt(qg, kc.astype(bf16)), MASK_NEG)
    e = jnp.where(vis4, jnp.exp(s - jnp.max(s, axis=-1, keepdims=True)), 0.0)
    p = e / jnp.maximum(jnp.sum(e, axis=-1, keepdims=True), 1e-30)
    o_cmp = jnp.dot(p.astype(bf16), vc.astype(bf16), preferred_element_type=f32)
    imp = p[0:rows]
    for h in range(1, hpg):
        imp = imp + p[h * rows:(h + 1) * rows]
    blk = lax.broadcasted_iota(jnp.int32, (rows, nbp), 1)
    cur = (t0 + lax.broadcasted_iota(jnp.int32, (rows, nbp), 0)) // NSA_BLOCK
    forced = jnp.where(blk == 0, NSA_FORCE, jnp.where(blk == cur, NSA_FORCE, jnp.where(blk == cur - 1, NSA_FORCE, imp)))
    score = jnp.where(blk > cur, -jnp.inf, forced)
    if rows % 128 == 0 and nbp == 128:
        nbr = _round_up(nb, 8)
        st = score.T[0:nbr]
        blk_t = lax.broadcasted_iota(jnp.int32, (nbr, rows), 0)
        cur_t = (t0 + lax.broadcasted_iota(jnp.int32, (nbr, rows), 1)) // NSA_BLOCK
        rank_t = jnp.zeros((nbr, rows), jnp.int32)
        for b2 in range(nb):
            rowv = st[b2:b2 + 1, :]
            rank_t = rank_t + jnp.where(rowv > st, 1, jnp.where(rowv == st, jnp.where(blk_t > b2, 1, 0), 0))
        sel_t = jnp.where(blk_t <= cur_t, jnp.where(rank_t < NSA_N_SEL, 1.0, 0.0), 0.0)
        sel = jnp.concatenate([sel_t, jnp.zeros((nbp - nbr, rows), f32)], axis=0).T.astype(bf16)
        return o_cmp, sel
    rank = jnp.zeros((rows, nbp), jnp.int32)
    for b2 in range(nb):
        colv = score[:, b2:b2 + 1]
        rank = rank + jnp.where(colv > score, 1, jnp.where(colv == score, jnp.where(blk > b2, 1, 0), 0))
    sel = jnp.where(blk <= cur, jnp.where(rank < NSA_N_SEL, 1.0, 0.0), 0.0).astype(bf16)
    return o_cmp, sel


def _nsa_selection_bias(sel, bias_scr, nvis, t0):
    rows, nbp = sel.shape
    cw = bias_scr.shape[-1]
    qpos = t0 + lax.broadcasted_iota(jnp.int32, (rows, cw), 0)
    lane = lax.broadcasted_iota(jnp.int32, (rows, cw), 1)
    blk_row = lax.broadcasted_iota(jnp.int32, (nbp, cw), 0)
    key_blk = lax.broadcasted_iota(jnp.int32, (nbp, cw), 1) // NSA_BLOCK

    def bias_chunk(c, carry):
        expand = jnp.where(key_blk + c * (cw // NSA_BLOCK) == blk_row, 1.0, 0.0).astype(bf16)
        selk = jnp.dot(sel, expand, preferred_element_type=f32)
        bias_scr[c] = jnp.where(c * cw + lane <= qpos, jnp.where(selk > 0.5, 0.0, MASK_NEG), MASK_NEG)
        return carry

    lax.fori_loop(0, nvis, bias_chunk, 0)


def _nsa_window(qg, kw, vw, t0, k0, rows):
    span = kw.shape[0]
    kpos = k0 + lax.broadcasted_iota(jnp.int32, (rows, span), 1)
    qpos = t0 + lax.broadcasted_iota(jnp.int32, (rows, span), 0)
    bias = jnp.where(kpos <= qpos, jnp.where(kpos >= qpos - NSA_WINDOW, 0.0, MASK_NEG), MASK_NEG)
    sw = _dot_nt(qg, kw) + jnp.concatenate([bias] * NSA_HPG, axis=0)
    ew = jnp.exp(sw - jnp.max(sw, axis=-1, keepdims=True))
    pw = ew / jnp.sum(ew, axis=-1, keepdims=True)
    return jnp.dot(pw.astype(bf16), vw, preferred_element_type=f32)


def _nsa_kernel(q_ref, sm_ref, kc_ref, vc_ref, ks_ref, vs_ref, kw_ref, vw_ref, o_ref,
                bias_scr, m_scr, l_scr, acc_scr, *, nb, span):
    tq = ATT_TQ
    hpg = NSA_HPG
    g = pl.program_id(1)
    t0 = pl.program_id(2) * tq
    nvis = (t0 + tq - 1) // ATT_KC + 1
    qg = _stack_heads(q_ref[...], 0, hpg, HEAD_DIM ** -0.5)

    o_cmp, sel = _nsa_compressed(qg, kc_ref[...], vc_ref[...], t0, tq, nb)
    _nsa_selection_bias(sel, bias_scr, nvis, t0)
    o_slc = _flash_chunks(qg, ks_ref, vs_ref, bias_scr, nvis, m_scr, l_scr, acc_scr, hpg)

    start = pl.multiple_of(jnp.maximum(t0 + tq - span, 0), tq)
    o_win = _nsa_window(qg, kw_ref[pl.ds(start, span), :].astype(bf16), vw_ref[pl.ds(start, span), :].astype(bf16),
                        t0, start, tq)

    gates = jax.nn.sigmoid(sm_ref[...][:, SM_GA:SM_GA + 3 * NSA_HEADS])
    for h in range(hpg):
        gh = [jnp.where(g == 0, gates[:, h * 3 + j:h * 3 + j + 1],
                        gates[:, 3 * hpg + h * 3 + j:3 * hpg + h * 3 + j + 1]) for j in range(3)]
        rows = slice(h * tq, (h + 1) * tq)
        o_ref[:, h * HEAD_DIM:(h + 1) * HEAD_DIM] = gh[0] * o_cmp[rows] + gh[1] * o_slc[rows] + gh[2] * o_win[rows]


def _nsa_prompt(u, cmp_w):
    B, T, _ = u.shape
    tq = ATT_TQ
    nb = T // NSA_BLOCK
    nbp = _round_up(nb, 128)
    span = min(NSA_WINDOW + tq, T)
    kcv = _nsa_compress(u, cmp_w, nbp)
    gw = NSA_HPG * HEAD_DIM
    kva = lambda r: pl.BlockSpec((None, T, HEAD_DIM), lambda b, g, i: (b, 0, COL_KVA // HEAD_DIM + 2 * r + g))
    kc = lambda r: pl.BlockSpec((None, None, None, nbp, HEAD_DIM), lambda b, g, i: (b, r, g, 0, 0))
    nck = T // ATT_KC
    return pl.pallas_call(
        functools.partial(_nsa_kernel, nb=nb, span=span),
        grid=(B, NSA_KV_HEADS, T // tq),
        in_specs=[
            pl.BlockSpec((None, tq, gw), lambda b, g, i: (b, i, COL_QA // gw + g)),
            pl.BlockSpec((None, tq, 128), lambda b, g, i: (b, i, COL_SM // 128)),
            kc(0), kc(1), kva(2), kva(3), kva(4), kva(5),
        ],
        out_specs=pl.BlockSpec((None, tq, gw), lambda b, g, i: (b, i, g)),
        out_shape=jax.ShapeDtypeStruct((B, T, NSA_HEADS * HEAD_DIM), f32),
        scratch_shapes=[
            pltpu.VMEM((nck, tq, ATT_KC), f32),
            pltpu.VMEM((NSA_HPG * tq, 1), f32), pltpu.VMEM((NSA_HPG * tq, 1), f32),
            pltpu.VMEM((NSA_HPG * tq, HEAD_DIM), f32),
        ],
        compiler_params=pltpu.CompilerParams(
            dimension_semantics=("parallel", "parallel", "arbitrary"), vmem_limit_bytes=VMEM_LIMIT),
        name="nsa_prompt",
    )(u, u, kcv, kcv, u, u, u, u)


SMP_ROWS = 8
SMP_NPG = 8


def _pad_rows(x, n):
    return jnp.concatenate([x, jnp.zeros((n - x.shape[0], x.shape[1]), x.dtype)], axis=0)


def _dsa_sample_kernel(pt_ref, us_ref, *refs, n_steps, past, n_keep, idx_bits):
    npg = SMP_NPG
    idx_refs, kv_refs, o_ref = refs[:npg], refs[npg:2 * npg], refs[2 * npg]
    key_scr, bias_scr, cut_scr, k_scr, v_scr, m_scr, l_scr, acc_scr = refs[2 * npg + 1:]
    rows = SMP_ROWS
    cw = npg * PAGE_SIZE
    s = pl.program_id(1)
    us = us_ref[...]
    wi = us[:, COL_SM + SM_WI:COL_SM + SM_WI + IDX_HEADS] * (IDX_HEADS ** -0.5)
    qi_h = [us[:, COL_QI + h * IDX_DIM:COL_QI + (h + 1) * IDX_DIM].astype(bf16) for h in range(IDX_HEADS)]

    ki_t = jnp.concatenate([idx_refs[j][...] for j in range(npg)], axis=1).astype(bf16)
    key_scr[s] = _order_key(_index_scores(qi_h, wi, ki_t, keys_on_lanes=True))
    for j in range(npg):
        r0 = pl.multiple_of((s * npg + j) * PAGE_SIZE, PAGE_SIZE)
        for g in range(DSA_KV_HEADS):
            per_pos = 2 * DSA_KV_HEADS
            k_scr[g, pl.ds(r0, PAGE_SIZE), :] = kv_refs[j][pl.ds(g, PAGE_SIZE, stride=per_pos), :].astype(bf16)
            v_scr[g, pl.ds(r0, PAGE_SIZE), :] = kv_refs[j][pl.ds(DSA_KV_HEADS + g, PAGE_SIZE, stride=per_pos), :].astype(bf16)

    @pl.when(s == n_steps - 1)
    def _():
        row = lax.broadcasted_iota(jnp.int32, (rows, cw), 0)
        lane = lax.broadcasted_iota(jnp.int32, (rows, cw), 1)
        ki_new = _pad_rows(us[:, COL_SM + SM_KI:COL_SM + SM_KI + IDX_DIM], cw).astype(bf16)
        key_scr[n_steps] = jnp.where(lane <= row, _order_key(_index_scores(qi_h, wi, ki_new)), INT_MIN)
        for g in range(DSA_KV_HEADS):
            kcol = COL_KVB + g * HEAD_DIM
            vcol = COL_KVB + (DSA_KV_HEADS + g) * HEAD_DIM
            k_scr[g, pl.ds(past, cw), :] = _pad_rows(us[:, kcol:kcol + HEAD_DIM], cw).astype(bf16)
            v_scr[g, pl.ds(past, cw), :] = _pad_rows(us[:, vcol:vcol + HEAD_DIM], cw).astype(bf16)
        _topk_bias(key_scr, bias_scr, cut_scr, n_steps + 1, past + row, n_keep, idx_bits)
        for g in range(DSA_KV_HEADS):
            qg = _stack_heads(us[:, COL_QB:COL_KVA], g * DSA_HPG, DSA_HPG, HEAD_DIM ** -0.5)
            o = _flash_chunks(qg, k_scr.at[g], v_scr.at[g], bias_scr, n_steps + 1, m_scr, l_scr, acc_scr, DSA_HPG)
            for h in range(DSA_HPG):
                c0 = (g * DSA_HPG + h) * HEAD_DIM
                o_ref[:, c0:c0 + HEAD_DIM] = o[h * rows:(h + 1) * rows]


def _page_specs(rows, width, layer):
    return [pl.BlockSpec((None, None, rows, width),
                         lambda b, s, pt, j=j: (layer, pt[b, s * SMP_NPG + j], 0, 0)) for j in range(SMP_NPG)]


def _page_rows_view(pool):
    return pool.reshape(pool.shape[:2] + (-1, pool.shape[-1]))


def _dsa_sample(us, idx_pool, kv_pool, page_table, layer, t_new):
    B = us.shape[0]
    rows = SMP_ROWS
    n_pages = page_table.shape[1]
    past = n_pages * PAGE_SIZE
    n_steps = n_pages // SMP_NPG
    cw = SMP_NPG * PAGE_SIZE
    width = DSA_HEADS * HEAD_DIM
    grid_spec = pltpu.PrefetchScalarGridSpec(
        num_scalar_prefetch=1,
        grid=(B, n_steps),
        in_specs=([pl.BlockSpec((None, rows, us.shape[-1]), lambda b, s, pt: (b, 0, 0))]
                  + _page_specs(IDX_DIM, PAGE_SIZE, layer)
                  + _page_specs(PAGE_SIZE * 2 * DSA_KV_HEADS, HEAD_DIM, layer)),
        out_specs=pl.BlockSpec((None, rows, width), lambda b, s, pt: (b, 0, 0)),
        scratch_shapes=[
            pltpu.VMEM((n_steps + 1, rows, cw), jnp.int32), pltpu.VMEM((n_steps + 1, rows, cw), f32),
            pltpu.VMEM((rows, 1), jnp.int32),
            pltpu.VMEM((DSA_KV_HEADS, past + cw, HEAD_DIM), bf16), pltpu.VMEM((DSA_KV_HEADS, past + cw, HEAD_DIM), bf16),
            pltpu.VMEM((DSA_HPG * rows, 1), f32), pltpu.VMEM((DSA_HPG * rows, 1), f32),
            pltpu.VMEM((DSA_HPG * rows, HEAD_DIM), f32),
        ])
    return pl.pallas_call(
        functools.partial(_dsa_sample_kernel, n_steps=n_steps, past=past,
                          n_keep=min(DSA_TOPK, (past + t_new) // 4), idx_bits=(past + cw - 1).bit_length()),
        grid_spec=grid_spec,
        out_shape=jax.ShapeDtypeStruct((B, rows, width), f32),
        compiler_params=pltpu.CompilerParams(
            dimension_semantics=("parallel", "arbitrary"), vmem_limit_bytes=VMEM_LIMIT),
        name="dsa_sample",
    )(page_table, us, *([idx_pool] * SMP_NPG), *([kv_pool] * SMP_NPG))


def _nsa_sample_kernel(pt_ref, us_ref, cw_ref, win_ref, *refs, n_steps, past, t_new):
    npg = SMP_NPG
    pages, o_ref = refs[:npg], refs[npg]
    kc_scr, vc_scr, ks_scr, vs_scr, bias_scr, m_scr, l_scr, acc_scr = refs[npg + 1:]
    rows = SMP_ROWS
    hpg = NSA_HPG
    G = NSA_KV_HEADS
    cw = npg * PAGE_SIZE
    bpp = PAGE_SIZE // NSA_BLOCK
    bpc = cw // NSA_BLOCK
    s = pl.program_id(1)
    w = cw_ref[...]

    @pl.when(s == 0)
    def _():
        kc_scr[...] = jnp.zeros_like(kc_scr)
        vc_scr[...] = jnp.zeros_like(vc_scr)

    def page_rows(j, r, g):
        return pages[j][pl.ds(r * G + g, PAGE_SIZE, stride=4 * G), :]

    for g in range(G):
        kcs, vcs = [], []
        for j in range(npg):
            r0 = pl.multiple_of((s * npg + j) * PAGE_SIZE, PAGE_SIZE)
            kcs.append(jnp.sum(page_rows(j, 0, g).reshape(bpp, NSA_BLOCK, HEAD_DIM) * w[0][None], axis=1))
            vcs.append(jnp.sum(page_rows(j, 1, g).reshape(bpp, NSA_BLOCK, HEAD_DIM) * w[1][None], axis=1))
            ks_scr[g, pl.ds(r0, PAGE_SIZE), :] = page_rows(j, 2, g).astype(bf16)
            vs_scr[g, pl.ds(r0, PAGE_SIZE), :] = page_rows(j, 3, g).astype(bf16)
        b0 = pl.multiple_of(s * bpc, bpc)
        kc_scr[g, pl.ds(b0, bpc), :] = jnp.concatenate(kcs, axis=0)
        vc_scr[g, pl.ds(b0, bpc), :] = jnp.concatenate(vcs, axis=0)

    @pl.when(s == n_steps - 1)
    def _():
        us = us_ref[...]
        win = win_ref[...]
        wk = win.shape[0]
        span = _round_up(wk + rows, 128)
        valid = lax.broadcasted_iota(jnp.int32, (rows, HEAD_DIM), 0) < t_new
        gates = jax.nn.sigmoid(us[:, COL_SM + SM_GA:COL_SM + SM_GA + 3 * NSA_HEADS])
        for g in range(G):
            def new(r):
                c0 = COL_KVA + (r * G + g) * HEAD_DIM
                return us[:, c0:c0 + HEAD_DIM]
            kc_new = jnp.sum(jnp.where(valid, new(0) * w[0][0:rows], 0.0), axis=0, keepdims=True)
            vc_new = jnp.sum(jnp.where(valid, new(1) * w[1][0:rows], 0.0), axis=0, keepdims=True)
            kc_scr[g, pl.ds(n_steps * bpc, rows), :] = _pad_rows(kc_new, rows)
            vc_scr[g, pl.ds(n_steps * bpc, rows), :] = _pad_rows(vc_new, rows)
            ks_scr[g, pl.ds(past, cw), :] = _pad_rows(new(2), cw).astype(bf16)
            vs_scr[g, pl.ds(past, cw), :] = _pad_rows(new(3), cw).astype(bf16)
            qg = _stack_heads(us[:, COL_QA:COL_QB], g * hpg, hpg, HEAD_DIM ** -0.5)
            o_cmp, sel = _nsa_compressed(qg, kc_scr[g], vc_scr[g], past, rows, past // NSA_BLOCK + 1)
            _nsa_selection_bias(sel, bias_scr, n_steps + 1, past)
            o_slc = _flash_chunks(qg, ks_scr.at[g], vs_scr.at[g], bias_scr, n_steps + 1, m_scr, l_scr, acc_scr, hpg)
            kw = _pad_rows(jnp.concatenate([win[:, g * HEAD_DIM:(g + 1) * HEAD_DIM], new(4)], axis=0), span)
            vw = _pad_rows(jnp.concatenate([win[:, (G + g) * HEAD_DIM:(G + g + 1) * HEAD_DIM], new(5)], axis=0), span)
            o_win = _nsa_window(qg, kw.astype(bf16), vw.astype(bf16), past, past - wk, rows)
            for h in range(hpg):
                c = (g * hpg + h) * 3
                rws = slice(h * rows, (h + 1) * rows)
                o_ref[:, (g * hpg + h) * HEAD_DIM:(g * hpg + h + 1) * HEAD_DIM] = (
                    gates[:, c:c + 1] * o_cmp[rws] + gates[:, c + 1:c + 2] * o_slc[rws] + gates[:, c + 2:c + 3] * o_win[rws])


def _nsa_sample(us, cmp_w, win_buf, nsa_pool, page_table, layer, t_new):
    B = us.shape[0]
    rows = SMP_ROWS
    n_pages = page_table.shape[1]
    past = n_pages * PAGE_SIZE
    n_steps = n_pages // SMP_NPG
    cw = SMP_NPG * PAGE_SIZE
    nbp = _round_up((n_steps + 1) * (cw // NSA_BLOCK), 128)
    wk = win_buf.shape[2]
    width = NSA_HEADS * HEAD_DIM
    grid_spec = pltpu.PrefetchScalarGridSpec(
        num_scalar_prefetch=1,
        grid=(B, n_steps),
        in_specs=([pl.BlockSpec((None, rows, us.shape[-1]), lambda b, s, pt: (b, 0, 0)),
                   pl.BlockSpec((2, NSA_BLOCK, HEAD_DIM), lambda b, s, pt: (0, 0, 0)),
                   pl.BlockSpec((None, None, wk, win_buf.shape[-1]), lambda b, s, pt: (layer, b, 0, 0))]
                  + _page_specs(PAGE_SIZE * 4 * NSA_KV_HEADS, HEAD_DIM, layer)),
        out_specs=pl.BlockSpec((None, rows, width), lambda b, s, pt: (b, 0, 0)),
        scratch_shapes=[
            pltpu.VMEM((NSA_KV_HEADS, nbp, HEAD_DIM), f32), pltpu.VMEM((NSA_KV_HEADS, nbp, HEAD_DIM), f32),
            pltpu.VMEM((NSA_KV_HEADS, past + cw, HEAD_DIM), bf16), pltpu.VMEM((NSA_KV_HEADS, past + cw, HEAD_DIM), bf16),
            pltpu.VMEM((n_steps + 1, rows, cw), f32),
            pltpu.VMEM((NSA_HPG * rows, 1), f32), pltpu.VMEM((NSA_HPG * rows, 1), f32),
            pltpu.VMEM((NSA_HPG * rows, HEAD_DIM), f32),
        ])
    return pl.pallas_call(
        functools.partial(_nsa_sample_kernel, n_steps=n_steps, past=past, t_new=t_new),
        grid_spec=grid_spec,
        out_shape=jax.ShapeDtypeStruct((B, rows, width), f32),
        compiler_params=pltpu.CompilerParams(
            dimension_semantics=("parallel", "arbitrary"), vmem_limit_bytes=VMEM_LIMIT),
        name="nsa_sample",
    )(page_table, us, cmp_w, win_buf, *([nsa_pool] * SMP_NPG))


REC_CHUNK = 64
REC_TC = 512
CONV_PAD = 8


def _dot_b(a, b):
    return jnp.dot(a.astype(bf16), b.astype(bf16), preferred_element_type=f32)


def _dot_nt_b(a, b):
    return _dot_nt(a.astype(bf16), b.astype(bf16))


def _dot_tn_b(a, b):
    return _dot_b(a.T, b)


def _dot_3(a, b):
    ah, bh = a.astype(bf16), b.astype(bf16)
    al, bl = (a - ah.astype(f32)).astype(bf16), (b - bh.astype(f32)).astype(bf16)
    return (jnp.dot(ah, bh, preferred_element_type=f32) + jnp.dot(ah, bl, preferred_element_type=f32)
            + jnp.dot(al, bh, preferred_element_type=f32))


def _softplus(x):
    return jnp.maximum(x, 0.0) + jnp.log(1.0 + jnp.exp(-jnp.abs(x)))


def _silu(x):
    return x * jax.nn.sigmoid(x)


def _conv_block(xbuf_scr, raw, w, first, conv0):
    tc = raw.shape[0]
    nh = CONV_WIDTH - 1

    @pl.when(first)
    def _():
        xbuf_scr[CONV_PAD - nh:CONV_PAD, :] = conv0

    xbuf_scr[CONV_PAD:CONV_PAD + tc, :] = raw
    y = xbuf_scr[pl.ds(CONV_PAD - nh, tc), :] * w[0:1, :]
    for j in range(1, CONV_WIDTH):
        y = y + xbuf_scr[pl.ds(CONV_PAD - nh + j, tc), :] * w[j:j + 1, :]
    xbuf_scr[CONV_PAD - nh:CONV_PAD, :] = raw[tc - nh:tc, :]
    return y


def _lane_pick(x, idx):
    lane = lax.broadcasted_iota(jnp.int32, x.shape, 1)
    return jnp.sum(jnp.where(lane == idx, x, 0.0), axis=-1, keepdims=True)


def _chunk_cumsum(col, row):
    c = col.shape[0]
    i = lax.broadcasted_iota(jnp.int32, (c, c), 0)
    j = lax.broadcasted_iota(jnp.int32, (c, c), 1)
    ccol = jnp.sum(jnp.where(j <= i, jnp.broadcast_to(row, (c, c)), 0.0), axis=1, keepdims=True)
    crow = jnp.sum(jnp.where(i <= j, jnp.broadcast_to(col, (c, c)), 0.0), axis=0, keepdims=True)
    return ccol, crow


def _gdn_kernel(alog_ref, dtb_ref, q_ref, k_ref, v_ref, z_ref, sm_ref, smt_ref, wq_ref, wk_ref, wv_ref,
                cq_ref, ck_ref, cv_ref, s0_ref, ng_ref, o_ref, s_out_ref, xbuf_scr, s_scr, *, t_valid):
    tc = q_ref.shape[0]
    C = REC_CHUNK
    nc = tc // C
    hp = pl.program_id(1)
    blk = pl.program_id(2)
    first = blk == 0

    @pl.when(first)
    def _():
        s_scr[...] = s0_ref[...]

    raw = jnp.concatenate([q_ref[...], k_ref[...], v_ref[...]], axis=1)
    w = jnp.concatenate([wq_ref[...], wk_ref[...], wv_ref[...]], axis=1)
    conv0 = jnp.concatenate([cq_ref[...], ck_ref[...], cv_ref[...]], axis=1)
    y = _silu(_conv_block(xbuf_scr, raw, w, first, conv0))
    valid_col = (blk * tc + lax.broadcasted_iota(jnp.int32, (tc, 1), 0)) < t_valid
    valid_row = (blk * tc + lax.broadcasted_iota(jnp.int32, (1, tc), 1)) < t_valid
    sm = sm_ref[...]
    ii = lax.broadcasted_iota(jnp.int32, (C, C), 0)
    jj = lax.broadcasted_iota(jnp.int32, (C, C), 1)
    eye = jnp.where(ii == jj, 1.0, 0.0)

    pieces = []
    for e in range(2):
        h = 2 * hp + e
        qf = y[:, e * GDN_DK:(e + 1) * GDN_DK]
        kf = y[:, (2 + e) * GDN_DK:(3 + e) * GDN_DK]
        vf = y[:, (4 + e) * GDN_DK:(5 + e) * GDN_DK]
        q = jnp.where(valid_col, qf * lax.rsqrt(jnp.sum(qf * qf, axis=-1, keepdims=True) + 1e-6) * (GDN_DK ** -0.5), 0.0)
        k = jnp.where(valid_col, kf * lax.rsqrt(jnp.sum(kf * kf, axis=-1, keepdims=True) + 1e-6), 0.0)
        v = jnp.where(valid_col, vf, 0.0)
        a_scale = -jnp.exp(jnp.full((1, 1), alog_ref[h], f32))
        dtb = dtb_ref[h]
        beta = jnp.where(valid_col, jax.nn.sigmoid(_lane_pick(sm, OSM_BETA + h)), 0.0)
        g_col = jnp.where(valid_col, a_scale * _softplus(_lane_pick(sm, OSM_A + h) + dtb), 0.0)
        g_row = jnp.where(valid_row, a_scale * _softplus(smt_ref[pl.ds(OSM_A + h, 1), :] + dtb), 0.0)
        for c in range(nc):
            r = slice(c * C, (c + 1) * C)
            gc_col, gc_row = _chunk_cumsum(g_col[r], g_row[:, r])
            gam = jnp.where(jj <= ii, jnp.exp(jnp.where(jj <= ii, gc_col - gc_row, 0.0)), 0.0)
            kb = k[r] * beta[r]
            pieces.append(dict(q=q[r], k=k[r], vb=v[r] * beta[r], kb=kb, gc=gc_col, gam=gam,
                               x=jnp.where(jj < ii, -(_dot_nt_b(kb, k[r]) * gam), 0.0)))

    xs = [p['x'] for p in pieces]
    tms = [eye + x for x in xs]
    for _ in range(max(1, (C - 1).bit_length()) - 1):
        xs = [_dot_3(x, x) for x in xs]
        tms = [tm + _dot_3(tm, x) for tm, x in zip(tms, xs)]
    for p, tm in zip(pieces, tms):
        p['u0'] = _dot_b(tm, p['vb'])
        p['wd'] = _dot_b(tm, p['kb'] * jnp.exp(p['gc']))
        p['qk'] = _dot_nt_b(p['q'], p['k']) * p['gam']

    S = [s_scr[0], s_scr[1]]
    outs = [[], []]
    for c in range(nc):
        for e in range(2):
            p = pieces[e * nc + c]
            g_last = p['gc'][C - 1:C, :]
            u = p['u0'] - _dot_b(p['wd'], S[e])
            outs[e].append(_dot_b(p['q'] * jnp.exp(p['gc']), S[e]) + _dot_b(p['qk'], u))
            S[e] = S[e] * jnp.exp(g_last) + _dot_tn_b(p['k'] * jnp.exp(g_last - p['gc']), u)
    z = z_ref[...]
    halves = []
    for e in range(2):
        s_scr[e] = S[e]
        o = jnp.concatenate(outs[e], axis=0) if nc > 1 else outs[e][0]
        o = o * lax.rsqrt(jnp.mean(o * o, axis=-1, keepdims=True) + NORM_EPS) * ng_ref[...]
        halves.append(o * _silu(z[:, e * GDN_DV:(e + 1) * GDN_DV]))
    o_ref[...] = jnp.concatenate(halves, axis=1)

    @pl.when(blk == pl.num_programs(2) - 1)
    def _():
        s_out_ref[...] = s_scr[...]


def _smem_spec():
    return pl.BlockSpec(memory_space=pltpu.SMEM)


def _gdn(u, smt, conv_w, conv0, s0, a_log, dt_bias, norm_g, layer, t_valid):
    B, Tp, _ = u.shape
    tc = min(REC_TC, Tp)
    H = GDN_HEADS
    HP = H // 2
    pw = 2 * GDN_DK
    qkv = lambda part: pl.BlockSpec((None, tc, pw), lambda b, hp, i: (b, i, OC_QKV // pw + part * HP + hp))
    cw = lambda part: pl.BlockSpec((None, CONV_WIDTH, pw), lambda b, hp, i: (layer, 0, part * HP + hp))
    c0 = lambda part: pl.BlockSpec((None, CONV_WIDTH - 1, pw), lambda b, hp, i: (b, 0, part * HP + hp))
    return pl.pallas_call(
        functools.partial(_gdn_kernel, t_valid=t_valid),
        grid=(B, HP, Tp // tc),
        in_specs=[
            _smem_spec(), _smem_spec(),
            qkv(0), qkv(1), qkv(2),
            pl.BlockSpec((None, tc, pw), lambda b, hp, i: (b, i, OC_ZC // pw + hp)),
            pl.BlockSpec((None, tc, 128), lambda b, hp, i: (b, i, OC_SM // 128)),
            pl.BlockSpec((None, 128, tc), lambda b, hp, i: (b, 0, i)),
            cw(0), cw(1), cw(2), c0(0), c0(1), c0(2),
            pl.BlockSpec((None, 2, GDN_DK, GDN_DV), lambda b, hp, i: (b, hp, 0, 0)),
            pl.BlockSpec((1, GDN_DV), lambda b, hp, i: (0, 0)),
        ],
        out_specs=[
            pl.BlockSpec((None, tc, pw), lambda b, hp, i: (b, i, hp)),
            pl.BlockSpec((None, 2, GDN_DK, GDN_DV), lambda b, hp, i: (b, hp, 0, 0)),
        ],
        out_shape=[jax.ShapeDtypeStruct((B, Tp, H * GDN_DV), f32),
                   jax.ShapeDtypeStruct((B, H, GDN_DK, GDN_DV), f32)],
        scratch_shapes=[pltpu.VMEM((CONV_PAD + tc, 3 * pw), f32), pltpu.VMEM((2, GDN_DK, GDN_DV), f32)],
        compiler_params=pltpu.CompilerParams(
            dimension_semantics=("parallel", "parallel", "arbitrary"), vmem_limit_bytes=VMEM_LIMIT),
        name="gdn_scan",
    )(a_log, dt_bias, u, u, u, u, u, smt, conv_w, conv_w, conv_w, conv0, conv0, conv0, s0, norm_g.reshape(1, -1))


def _ssd_kernel(alog_ref, dtb_ref, d_ref, x_ref, b_ref, c_ref, z_ref, sm_ref, smt_ref, wx_ref, wb_ref, wc_ref,
                bx_ref, bb_ref, bc_ref, cx_ref, cb_ref, cc_ref, h0_ref, o_ref, h_out_ref, xbuf_scr, h_scr,
                *, t_valid):
    tc = x_ref.shape[0]
    C = REC_CHUNK
    P = SSD_HEAD_DIM
    hp = pl.program_id(1)
    blk = pl.program_id(2)
    first = blk == 0

    @pl.when(first)
    def _():
        h_scr[...] = h0_ref[...]

    raw = jnp.concatenate([x_ref[...], b_ref[...], c_ref[...]], axis=1)
    w = jnp.concatenate([wx_ref[...], wb_ref[...], wc_ref[...]], axis=1)
    bias = jnp.concatenate([bx_ref[...], bb_ref[...], bc_ref[...]], axis=1)
    conv0 = jnp.concatenate([cx_ref[...], cb_ref[...], cc_ref[...]], axis=1)
    y = _silu(_conv_block(xbuf_scr, raw, w, first, conv0) + bias)
    valid_col = (blk * tc + lax.broadcasted_iota(jnp.int32, (tc, 1), 0)) < t_valid
    valid_row = (blk * tc + lax.broadcasted_iota(jnp.int32, (1, tc), 1)) < t_valid
    xs = jnp.where(valid_col, y[:, 0:128], 0.0)
    bm = jnp.where(valid_col, y[:, 128:256], 0.0)
    cm = jnp.where(valid_col, y[:, 256:384], 0.0)
    sm = sm_ref[...]
    z = z_ref[...]
    ii = lax.broadcasted_iota(jnp.int32, (C, C), 0)
    jj = lax.broadcasted_iota(jnp.int32, (C, C), 1)
    cbs = [_dot_nt_b(cm[c * C:(c + 1) * C], bm[c * C:(c + 1) * C]) for c in range(tc // C)]
    halves = []
    for e in range(2):
        hh = 2 * hp + e
        a_neg = -jnp.exp(jnp.full((1, 1), alog_ref[hh], f32))
        dtb = dtb_ref[hh]
        dt_col = jnp.where(valid_col, _softplus(_lane_pick(sm, OSM_DT + hh) + dtb), 0.0)
        dt_row = jnp.where(valid_row, _softplus(smt_ref[pl.ds(OSM_DT + hh, 1), :] + dtb), 0.0)
        xh = xs[:, e * P:(e + 1) * P]
        xdt = xh * dt_col
        hst = h_scr[e]
        outs = []
        for c in range(tc // C):
            r = slice(c * C, (c + 1) * C)
            acs_col, acs_row = _chunk_cumsum(dt_col[r] * a_neg, dt_row[:, r] * a_neg)
            lm = jnp.where(jj <= ii, jnp.exp(jnp.where(jj <= ii, acs_col - acs_row, 0.0)), 0.0)
            a_last = acs_col[C - 1:C, :]
            y_diag = _dot_b(cbs[c] * lm, xdt[r])
            y_off = _dot_nt_b(cm[r] * jnp.exp(acs_col), hst)
            outs.append(y_diag + y_off)
            hst = hst * jnp.exp(a_last) + _dot_tn_b(xdt[r] * jnp.exp(a_last - acs_col), bm[r])
        h_scr[e] = hst
        yh = jnp.concatenate(outs, axis=0) if len(outs) > 1 else outs[0]
        halves.append((yh + d_ref[hh] * xh) * _silu(z[:, e * P:(e + 1) * P]))
    o_ref[...] = jnp.concatenate(halves, axis=1)

    @pl.when(blk == pl.num_programs(2) - 1)
    def _():
        h_out_ref[...] = h_scr[...]


def _ssd(u, smt, conv_w, conv_b, conv0, h0, a_log, dt_bias, d_skip, layer, t_valid):
    B, Tp, _ = u.shape
    tc = min(REC_TC, Tp)
    HP = SSD_HEADS // 2
    hpg = SSD_HEADS // SSD_GROUPS // 2
    xcol = lambda hp: hp
    bcol = lambda hp: SSD_D_INNER // 128 + hp // hpg
    ccol = lambda hp: (SSD_D_INNER + SSD_GROUPS * SSD_STATE) // 128 + hp // hpg
    def tri(fn_col):
        return (pl.BlockSpec((None, tc, 128), lambda b, hp, i: (b, i, OC_XBC // 128 + fn_col(hp))),
                pl.BlockSpec((None, CONV_WIDTH, 128), lambda b, hp, i: (layer, 0, fn_col(hp))),
                pl.BlockSpec((None, 1, 128), lambda b, hp, i: (layer, 0, fn_col(hp))),
                pl.BlockSpec((None, CONV_WIDTH - 1, 128), lambda b, hp, i: (b, 0, fn_col(hp))))
    (xs, wx, bx, cx), (bs, wb, bb, cb), (cs, wc, bc, cc) = tri(xcol), tri(bcol), tri(ccol)
    return pl.pallas_call(
        functools.partial(_ssd_kernel, t_valid=t_valid),
        grid=(B, HP, Tp // tc),
        in_specs=[
            _smem_spec(), _smem_spec(), _smem_spec(),
            xs, bs, cs,
            pl.BlockSpec((None, tc, 128), lambda b, hp, i: (b, i, OC_ZD // 128 + hp)),
            pl.BlockSpec((None, tc, 128), lambda b, hp, i: (b, i, OC_SM // 128)),
            pl.BlockSpec((None, 128, tc), lambda b, hp, i: (b, 0, i)),
            wx, wb, wc, bx, bb, bc, cx, cb, cc,
            pl.BlockSpec((None, 2, SSD_HEAD_DIM, SSD_STATE), lambda b, hp, i: (b, hp, 0, 0)),
        ],
        out_specs=[
            pl.BlockSpec((None, tc, 128), lambda b, hp, i: (b, i, hp)),
            pl.BlockSpec((None, 2, SSD_HEAD_DIM, SSD_STATE), lambda b, hp, i: (b, hp, 0, 0)),
        ],
        out_shape=[jax.ShapeDtypeStruct((B, Tp, SSD_D_INNER), f32),
                   jax.ShapeDtypeStruct((B, SSD_HEADS, SSD_HEAD_DIM, SSD_STATE), f32)],
        scratch_shapes=[pltpu.VMEM((CONV_PAD + tc, 3 * 128), f32), pltpu.VMEM((2, SSD_HEAD_DIM, SSD_STATE), f32)],
        compiler_params=pltpu.CompilerParams(
            dimension_semantics=("parallel", "parallel", "arbitrary"), vmem_limit_bytes=VMEM_LIMIT),
        name="ssd_scan",
    )(a_log, dt_bias, d_skip, u, u, u, u, u, smt, conv_w, conv_w, conv_w, conv_b, conv_b, conv_b,
      conv0, conv0, conv0, h0)


def _split(u, sizes):
    return jnp.split(u, np.cumsum(sizes)[:-1].tolist(), axis=-1)


def _even_mixer(u, cmp_w, cache, i):
    B, T, _ = u.shape
    kva = u[..., COL_KVA:COL_KVB].reshape(B, T, 6, NSA_KV_HEADS, HEAD_DIM)
    kvb = u[..., COL_KVB:COL_QI].reshape(B, T, 2, DSA_KV_HEADS, HEAD_DIM)
    ki = u[..., COL_SM + SM_KI:COL_SM + SM_GA]
    nsa_rows, win_rows = kva[:, :, :4], kva[:, :, 4:]
    if cache is None:
        o_a = _nsa_prompt(u, cmp_w)
        o_b = _dsa_prompt(u, min(DSA_TOPK, T // 4))
        return (o_a, o_b), (nsa_rows, kvb, ki, win_rows[:, T - min(NSA_WINDOW, T):])
    flat = lambda a: a.reshape(a.shape[:3] + (-1,))
    us = jnp.pad(u, ((0, 0), (0, SMP_ROWS - T), (0, 0)))
    pt = cache['page_table']
    o_a = _nsa_sample(us, cmp_w, flat(cache['nsa_win']), _page_rows_view(cache['nsa_kv']), pt, i, T)[:, :T]
    idx_t = jnp.swapaxes(cache['dsa_idx_k'], 2, 3)
    o_b = _dsa_sample(us, idx_t, _page_rows_view(cache['dsa_kv']), pt, i, T)[:, :T]
    new_win = jnp.concatenate([cache['nsa_win'][i], win_rows], axis=1)[:, T:]
    return (o_a, o_b), (nsa_rows, kvb, ki, new_win)


def _odd_mixer(u, W, i, init, t_valid):
    S0, conv_c0, h0, conv_d0 = init
    nr = W['ssd_conv_b'].shape[0]
    smt = jnp.swapaxes(u[..., OC_SM:OC_SM + 128], 1, 2)
    o_c, S = _gdn(u, smt, W['gdn_conv_w'], conv_c0, S0, W['gdn_a_log'][i], W['gdn_dt_bias'][i],
                  W['gdn_norm_g'][i], i, t_valid)
    y, hN = _ssd(u, smt, W['ssd_conv_w'], W['ssd_conv_b'].reshape(nr, 1, -1), conv_d0, h0, W['ssd_a_log'][i],
                 W['ssd_dt_bias'][i], W['ssd_d'][i], i, t_valid)

    def conv_state(buf, lo, width):
        rows = u[:, :t_valid, lo:lo + width]
        nh = CONV_WIDTH - 1
        return rows[:, t_valid - nh:] if t_valid >= nh else jnp.concatenate([buf, rows], axis=1)[:, -nh:]

    return (o_c[:, :t_valid], y[:, :t_valid]), (S, conv_state(conv_c0, OC_QKV, GDN_CONV_DIM), hN,
                                               conv_state(conv_d0, OC_XBC, SSD_CONV_DIM))


def _trunk(x, W, cache):
    B, T, D = x.shape
    x = x.reshape(B * T, D)
    attn_new, rec_new = [], []
    for l in range(DEPTH):
        i = l // 2
        x = _ffn(x, W['norm_g'][l, 0], W['ffn_wg'], W['ffn_wu'], W['ffn_wd'], l, 0)
        if l % 2 == 0:
            u = _inproj(x, W['norm_g'][l, 1], W['attn_w_in'], i).reshape(B, T, -1)
            (o0, o1), st = _even_mixer(u, W['nsa_cmp_w'][i], cache, i)
            attn_new.append(st)
            x = _outproj(x, o0.reshape(B * T, -1), o1.reshape(B * T, -1), W['attn_w_out'], i)
        else:
            u = _inproj(x, W['norm_g'][l, 1], W['rec_w_in'], i).reshape(B, T, -1)
            u = jnp.pad(u, ((0, 0), (0, _round_up(T, REC_CHUNK) - T), (0, 0)))
            if cache is None:
                init = (jnp.zeros((B, GDN_HEADS, GDN_DK, GDN_DV), f32),
                        jnp.zeros((B, CONV_WIDTH - 1, GDN_CONV_DIM), f32),
                        jnp.zeros((B, SSD_HEADS, SSD_HEAD_DIM, SSD_STATE), f32),
                        jnp.zeros((B, CONV_WIDTH - 1, SSD_CONV_DIM), f32))
            else:
                init = (cache['gdn'][i], cache['gdn_conv'][i], cache['ssd'][i], cache['ssd_conv'][i])
            (o0, o1), st = _odd_mixer(u, W, i, init, T)
            rec_new.append(st)
            x = _outproj(x, o0.reshape(B * T, -1), o1.reshape(B * T, -1), W['rec_w_out'], i,
                         g1=W['ssd_norm_g'][i], norm_groups=SSD_GROUPS)
        x = _ffn(x, W['norm_g'][l, 2], W['ffn_wg'], W['ffn_wu'], W['ffn_wd'], l, 1,
                 final_g=W['final_norm_g'] if l == DEPTH - 1 else None)
    y = x.reshape(B, T, D)

    def stack(lst, j):
        return jnp.stack([s[j] for s in lst])

    return (y, stack(attn_new, 0), stack(attn_new, 1), stack(attn_new, 2), stack(attn_new, 3),
            stack(rec_new, 0), stack(rec_new, 1), stack(rec_new, 2), stack(rec_new, 3))


def _pad_cols(w, n):
    return jnp.pad(w, ((0, 0), (0, 0), (0, n - w.shape[-1])))


def _pack_even_w(w):
    qa, kva, ga, qb, kvb, qi, ki, wi = _split(w, EVEN_SIZES)
    return _pad_cols(jnp.concatenate([qa, qb, kva, kvb, qi, ki, ga, wi], axis=-1), _round_up(EVEN_IN, PROJ_TN))


def _pack_odd_w(w):
    qkv, beta, a, zc, zd, xbc, dt = _split(w, ODD_SIZES)
    return _pad_cols(jnp.concatenate([qkv, zc, zd, xbc, beta, a, dt], axis=-1), _round_up(OC_SM + 128, PROJ_TN))


def kernel(x_prompt, x_sample, cache_nsa_kv, cache_dsa_kv, cache_dsa_idx_k, page_table, state_nsa_win,
           state_gdn, state_gdn_conv, state_ssd, state_ssd_conv, norm_g, final_norm_g, ffn_w_gate, ffn_w_up,
           ffn_w_down, attn_w_in, attn_w_out, nsa_cmp_w, rec_w_in, rec_w_out, gdn_conv_w, gdn_a_log,
           gdn_dt_bias, gdn_norm_g, ssd_conv_w, ssd_conv_b, ssd_dt_bias, ssd_a_log, ssd_d, ssd_norm_g):
    W = {'norm_g': norm_g, 'final_norm_g': final_norm_g,
         'ffn_wg': ffn_w_gate.astype(bf16), 'ffn_wu': ffn_w_up.astype(bf16), 'ffn_wd': ffn_w_down.astype(bf16),
         'attn_w_in': _pack_even_w(attn_w_in).astype(bf16),
         'attn_w_out': attn_w_out.astype(bf16), 'nsa_cmp_w': nsa_cmp_w,
         'rec_w_in': _pack_odd_w(rec_w_in).astype(bf16),
         'rec_w_out': rec_w_out.astype(bf16), 'gdn_conv_w': gdn_conv_w, 'gdn_a_log': gdn_a_log,
         'gdn_dt_bias': gdn_dt_bias, 'gdn_norm_g': gdn_norm_g, 'ssd_conv_w': ssd_conv_w, 'ssd_conv_b': ssd_conv_b,
         'ssd_dt_bias': ssd_dt_bias, 'ssd_a_log': ssd_a_log, 'ssd_d': ssd_d, 'ssd_norm_g': ssd_norm_g}
    cache = {'nsa_kv': cache_nsa_kv, 'dsa_kv': cache_dsa_kv, 'dsa_idx_k': cache_dsa_idx_k,
             'page_table': page_table, 'nsa_win': state_nsa_win, 'gdn': state_gdn, 'gdn_conv': state_gdn_conv,
             'ssd': state_ssd, 'ssd_conv': state_ssd_conv}
    (y_prompt, p_nsa_kv, p_dsa_kv, p_dsa_idx_k, p_nsa_win,
     p_gdn, p_gdn_conv, p_ssd, p_ssd_conv) = _trunk(x_prompt, W, None)
    (y_sample, s_nsa_kv, s_dsa_kv, s_dsa_idx_k, s_nsa_win,
     s_gdn, s_gdn_conv, s_ssd, s_ssd_conv) = _trunk(x_sample, W, cache)
    return (y_prompt, y_sample, p_nsa_kv, p_dsa_kv, p_dsa_idx_k, p_nsa_win, p_gdn, p_gdn_conv, p_ssd, p_ssd_conv,
            s_nsa_kv, s_dsa_kv, s_dsa_idx_k, s_nsa_win, s_gdn, s_gdn_conv, s_ssd, s_ssd_conv)
```

```python
import functools
import math

import jax
import jax.numpy as jnp
import numpy as np
from jax import lax
from jax.experimental import pallas as pl
from jax.experimental.pallas import tpu as pltpu

D_MODEL = 2048
DEPTH = 4
PAGE_SIZE = 128
HEAD_DIM = 128
NSA_HEADS = 8
NSA_KV_HEADS = 2
NSA_HPG = NSA_HEADS // NSA_KV_HEADS
NSA_BLOCK = 64
NSA_N_SEL = 16
NSA_WINDOW = 512
NSA_FORCE = 1.0e4
SLC_Q_BLOCK = 64
DSA_HEADS = 8
DSA_KV_HEADS = 2
DSA_HPG = DSA_HEADS // DSA_KV_HEADS
IDX_HEADS = 8
IDX_DIM = 64
DSA_TOPK = 256
Q_BLOCK = 128
GDN_HEADS = 8
GDN_DK = 128
GDN_DV = 128
GDN_CHUNK = 64
CONV_WIDTH = 4
SSD_D_INNER = D_MODEL // 2
SSD_HEAD_DIM = 64
SSD_HEADS = SSD_D_INNER // SSD_HEAD_DIM
SSD_GROUPS = 2
SSD_STATE = 128
SSD_CHUNK = 64
D_FF = 5632
NORM_EPS = 1e-6

EVEN_SIZES = (NSA_HEADS * HEAD_DIM, 6 * NSA_KV_HEADS * HEAD_DIM, 3 * NSA_HEADS,
              DSA_HEADS * HEAD_DIM, 2 * DSA_KV_HEADS * HEAD_DIM,
              IDX_HEADS * IDX_DIM, IDX_DIM, IDX_HEADS)
EVEN_IN = sum(EVEN_SIZES)
GDN_CONV_DIM = GDN_HEADS * (2 * GDN_DK + GDN_DV)
SSD_CONV_DIM = SSD_D_INNER + 2 * SSD_GROUPS * SSD_STATE
ODD_SIZES = (GDN_CONV_DIM, GDN_HEADS, GDN_HEADS, GDN_HEADS * GDN_DV,
             SSD_D_INNER, SSD_CONV_DIM, SSD_HEADS)
ODD_IN = sum(ODD_SIZES)

COL_QA = 0
COL_QB = COL_QA + NSA_HEADS * HEAD_DIM
COL_KVA = COL_QB + DSA_HEADS * HEAD_DIM
COL_KVB = COL_KVA + 6 * NSA_KV_HEADS * HEAD_DIM
COL_QI = COL_KVB + 2 * DSA_KV_HEADS * HEAD_DIM
COL_SM = COL_QI + IDX_HEADS * IDX_DIM
SM_KI = 0
SM_GA = SM_KI + IDX_DIM
SM_WI = SM_GA + 3 * NSA_HEADS
OC_QKV = 0
OC_ZC = OC_QKV + GDN_CONV_DIM
OC_ZD = OC_ZC + GDN_HEADS * GDN_DV
OC_XBC = OC_ZD + SSD_D_INNER
OC_SM = OC_XBC + SSD_CONV_DIM
OSM_BETA = 0
OSM_A = OSM_BETA + GDN_HEADS
OSM_DT = OSM_A + GDN_HEADS

V7X_VMEM_BYTES = 64 * 1024 * 1024
VMEM_LIMIT = V7X_VMEM_BYTES - 12 * 1024 * 1024
PROJ_TN = 1024
FFN_TF = 512

bf16 = jnp.bfloat16
f32 = jnp.float32


def _round_up(n, m):
    return -(-n // m) * m


def _row_tile(m):
    return 512 if m % 512 == 0 else m


def _rms(x, g):
    return x * lax.rsqrt(jnp.mean(x * x, axis=-1, keepdims=True) + NORM_EPS) * g


def _ffn_kernel(x_ref, g_ref, wg_ref, wu_ref, wd_ref, fg_ref, o_ref, h_scr, acc_scr, *, final):
    f = pl.program_id(1)

    @pl.when(f == 0)
    def _():
        h_scr[...] = _rms(x_ref[...], g_ref[...]).astype(bf16)
        acc_scr[...] = jnp.zeros_like(acc_scr)

    h = h_scr[...]
    gate = jnp.dot(h, wg_ref[...], preferred_element_type=f32)
    up = jnp.dot(h, wu_ref[...], preferred_element_type=f32)
    act = (gate * jax.nn.sigmoid(gate) * up).astype(bf16)
    acc_scr[...] += jnp.dot(act, wd_ref[...], preferred_element_type=f32)

    @pl.when(f == pl.num_programs(1) - 1)
    def _():
        y = x_ref[...] + 0.5 * acc_scr[...]
        if final:
            y = _rms(y, fg_ref[...])
        o_ref[...] = y


def _ffn(x, g, wg, wu, wd, l, j, final_g=None):
    m, d = x.shape
    tm = _row_tile(m)
    final = final_g is not None
    fg = final_g if final else g
    return pl.pallas_call(
        functools.partial(_ffn_kernel, final=final),
        grid=(m // tm, D_FF // FFN_TF),
        in_specs=[
            pl.BlockSpec((tm, d), lambda i, f: (i, 0)),
            pl.BlockSpec((1, d), lambda i, f: (0, 0)),
            pl.BlockSpec((None, None, d, FFN_TF), lambda i, f: (l, j, 0, f)),
            pl.BlockSpec((None, None, d, FFN_TF), lambda i, f: (l, j, 0, f)),
            pl.BlockSpec((None, None, FFN_TF, d), lambda i, f: (l, j, f, 0)),
            pl.BlockSpec((1, d), lambda i, f: (0, 0)),
        ],
        out_specs=pl.BlockSpec((tm, d), lambda i, f: (i, 0)),
        out_shape=jax.ShapeDtypeStruct((m, d), f32),
        scratch_shapes=[pltpu.VMEM((tm, d), bf16), pltpu.VMEM((tm, d), f32)],
        compiler_params=pltpu.CompilerParams(
            dimension_semantics=("parallel", "arbitrary"), vmem_limit_bytes=VMEM_LIMIT),
        name="ffn_half",
    )(x, g.reshape(1, d), wg, wu, wd, fg.reshape(1, d))


def _inproj_kernel(x_ref, g_ref, w_ref, o_ref, h_scr):
    @pl.when(pl.program_id(1) == 0)
    def _():
        h_scr[...] = _rms(x_ref[...], g_ref[...]).astype(bf16)

    o_ref[...] = jnp.dot(h_scr[...], w_ref[...], preferred_element_type=f32)


def _inproj(x, g, w, i):
    m, d = x.shape
    n = w.shape[-1]
    tm = _row_tile(m)
    return pl.pallas_call(
        _inproj_kernel,
        grid=(m // tm, n // PROJ_TN),
        in_specs=[
            pl.BlockSpec((tm, d), lambda r, c: (r, 0)),
            pl.BlockSpec((1, d), lambda r, c: (0, 0)),
            pl.BlockSpec((None, d, PROJ_TN), lambda r, c: (i, 0, c)),
        ],
        out_specs=pl.BlockSpec((tm, PROJ_TN), lambda r, c: (r, c)),
        out_shape=jax.ShapeDtypeStruct((m, n), f32),
        scratch_shapes=[pltpu.VMEM((tm, d), bf16)],
        compiler_params=pltpu.CompilerParams(
            dimension_semantics=("parallel", "arbitrary"), vmem_limit_bytes=VMEM_LIMIT),
        name="mixer_in_proj",
    )(x, g.reshape(1, d), w)


def _outproj_kernel(x_ref, a0_ref, a1_ref, w0_ref, w1_ref, g1_ref, o_ref, *, norm_groups):
    a1 = a1_ref[...]
    if norm_groups:
        gw = a1.shape[-1] // norm_groups
        a1 = jnp.concatenate([_rms(a1[:, j * gw:(j + 1) * gw], g1_ref[:, j * gw:(j + 1) * gw])
                              for j in range(norm_groups)], axis=1)
    o_ref[...] = (x_ref[...]
                  + jnp.dot(a0_ref[...].astype(bf16), w0_ref[...], preferred_element_type=f32)
                  + jnp.dot(a1.astype(bf16), w1_ref[...], preferred_element_type=f32))


def _outproj(x, a0, a1, w, i, g1=None, norm_groups=0):
    m, d = x.shape
    k = a0.shape[-1]
    tm = _row_tile(m)
    g1 = jnp.ones((k,), f32) if g1 is None else g1
    return pl.pallas_call(
        functools.partial(_outproj_kernel, norm_groups=norm_groups),
        grid=(m // tm, d // PROJ_TN),
        in_specs=[
            pl.BlockSpec((tm, PROJ_TN), lambda r, c: (r, c)),
            pl.BlockSpec((tm, k), lambda r, c: (r, 0)),
            pl.BlockSpec((tm, k), lambda r, c: (r, 0)),
            pl.BlockSpec((None, k, PROJ_TN), lambda r, c: (i, 0, c)),
            pl.BlockSpec((None, k, PROJ_TN), lambda r, c: (i, 1, c)),
            pl.BlockSpec((1, k), lambda r, c: (0, 0)),
        ],
        out_specs=pl.BlockSpec((tm, PROJ_TN), lambda r, c: (r, c)),
        out_shape=jax.ShapeDtypeStruct((m, d), f32),
        compiler_params=pltpu.CompilerParams(
            dimension_semantics=("parallel", "arbitrary"), vmem_limit_bytes=VMEM_LIMIT),
        name="mixer_out_proj",
    )(x, a0, a1, w, w, g1.reshape(1, k))


ATT_TQ = 256
ATT_KC = 1024
MASK_NEG = -1e30
INT_MIN = -2 ** 31
INT_MAX = 2 ** 31 - 1


def _stack_heads(q, first, n, scale):
    rows = jnp.concatenate([q[:, (first + h) * HEAD_DIM:(first + h + 1) * HEAD_DIM] for h in range(n)], axis=0)
    return (rows * scale).astype(bf16)


def _dot_nt(a, b):
    return lax.dot_general(a, b, (((1,), (1,)), ((), ())), preferred_element_type=f32)


def _flash_chunks(qg, k_ref, v_ref, bias_scr, nvis, m_scr, l_scr, acc_scr, reps):
    m_scr[...] = jnp.full_like(m_scr, MASK_NEG)
    l_scr[...] = jnp.zeros_like(l_scr)
    acc_scr[...] = jnp.zeros_like(acc_scr)
    cw = bias_scr.shape[-1]

    def scores(c):
        r0 = pl.multiple_of(c * cw, cw)
        return _dot_nt(qg, k_ref[pl.ds(r0, cw), :].astype(bf16))

    def update(c, s):
        r0 = pl.multiple_of(c * cw, cw)
        vc = v_ref[pl.ds(r0, cw), :].astype(bf16)
        s = s + jnp.concatenate([bias_scr[c]] * reps, axis=0)
        m_old = m_scr[...]
        m_new = jnp.maximum(m_old, jnp.max(s, axis=-1, keepdims=True))
        alpha = jnp.exp(m_old - m_new)
        p = jnp.exp(s - m_new)
        pv = jnp.dot(p.astype(bf16), jnp.concatenate([vc, jnp.ones_like(vc)], axis=1), preferred_element_type=f32)
        l_scr[...] = alpha * l_scr[...] + pv[:, HEAD_DIM:HEAD_DIM + 1]
        acc_scr[...] = alpha * acc_scr[...] + pv[:, 0:HEAD_DIM]
        m_scr[...] = m_new

    def body(c, s):
        s_next = scores(c + 1)
        update(c, s)
        return s_next

    s_last = lax.fori_loop(0, nvis - 1, body, scores(0))
    update(nvis - 1, s_last)
    return acc_scr[...] / jnp.maximum(l_scr[...], 1e-30)


def _index_scores(qi_h, wi, ki, keys_on_lanes=False):
    n = ki.shape[1] if keys_on_lanes else ki.shape[0]
    acc = jnp.zeros((qi_h[0].shape[0], n), f32)
    for h in range(IDX_HEADS):
        qk = jnp.dot(qi_h[h], ki, preferred_element_type=f32) if keys_on_lanes else _dot_nt(qi_h[h], ki)
        rel = jnp.maximum(qk * (IDX_DIM ** -0.5), 0.0)
        acc = acc + rel * wi[:, h:h + 1]
    return acc


def _order_key(x):
    bits = lax.bitcast_convert_type(x, jnp.int32)
    return jnp.where(bits < 0, bits ^ INT_MAX, bits)


def _topk_bias(key_scr, bias_scr, cut_scr, nvis, qpos, n_keep, idx_bits):
    rows, cw = qpos.shape
    lane = lax.broadcasted_iota(jnp.int32, (rows, cw), 1)

    def count(indicator):
        def body(c, acc):
            one = indicator(key_scr[c], c)
            for j in range(cw // 128):
                acc = acc + one[:, j * 128:(j + 1) * 128]
            return acc
        acc = lax.fori_loop(0, nvis, body, jnp.zeros((rows, 128), jnp.int32))
        return jnp.sum(acc, axis=-1, keepdims=True)

    thr = jnp.where(count(lambda k, c: jnp.where(k >= 0, 1, 0)) >= n_keep, 0, INT_MIN)

    def bit_body(i, thr):
        cand = thr | lax.shift_left(jnp.int32(1), 30 - i)
        return jnp.where(count(lambda k, c: jnp.where(k >= cand, 1, 0)) >= n_keep, cand, thr)

    thr = lax.fori_loop(0, 31, bit_body, thr)
    n_gt = count(lambda k, c: jnp.where(k > thr, 1, 0))
    n_ge = count(lambda k, c: jnp.where(k >= thr, 1, 0))
    need = n_keep - n_gt
    tie_rows = jnp.where(n_ge - n_gt > need, jnp.where(thr > INT_MIN, 1, 0), 0)
    cut_scr[...] = jnp.full((rows, 1), INT_MAX, jnp.int32)

    @pl.when(jnp.max(tie_rows) > 0)
    def _():
        def idx_body(i, cut):
            cand = cut | lax.shift_left(jnp.int32(1), idx_bits - 1 - i)
            n = count(lambda k, c: jnp.where(k == thr, jnp.where(c * cw + lane < cand, 1, 0), 0))
            return jnp.where(n < need, cand, cut)
        cut_scr[...] = lax.fori_loop(0, idx_bits, idx_body, jnp.zeros((rows, 1), jnp.int32))

    cut = cut_scr[...]

    def bias_chunk(c, carry):
        k = key_scr[c]
        kpos = c * cw + lane
        tie = jnp.where(k == thr, jnp.where(kpos <= cut, 0.0, MASK_NEG), MASK_NEG)
        bias_scr[c] = jnp.where(kpos <= qpos, jnp.where(k > thr, 0.0, tie), MASK_NEG)
        return carry

    lax.fori_loop(0, nvis, bias_chunk, 0)


def _dsa_kernel(q_ref, qi_ref, sm_ref, ksm_ref, k0_ref, k1_ref, v0_ref, v1_ref, o_ref,
                key_scr, bias_scr, cut_scr, m_scr, l_scr, acc_scr, *, n_keep, idx_bits):
    tq = ATT_TQ
    t0 = pl.program_id(1) * tq
    nvis = (t0 + tq - 1) // ATT_KC + 1
    qpos = t0 + lax.broadcasted_iota(jnp.int32, (tq, ATT_KC), 0)
    lane = lax.broadcasted_iota(jnp.int32, (tq, ATT_KC), 1)
    wi = sm_ref[...][:, SM_WI:SM_WI + IDX_HEADS] * (IDX_HEADS ** -0.5)
    qi_all = qi_ref[...]
    qi_h = [qi_all[:, h * IDX_DIM:(h + 1) * IDX_DIM].astype(bf16) for h in range(IDX_HEADS)]

    def score_chunk(c, carry):
        r0 = pl.multiple_of(c * ATT_KC, ATT_KC)
        ki_c = ksm_ref[pl.ds(r0, ATT_KC), :][:, SM_KI:SM_KI + IDX_DIM].astype(bf16)
        key_scr[c] = jnp.where(r0 + lane <= qpos, _order_key(_index_scores(qi_h, wi, ki_c)), INT_MIN)
        return carry

    lax.fori_loop(0, nvis, score_chunk, 0)
    _topk_bias(key_scr, bias_scr, cut_scr, nvis, qpos, n_keep, idx_bits)

    q = q_ref[...]
    for g, (k_ref, v_ref) in enumerate(((k0_ref, v0_ref), (k1_ref, v1_ref))):
        qg = _stack_heads(q, g * DSA_HPG, DSA_HPG, HEAD_DIM ** -0.5)
        o = _flash_chunks(qg, k_ref, v_ref, bias_scr, nvis, m_scr, l_scr, acc_scr, DSA_HPG)
        for h in range(DSA_HPG):
            c0 = (g * DSA_HPG + h) * HEAD_DIM
            o_ref[:, c0:c0 + HEAD_DIM] = o[h * tq:(h + 1) * tq]


def _dsa_prompt(u, n_keep):
    B, T, _ = u.shape
    tq = ATT_TQ
    nck = T // ATT_KC
    col = lambda off, w: off // w
    kv = lambda r, g: pl.BlockSpec((None, T, HEAD_DIM), lambda b, i: (b, 0, col(COL_KVB, HEAD_DIM) + 2 * r + g))
    return pl.pallas_call(
        functools.partial(_dsa_kernel, n_keep=n_keep, idx_bits=max(1, (T - 1).bit_length())),
        grid=(B, T // tq),
        in_specs=[
            pl.BlockSpec((None, tq, DSA_HEADS * HEAD_DIM), lambda b, i: (b, i, col(COL_QB, DSA_HEADS * HEAD_DIM))),
            pl.BlockSpec((None, tq, IDX_HEADS * IDX_DIM), lambda b, i: (b, i, col(COL_QI, IDX_HEADS * IDX_DIM))),
            pl.BlockSpec((None, tq, 128), lambda b, i: (b, i, col(COL_SM, 128))),
            pl.BlockSpec((None, T, 128), lambda b, i: (b, 0, col(COL_SM, 128))),
            kv(0, 0), kv(0, 1), kv(1, 0), kv(1, 1),
        ],
        out_specs=pl.BlockSpec((None, tq, DSA_HEADS * HEAD_DIM), lambda b, i: (b, i, 0)),
        out_shape=jax.ShapeDtypeStruct((B, T, DSA_HEADS * HEAD_DIM), f32),
        scratch_shapes=[
            pltpu.VMEM((nck, tq, ATT_KC), jnp.int32), pltpu.VMEM((nck, tq, ATT_KC), f32),
            pltpu.VMEM((tq, 1), jnp.int32),
            pltpu.VMEM((DSA_HPG * tq, 1), f32), pltpu.VMEM((DSA_HPG * tq, 1), f32),
            pltpu.VMEM((DSA_HPG * tq, HEAD_DIM), f32),
        ],
        compiler_params=pltpu.CompilerParams(
            dimension_semantics=("parallel", "arbitrary"), vmem_limit_bytes=VMEM_LIMIT),
        name="dsa_prompt",
    )(u, u, u, u, u, u, u, u)


def _nsa_compress_kernel(x_ref, w_ref, o_ref, *, nb):
    x = x_ref[...].reshape(nb, NSA_BLOCK, HEAD_DIM)
    o_ref[...] = jnp.zeros_like(o_ref)
    o_ref[0:nb, :] = jnp.sum(x * w_ref[...][None], axis=1)


def _nsa_compress(u, cmp_w, nbp):
    B, T, _ = u.shape
    nb = T // NSA_BLOCK
    return pl.pallas_call(
        functools.partial(_nsa_compress_kernel, nb=nb),
        grid=(B, 2, NSA_KV_HEADS),
        in_specs=[
            pl.BlockSpec((None, T, HEAD_DIM), lambda b, r, g: (b, 0, COL_KVA // HEAD_DIM + 2 * r + g)),
            pl.BlockSpec((None, NSA_BLOCK, HEAD_DIM), lambda b, r, g: (r, 0, 0)),
        ],
        out_specs=pl.BlockSpec((None, None, None, nbp, HEAD_DIM), lambda b, r, g: (b, r, g, 0, 0)),
        out_shape=jax.ShapeDtypeStruct((B, 2, NSA_KV_HEADS, nbp, HEAD_DIM), f32),
        compiler_params=pltpu.CompilerParams(dimension_semantics=("parallel", "parallel", "parallel")),
        name="nsa_compress",
    )(u, cmp_w)


def _nsa_compressed(qg, kc, vc, t0, rows, nb):
    hpg = NSA_HPG
    nbp = kc.shape[0]
    blk4 = lax.broadcasted_iota(jnp.int32, (hpg * rows, nbp), 1)
    qpos4 = t0 + (lax.broadcasted_iota(jnp.int32, (hpg * rows, nbp), 0) & (rows - 1))
    vis4 = (blk4 + 1) * NSA_BLOCK - 1 <= qpos4
    s = jnp.where(vis4, _dot_nt(qg, kc.astype(bf16)), MASK_NEG)
    e = jnp.where(vis4, jnp.exp(s - jnp.max(s, axis=-1, keepdims=True)), 0.0)
    p = e / jnp.maximum(jnp.sum(e, axis=-1, keepdims=True), 1e-30)
    o_cmp = jnp.dot(p.astype(bf16), vc.astype(bf16), preferred_element_type=f32)
    imp = p[0:rows]
    for h in range(1, hpg):
        imp = imp + p[h * rows:(h + 1) * rows]
    blk = lax.broadcasted_iota(jnp.int32, (rows, nbp), 1)
    cur = (t0 + lax.broadcasted_iota(jnp.int32, (rows, nbp), 0)) // NSA_BLOCK
    forced = jnp.where(blk == 0, NSA_FORCE, jnp.where(blk == cur, NSA_FORCE, jnp.where(blk == cur - 1, NSA_FORCE, imp)))
    score = jnp.where(blk > cur, -jnp.inf, forced)
    if rows % 128 == 0 and nbp == 128:
        nbr = _round_up(nb, 8)
        st = score.T[0:nbr]
        blk_t = lax.broadcasted_iota(jnp.int32, (nbr, rows), 0)
        cur_t = (t0 + lax.broadcasted_iota(jnp.int32, (nbr, rows), 1)) // NSA_BLOCK
        rank_t = jnp.zeros((nbr, rows), jnp.int32)
        for b2 in range(nb):
            rowv = st[b2:b2 + 1, :]
            rank_t = rank_t + jnp.where(rowv > st, 1, jnp.where(rowv == st, jnp.where(blk_t > b2, 1, 0), 0))
        sel_t = jnp.where(blk_t <= cur_t, jnp.where(rank_t < NSA_N_SEL, 1.0, 0.0), 0.0)
        sel = jnp.concatenate([sel_t, jnp.zeros((nbp - nbr, rows), f32)], axis=0).T.astype(bf16)
        return o_cmp, sel
    rank = jnp.zeros((rows, nbp), jnp.int32)
    for b2 in range(nb):
        colv = score[:, b2:b2 + 1]
        rank = rank + jnp.where(colv > score, 1, jnp.where(colv == score, jnp.where(blk > b2, 1, 0), 0))
    sel = jnp.where(blk <= cur, jnp.where(rank < NSA_N_SEL, 1.0, 0.0), 0.0).astype(bf16)
    return o_cmp, sel


def _nsa_selection_bias(sel, bias_scr, nvis, t0):
    rows, nbp = sel.shape
    cw = bias_scr.shape[-1]
    qpos = t0 + lax.broadcasted_iota(jnp.int32, (rows, cw), 0)
    lane = lax.broadcasted_iota(jnp.int32, (rows, cw), 1)
    blk_row = lax.broadcasted_iota(jnp.int32, (nbp, cw), 0)
    key_blk = lax.broadcasted_iota(jnp.int32, (nbp, cw), 1) // NSA_BLOCK

    def bias_chunk(c, carry):
        expand = jnp.where(key_blk + c * (cw // NSA_BLOCK) == blk_row, 1.0, 0.0).astype(bf16)
        selk = jnp.dot(sel, expand, preferred_element_type=f32)
        bias_scr[c] = jnp.where(c * cw + lane <= qpos, jnp.where(selk > 0.5, 0.0, MASK_NEG), MASK_NEG)
        return carry

    lax.fori_loop(0, nvis, bias_chunk, 0)


def _nsa_window(qg, kw, vw, t0, k0, rows):
    span = kw.shape[0]
    kpos = k0 + lax.broadcasted_iota(jnp.int32, (rows, span), 1)
    qpos = t0 + lax.broadcasted_iota(jnp.int32, (rows, span), 0)
    bias = jnp.where(kpos <= qpos, jnp.where(kpos >= qpos - NSA_WINDOW, 0.0, MASK_NEG), MASK_NEG)
    sw = _dot_nt(qg, kw) + jnp.concatenate([bias] * NSA_HPG, axis=0)
    ew = jnp.exp(sw - jnp.max(sw, axis=-1, keepdims=True))
    pw = ew / jnp.sum(ew, axis=-1, keepdims=True)
    return jnp.dot(pw.astype(bf16), vw, preferred_element_type=f32)


def _nsa_kernel(q_ref, sm_ref, kc_ref, vc_ref, ks_ref, vs_ref, kw_ref, vw_ref, o_ref,
                bias_scr, m_scr, l_scr, acc_scr, *, nb, span):
    tq = ATT_TQ
    hpg = NSA_HPG
    g = pl.program_id(1)
    t0 = pl.program_id(2) * tq
    nvis = (t0 + tq - 1) // ATT_KC + 1
    qg = _stack_heads(q_ref[...], 0, hpg, HEAD_DIM ** -0.5)

    o_cmp, sel = _nsa_compressed(qg, kc_ref[...], vc_ref[...], t0, tq, nb)
    _nsa_selection_bias(sel, bias_scr, nvis, t0)
    o_slc = _flash_chunks(qg, ks_ref, vs_ref, bias_scr, nvis, m_scr, l_scr, acc_scr, hpg)

    start = pl.multiple_of(jnp.maximum(t0 + tq - span, 0), tq)
    o_win = _nsa_window(qg, kw_ref[pl.ds(start, span), :].astype(bf16), vw_ref[pl.ds(start, span), :].astype(bf16),
                        t0, start, tq)

    gates = jax.nn.sigmoid(sm_ref[...][:, SM_GA:SM_GA + 3 * NSA_HEADS])
    for h in range(hpg):
        gh = [jnp.where(g == 0, gates[:, h * 3 + j:h * 3 + j + 1],
                        gates[:, 3 * hpg + h * 3 + j:3 * hpg + h * 3 + j + 1]) for j in range(3)]
        rows = slice(h * tq, (h + 1) * tq)
        o_ref[:, h * HEAD_DIM:(h + 1) * HEAD_DIM] = gh[0] * o_cmp[rows] + gh[1] * o_slc[rows] + gh[2] * o_win[rows]


def _nsa_prompt(u, cmp_w):
    B, T, _ = u.shape
    tq = ATT_TQ
    nb = T // NSA_BLOCK
    nbp = _round_up(nb, 128)
    span = min(NSA_WINDOW + tq, T)
    kcv = _nsa_compress(u, cmp_w, nbp)
    gw = NSA_HPG * HEAD_DIM
    kva = lambda r: pl.BlockSpec((None, T, HEAD_DIM), lambda b, g, i: (b, 0, COL_KVA // HEAD_DIM + 2 * r + g))
    kc = lambda r: pl.BlockSpec((None, None, None, nbp, HEAD_DIM), lambda b, g, i: (b, r, g, 0, 0))
    nck = T // ATT_KC
    return pl.pallas_call(
        functools.partial(_nsa_kernel, nb=nb, span=span),
        grid=(B, NSA_KV_HEADS, T // tq),
        in_specs=[
            pl.BlockSpec((None, tq, gw), lambda b, g, i: (b, i, COL_QA // gw + g)),
            pl.BlockSpec((None, tq, 128), lambda b, g, i: (b, i, COL_SM // 128)),
            kc(0), kc(1), kva(2), kva(3), kva(4), kva(5),
        ],
        out_specs=pl.BlockSpec((None, tq, gw), lambda b, g, i: (b, i, g)),
        out_shape=jax.ShapeDtypeStruct((B, T, NSA_HEADS * HEAD_DIM), f32),
        scratch_shapes=[
            pltpu.VMEM((nck, tq, ATT_KC), f32),
            pltpu.VMEM((NSA_HPG * tq, 1), f32), pltpu.VMEM((NSA_HPG * tq, 1), f32),
            pltpu.VMEM((NSA_HPG * tq, HEAD_DIM), f32),
        ],
        compiler_params=pltpu.CompilerParams(
            dimension_semantics=("parallel", "parallel", "arbitrary"), vmem_limit_bytes=VMEM_LIMIT),
        name="nsa_prompt",
    )(u, u, kcv, kcv, u, u, u, u)


SMP_ROWS = 8
SMP_NPG = 8


def _pad_rows(x, n):
    return jnp.concatenate([x, jnp.zeros((n - x.shape[0], x.shape[1]), x.dtype)], axis=0)


def _dsa_sample_kernel(pt_ref, us_ref, *refs, n_steps, past, n_keep, idx_bits):
    npg = SMP_NPG
    idx_refs, kv_refs, o_ref = refs[:npg], refs[npg:2 * npg], refs[2 * npg]
    key_scr, bias_scr, cut_scr, k_scr, v_scr, m_scr, l_scr, acc_scr = refs[2 * npg + 1:]
    rows = SMP_ROWS
    cw = npg * PAGE_SIZE
    s = pl.program_id(1)
    us = us_ref[...]
    wi = us[:, COL_SM + SM_WI:COL_SM + SM_WI + IDX_HEADS] * (IDX_HEADS ** -0.5)
    qi_h = [us[:, COL_QI + h * IDX_DIM:COL_QI + (h + 1) * IDX_DIM].astype(bf16) for h in range(IDX_HEADS)]

    ki_t = jnp.concatenate([idx_refs[j][...] for j in range(npg)], axis=1).astype(bf16)
    key_scr[s] = _order_key(_index_scores(qi_h, wi, ki_t, keys_on_lanes=True))
    for j in range(npg):
        r0 = pl.multiple_of((s * npg + j) * PAGE_SIZE, PAGE_SIZE)
        for g in range(DSA_KV_HEADS):
            per_pos = 2 * DSA_KV_HEADS
            k_scr[g, pl.ds(r0, PAGE_SIZE), :] = kv_refs[j][pl.ds(g, PAGE_SIZE, stride=per_pos), :].astype(bf16)
            v_scr[g, pl.ds(r0, PAGE_SIZE), :] = kv_refs[j][pl.ds(DSA_KV_HEADS + g, PAGE_SIZE, stride=per_pos), :].astype(bf16)

    @pl.when(s == n_steps - 1)
    def _():
        row = lax.broadcasted_iota(jnp.int32, (rows, cw), 0)
        lane = lax.broadcasted_iota(jnp.int32, (rows, cw), 1)
        ki_new = _pad_rows(us[:, COL_SM + SM_KI:COL_SM + SM_KI + IDX_DIM], cw).astype(bf16)
        key_scr[n_steps] = jnp.where(lane <= row, _order_key(_index_scores(qi_h, wi, ki_new)), INT_MIN)
        for g in range(DSA_KV_HEADS):
            kcol = COL_KVB + g * HEAD_DIM
            vcol = COL_KVB + (DSA_KV_HEADS + g) * HEAD_DIM
            k_scr[g, pl.ds(past, cw), :] = _pad_rows(us[:, kcol:kcol + HEAD_DIM], cw).astype(bf16)
            v_scr[g, pl.ds(past, cw), :] = _pad_rows(us[:, vcol:vcol + HEAD_DIM], cw).astype(bf16)
        _topk_bias(key_scr, bias_scr, cut_scr, n_steps + 1, past + row, n_keep, idx_bits)
        for g in range(DSA_KV_HEADS):
            qg = _stack_heads(us[:, COL_QB:COL_KVA], g * DSA_HPG, DSA_HPG, HEAD_DIM ** -0.5)
            o = _flash_chunks(qg, k_scr.at[g], v_scr.at[g], bias_scr, n_steps + 1, m_scr, l_scr, acc_scr, DSA_HPG)
            for h in range(DSA_HPG):
                c0 = (g * DSA_HPG + h) * HEAD_DIM
                o_ref[:, c0:c0 + HEAD_DIM] = o[h * rows:(h + 1) * rows]


def _page_specs(rows, width, layer):
    return [pl.BlockSpec((None, None, rows, width),
                         lambda b, s, pt, j=j: (layer, pt[b, s * SMP_NPG + j], 0, 0)) for j in range(SMP_NPG)]


def _page_rows_view(pool):
    return pool.reshape(pool.shape[:2] + (-1, pool.shape[-1]))


def _dsa_sample(us, idx_pool, kv_pool, page_table, layer, t_new):
    B = us.shape[0]
    rows = SMP_ROWS
    n_pages = page_table.shape[1]
    past = n_pages * PAGE_SIZE
    n_steps = n_pages // SMP_NPG
    cw = SMP_NPG * PAGE_SIZE
    width = DSA_HEADS * HEAD_DIM
    grid_spec = pltpu.PrefetchScalarGridSpec(
        num_scalar_prefetch=1,
        grid=(B, n_steps),
        in_specs=([pl.BlockSpec((None, rows, us.shape[-1]), lambda b, s, pt: (b, 0, 0))]
                  + _page_specs(IDX_DIM, PAGE_SIZE, layer)
                  + _page_specs(PAGE_SIZE * 2 * DSA_KV_HEADS, HEAD_DIM, layer)),
        out_specs=pl.BlockSpec((None, rows, width), lambda b, s, pt: (b, 0, 0)),
        scratch_shapes=[
            pltpu.VMEM((n_steps + 1, rows, cw), jnp.int32), pltpu.VMEM((n_steps + 1, rows, cw), f32),
            pltpu.VMEM((rows, 1), jnp.int32),
            pltpu.VMEM((DSA_KV_HEADS, past + cw, HEAD_DIM), bf16), pltpu.VMEM((DSA_KV_HEADS, past + cw, HEAD_DIM), bf16),
            pltpu.VMEM((DSA_HPG * rows, 1), f32), pltpu.VMEM((DSA_HPG * rows, 1), f32),
            pltpu.VMEM((DSA_HPG * rows, HEAD_DIM), f32),
        ])
    return pl.pallas_call(
        functools.partial(_dsa_sample_kernel, n_steps=n_steps, past=past,
                          n_keep=min(DSA_TOPK, (past + t_new) // 4), idx_bits=(past + cw - 1).bit_length()),
        grid_spec=grid_spec,
        out_shape=jax.ShapeDtypeStruct((B, rows, width), f32),
        compiler_params=pltpu.CompilerParams(
            dimension_semantics=("parallel", "arbitrary"), vmem_limit_bytes=VMEM_LIMIT),
        name="dsa_sample",
    )(page_table, us, *([idx_pool] * SMP_NPG), *([kv_pool] * SMP_NPG))


def _nsa_sample_kernel(pt_ref, us_ref, cw_ref, win_ref, *refs, n_steps, past, t_new):
    npg = SMP_NPG
    pages, o_ref = refs[:npg], refs[npg]
    kc_scr, vc_scr, ks_scr, vs_scr, bias_scr, m_scr, l_scr, acc_scr = refs[npg + 1:]
    rows = SMP_ROWS
    hpg = NSA_HPG
    G = NSA_KV_HEADS
    cw = npg * PAGE_SIZE
    bpp = PAGE_SIZE // NSA_BLOCK
    bpc = cw // NSA_BLOCK
    s = pl.program_id(1)
    w = cw_ref[...]

    @pl.when(s == 0)
    def _():
        kc_scr[...] = jnp.zeros_like(kc_scr)
        vc_scr[...] = jnp.zeros_like(vc_scr)

    def page_rows(j, r, g):
        return pages[j][pl.ds(r * G + g, PAGE_SIZE, stride=4 * G), :]

    for g in range(G):
        kcs, vcs = [], []
        for j in range(npg):
            r0 = pl.multiple_of((s * npg + j) * PAGE_SIZE, PAGE_SIZE)
            kcs.append(jnp.sum(page_rows(j, 0, g).reshape(bpp, NSA_BLOCK, HEAD_DIM) * w[0][None], axis=1))
            vcs.append(jnp.sum(page_rows(j, 1, g).reshape(bpp, NSA_BLOCK, HEAD_DIM) * w[1][None], axis=1))
            ks_scr[g, pl.ds(r0, PAGE_SIZE), :] = page_rows(j, 2, g).astype(bf16)
            vs_scr[g, pl.ds(r0, PAGE_SIZE), :] = page_rows(j, 3, g).astype(bf16)
        b0 = pl.multiple_of(s * bpc, bpc)
        kc_scr[g, pl.ds(b0, bpc), :] = jnp.concatenate(kcs, axis=0)
        vc_scr[g, pl.ds(b0, bpc), :] = jnp.concatenate(vcs, axis=0)

    @pl.when(s == n_steps - 1)
    def _():
        us = us_ref[...]
        win = win_ref[...]
        wk = win.shape[0]
        span = _round_up(wk + rows, 128)
        valid = lax.broadcasted_iota(jnp.int32, (rows, HEAD_DIM), 0) < t_new
        gates = jax.nn.sigmoid(us[:, COL_SM + SM_GA:COL_SM + SM_GA + 3 * NSA_HEADS])
        for g in range(G):
            def new(r):
                c0 = COL_KVA + (r * G + g) * HEAD_DIM
                return us[:, c0:c0 + HEAD_DIM]
            kc_new = jnp.sum(jnp.where(valid, new(0) * w[0][0:rows], 0.0), axis=0, keepdims=True)
            vc_new = jnp.sum(jnp.where(valid, new(1) * w[1][0:rows], 0.0), axis=0, keepdims=True)
            kc_scr[g, pl.ds(n_steps * bpc, rows), :] = _pad_rows(kc_new, rows)
            vc_scr[g, pl.ds(n_steps * bpc, rows), :] = _pad_rows(vc_new, rows)
            ks_scr[g, pl.ds(past, cw), :] = _pad_rows(new(2), cw).astype(bf16)
            vs_scr[g, pl.ds(past, cw), :] = _pad_rows(new(3), cw).astype(bf16)
            qg = _stack_heads(us[:, COL_QA:COL_QB], g * hpg, hpg, HEAD_DIM ** -0.5)
            o_cmp, sel = _nsa_compressed(qg, kc_scr[g], vc_scr[g], past, rows, past // NSA_BLOCK + 1)
            _nsa_selection_bias(sel, bias_scr, n_steps + 1, past)
            o_slc = _flash_chunks(qg, ks_scr.at[g], vs_scr.at[g], bias_scr, n_steps + 1, m_scr, l_scr, acc_scr, hpg)
            kw = _pad_rows(jnp.concatenate([win[:, g * HEAD_DIM:(g + 1) * HEAD_DIM], new(4)], axis=0), span)
            vw = _pad_rows(jnp.concatenate([win[:, (G + g) * HEAD_DIM:(G + g + 1) * HEAD_DIM], new(5)], axis=0), span)
            o_win = _nsa_window(qg, kw.astype(bf16), vw.astype(bf16), past, past - wk, rows)
            for h in range(hpg):
                c = (g * hpg + h) * 3
                rws = slice(h * rows, (h + 1) * rows)
                o_ref[:, (g * hpg + h) * HEAD_DIM:(g * hpg + h + 1) * HEAD_DIM] = (
                    gates[:, c:c + 1] * o_cmp[rws] + gates[:, c + 1:c + 2] * o_slc[rws] + gates[:, c + 2:c + 3] * o_win[rws])


def _nsa_sample(us, cmp_w, win_buf, nsa_pool, page_table, layer, t_new):
    B = us.shape[0]
    rows = SMP_ROWS
    n_pages = page_table.shape[1]
    past = n_pages * PAGE_SIZE
    n_steps = n_pages // SMP_NPG
    cw = SMP_NPG * PAGE_SIZE
    nbp = _round_up((n_steps + 1) * (cw // NSA_BLOCK), 128)
    wk = win_buf.shape[2]
    width = NSA_HEADS * HEAD_DIM
    grid_spec = pltpu.PrefetchScalarGridSpec(
        num_scalar_prefetch=1,
        grid=(B, n_steps),
        in_specs=([pl.BlockSpec((None, rows, us.shape[-1]), lambda b, s, pt: (b, 0, 0)),
                   pl.BlockSpec((2, NSA_BLOCK, HEAD_DIM), lambda b, s, pt: (0, 0, 0)),
                   pl.BlockSpec((None, None, wk, win_buf.shape[-1]), lambda b, s, pt: (layer, b, 0, 0))]
                  + _page_specs(PAGE_SIZE * 4 * NSA_KV_HEADS, HEAD_DIM, layer)),
        out_specs=pl.BlockSpec((None, rows, width), lambda b, s, pt: (b, 0, 0)),
        scratch_shapes=[
            pltpu.VMEM((NSA_KV_HEADS, nbp, HEAD_DIM), f32), pltpu.VMEM((NSA_KV_HEADS, nbp, HEAD_DIM), f32),
            pltpu.VMEM((NSA_KV_HEADS, past + cw, HEAD_DIM), bf16), pltpu.VMEM((NSA_KV_HEADS, past + cw, HEAD_DIM), bf16),
            pltpu.VMEM((n_steps + 1, rows, cw), f32),
            pltpu.VMEM((NSA_HPG * rows, 1), f32), pltpu.VMEM((NSA_HPG * rows, 1), f32),
            pltpu.VMEM((NSA_HPG * rows, HEAD_DIM), f32),
        ])
    return pl.pallas_call(
        functools.partial(_nsa_sample_kernel, n_steps=n_steps, past=past, t_new=t_new),
        grid_spec=grid_spec,
        out_shape=jax.ShapeDtypeStruct((B, rows, width), f32),
        compiler_params=pltpu.CompilerParams(
            dimension_semantics=("parallel", "arbitrary"), vmem_limit_bytes=VMEM_LIMIT),
        name="nsa_sample",
    )(page_table, us, cmp_w, win_buf, *([nsa_pool] * SMP_NPG))


REC_CHUNK = 64
REC_TC = 512
CONV_PAD = 8


def _dot_b(a, b):
    return jnp.dot(a.astype(bf16), b.astype(bf16), preferred_element_type=f32)


def _dot_nt_b(a, b):
    return _dot_nt(a.astype(bf16), b.astype(bf16))


def _dot_tn_b(a, b):
    return _dot_b(a.T, b)


def _dot_3(a, b):
    ah, bh = a.astype(bf16), b.astype(bf16)
    al, bl = (a - ah.astype(f32)).astype(bf16), (b - bh.astype(f32)).astype(bf16)
    return (jnp.dot(ah, bh, preferred_element_type=f32) + jnp.dot(ah, bl, preferred_element_type=f32)
            + jnp.dot(al, bh, preferred_element_type=f32))


def _softplus(x):
    return jnp.maximum(x, 0.0) + jnp.log(1.0 + jnp.exp(-jnp.abs(x)))


def _silu(x):
    return x * jax.nn.sigmoid(x)


def _conv_block(xbuf_scr, raw, w, first, conv0):
    tc = raw.shape[0]
    nh = CONV_WIDTH - 1

    @pl.when(first)
    def _():
        xbuf_scr[CONV_PAD - nh:CONV_PAD, :] = conv0

    xbuf_scr[CONV_PAD:CONV_PAD + tc, :] = raw
    y = xbuf_scr[pl.ds(CONV_PAD - nh, tc), :] * w[0:1, :]
    for j in range(1, CONV_WIDTH):
        y = y + xbuf_scr[pl.ds(CONV_PAD - nh + j, tc), :] * w[j:j + 1, :]
    xbuf_scr[CONV_PAD - nh:CONV_PAD, :] = raw[tc - nh:tc, :]
    return y


def _lane_pick(x, idx):
    lane = lax.broadcasted_iota(jnp.int32, x.shape, 1)
    return jnp.sum(jnp.where(lane == idx, x, 0.0), axis=-1, keepdims=True)


def _chunk_cumsum(col, row):
    c = col.shape[0]
    i = lax.broadcasted_iota(jnp.int32, (c, c), 0)
    j = lax.broadcasted_iota(jnp.int32, (c, c), 1)
    ccol = jnp.sum(jnp.where(j <= i, jnp.broadcast_to(row, (c, c)), 0.0), axis=1, keepdims=True)
    crow = jnp.sum(jnp.where(i <= j, jnp.broadcast_to(col, (c, c)), 0.0), axis=0, keepdims=True)
    return ccol, crow


def _gdn_kernel(alog_ref, dtb_ref, q_ref, k_ref, v_ref, z_ref, sm_ref, smt_ref, wq_ref, wk_ref, wv_ref,
                cq_ref, ck_ref, cv_ref, s0_ref, ng_ref, o_ref, s_out_ref, xbuf_scr, s_scr, *, t_valid):
    tc = q_ref.shape[0]
    C = REC_CHUNK
    nc = tc // C
    hp = pl.program_id(1)
    blk = pl.program_id(2)
    first = blk == 0

    @pl.when(first)
    def _():
        s_scr[...] = s0_ref[...]

    raw = jnp.concatenate([q_ref[...], k_ref[...], v_ref[...]], axis=1)
    w = jnp.concatenate([wq_ref[...], wk_ref[...], wv_ref[...]], axis=1)
    conv0 = jnp.concatenate([cq_ref[...], ck_ref[...], cv_ref[...]], axis=1)
    y = _silu(_conv_block(xbuf_scr, raw, w, first, conv0))
    valid_col = (blk * tc + lax.broadcasted_iota(jnp.int32, (tc, 1), 0)) < t_valid
    valid_row = (blk * tc + lax.broadcasted_iota(jnp.int32, (1, tc), 1)) < t_valid
    sm = sm_ref[...]
    ii = lax.broadcasted_iota(jnp.int32, (C, C), 0)
    jj = lax.broadcasted_iota(jnp.int32, (C, C), 1)
    eye = jnp.where(ii == jj, 1.0, 0.0)

    pieces = []
    for e in range(2):
        h = 2 * hp + e
        qf = y[:, e * GDN_DK:(e + 1) * GDN_DK]
        kf = y[:, (2 + e) * GDN_DK:(3 + e) * GDN_DK]
        vf = y[:, (4 + e) * GDN_DK:(5 + e) * GDN_DK]
        q = jnp.where(valid_col, qf * lax.rsqrt(jnp.sum(qf * qf, axis=-1, keepdims=True) + 1e-6) * (GDN_DK ** -0.5), 0.0)
        k = jnp.where(valid_col, kf * lax.rsqrt(jnp.sum(kf * kf, axis=-1, keepdims=True) + 1e-6), 0.0)
        v = jnp.where(valid_col, vf, 0.0)
        a_scale = -jnp.exp(jnp.full((1, 1), alog_ref[h], f32))
        dtb = dtb_ref[h]
        beta = jnp.where(valid_col, jax.nn.sigmoid(_lane_pick(sm, OSM_BETA + h)), 0.0)
        g_col = jnp.where(valid_col, a_scale * _softplus(_lane_pick(sm, OSM_A + h) + dtb), 0.0)
        g_row = jnp.where(valid_row, a_scale * _softplus(smt_ref[pl.ds(OSM_A + h, 1), :] + dtb), 0.0)
        for c in range(nc):
            r = slice(c * C, (c + 1) * C)
            gc_col, gc_row = _chunk_cumsum(g_col[r], g_row[:, r])
            gam = jnp.where(jj <= ii, jnp.exp(jnp.where(jj <= ii, gc_col - gc_row, 0.0)), 0.0)
            kb = k[r] * beta[r]
            pieces.append(dict(q=q[r], k=k[r], vb=v[r] * beta[r], kb=kb, gc=gc_col, gam=gam,
                               x=jnp.where(jj < ii, -(_dot_nt_b(kb, k[r]) * gam), 0.0)))

    xs = [p['x'] for p in pieces]
    tms = [eye + x for x in xs]
    for _ in range(max(1, (C - 1).bit_length()) - 1):
        xs = [_dot_3(x, x) for x in xs]
        tms = [tm + _dot_3(tm, x) for tm, x in zip(tms, xs)]
    for p, tm in zip(pieces, tms):
        p['u0'] = _dot_b(tm, p['vb'])
        p['wd'] = _dot_b(tm, p['kb'] * jnp.exp(p['gc']))
        p['qk'] = _dot_nt_b(p['q'], p['k']) * p['gam']

    S = [s_scr[0], s_scr[1]]
    outs = [[], []]
    for c in range(nc):
        for e in range(2):
            p = pieces[e * nc + c]
            g_last = p['gc'][C - 1:C, :]
            u = p['u0'] - _dot_b(p['wd'], S[e])
            outs[e].append(_dot_b(p['q'] * jnp.exp(p['gc']), S[e]) + _dot_b(p['qk'], u))
            S[e] = S[e] * jnp.exp(g_last) + _dot_tn_b(p['k'] * jnp.exp(g_last - p['gc']), u)
    z = z_ref[...]
    halves = []
    for e in range(2):
        s_scr[e] = S[e]
        o = jnp.concatenate(outs[e], axis=0) if nc > 1 else outs[e][0]
        o = o * lax.rsqrt(jnp.mean(o * o, axis=-1, keepdims=True) + NORM_EPS) * ng_ref[...]
        halves.append(o * _silu(z[:, e * GDN_DV:(e + 1) * GDN_DV]))
    o_ref[...] = jnp.concatenate(halves, axis=1)

    @pl.when(blk == pl.num_programs(2) - 1)
    def _():
        s_out_ref[...] = s_scr[...]


def _smem_spec():
    return pl.BlockSpec(memory_space=pltpu.SMEM)


def _gdn(u, smt, conv_w, conv0, s0, a_log, dt_bias, norm_g, layer, t_valid):
    B, Tp, _ = u.shape
    tc = min(REC_TC, Tp)
    H = GDN_HEADS
    HP = H // 2
    pw = 2 * GDN_DK
    qkv = lambda part: pl.BlockSpec((None, tc, pw), lambda b, hp, i: (b, i, OC_QKV // pw + part * HP + hp))
    cw = lambda part: pl.BlockSpec((None, CONV_WIDTH, pw), lambda b, hp, i: (layer, 0, part * HP + hp))
    c0 = lambda part: pl.BlockSpec((None, CONV_WIDTH - 1, pw), lambda b, hp, i: (b, 0, part * HP + hp))
    return pl.pallas_call(
        functools.partial(_gdn_kernel, t_valid=t_valid),
        grid=(B, HP, Tp // tc),
        in_specs=[
            _smem_spec(), _smem_spec(),
            qkv(0), qkv(1), qkv(2),
            pl.BlockSpec((None, tc, pw), lambda b, hp, i: (b, i, OC_ZC // pw + hp)),
            pl.BlockSpec((None, tc, 128), lambda b, hp, i: (b, i, OC_SM // 128)),
            pl.BlockSpec((None, 128, tc), lambda b, hp, i: (b, 0, i)),
            cw(0), cw(1), cw(2), c0(0), c0(1), c0(2),
            pl.BlockSpec((None, 2, GDN_DK, GDN_DV), lambda b, hp, i: (b, hp, 0, 0)),
            pl.BlockSpec((1, GDN_DV), lambda b, hp, i: (0, 0)),
        ],
        out_specs=[
            pl.BlockSpec((None, tc, pw), lambda b, hp, i: (b, i, hp)),
            pl.BlockSpec((None, 2, GDN_DK, GDN_DV), lambda b, hp, i: (b, hp, 0, 0)),
        ],
        out_shape=[jax.ShapeDtypeStruct((B, Tp, H * GDN_DV), f32),
                   jax.ShapeDtypeStruct((B, H, GDN_DK, GDN_DV), f32)],
        scratch_shapes=[pltpu.VMEM((CONV_PAD + tc, 3 * pw), f32), pltpu.VMEM((2, GDN_DK, GDN_DV), f32)],
        compiler_params=pltpu.CompilerParams(
            dimension_semantics=("parallel", "parallel", "arbitrary"), vmem_limit_bytes=VMEM_LIMIT),
        name="gdn_scan",
    )(a_log, dt_bias, u, u, u, u, u, smt, conv_w, conv_w, conv_w, conv0, conv0, conv0, s0, norm_g.reshape(1, -1))


def _ssd_kernel(alog_ref, dtb_ref, d_ref, x_ref, b_ref, c_ref, z_ref, sm_ref, smt_ref, wx_ref, wb_ref, wc_ref,
                bx_ref, bb_ref, bc_ref, cx_ref, cb_ref, cc_ref, h0_ref, o_ref, h_out_ref, xbuf_scr, h_scr,
                *, t_valid):
    tc = x_ref.shape[0]
    C = REC_CHUNK
    P = SSD_HEAD_DIM
    hp = pl.program_id(1)
    blk = pl.program_id(2)
    first = blk == 0

    @pl.when(first)
    def _():
        h_scr[...] = h0_ref[...]

    raw = jnp.concatenate([x_ref[...], b_ref[...], c_ref[...]], axis=1)
    w = jnp.concatenate([wx_ref[...], wb_ref[...], wc_ref[...]], axis=1)
    bias = jnp.concatenate([bx_ref[...], bb_ref[...], bc_ref[...]], axis=1)
    conv0 = jnp.concatenate([cx_ref[...], cb_ref[...], cc_ref[...]], axis=1)
    y = _silu(_conv_block(xbuf_scr, raw, w, first, conv0) + bias)
    valid_col = (blk * tc + lax.broadcasted_iota(jnp.int32, (tc, 1), 0)) < t_valid
    valid_row = (blk * tc + lax.broadcasted_iota(jnp.int32, (1, tc), 1)) < t_valid
    xs = jnp.where(valid_col, y[:, 0:128], 0.0)
    bm = jnp.where(valid_col, y[:, 128:256], 0.0)
    cm = jnp.where(valid_col, y[:, 256:384], 0.0)
    sm = sm_ref[...]
    z = z_ref[...]
    ii = lax.broadcasted_iota(jnp.int32, (C, C), 0)
    jj = lax.broadcasted_iota(jnp.int32, (C, C), 1)
    cbs = [_dot_nt_b(cm[c * C:(c + 1) * C], bm[c * C:(c + 1) * C]) for c in range(tc // C)]
    halves = []
    for e in range(2):
        hh = 2 * hp + e
        a_neg = -jnp.exp(jnp.full((1, 1), alog_ref[hh], f32))
        dtb = dtb_ref[hh]
        dt_col = jnp.where(valid_col, _softplus(_lane_pick(sm, OSM_DT + hh) + dtb), 0.0)
        dt_row = jnp.where(valid_row, _softplus(smt_ref[pl.ds(OSM_DT + hh, 1), :] + dtb), 0.0)
        xh = xs[:, e * P:(e + 1) * P]
        xdt = xh * dt_col
        hst = h_scr[e]
        outs = []
        for c in range(tc // C):
            r = slice(c * C, (c + 1) * C)
            acs_col, acs_row = _chunk_cumsum(dt_col[r] * a_neg, dt_row[:, r] * a_neg)
            lm = jnp.where(jj <= ii, jnp.exp(jnp.where(jj <= ii, acs_col - acs_row, 0.0)), 0.0)
            a_last = acs_col[C - 1:C, :]
            y_diag = _dot_b(cbs[c] * lm, xdt[r])
            y_off = _dot_nt_b(cm[r] * jnp.exp(acs_col), hst)
            outs.append(y_diag + y_off)
            hst = hst * jnp.exp(a_last) + _dot_tn_b(xdt[r] * jnp.exp(a_last - acs_col), bm[r])
        h_scr[e] = hst
        yh = jnp.concatenate(outs, axis=0) if len(outs) > 1 else outs[0]
        halves.append((yh + d_ref[hh] * xh) * _silu(z[:, e * P:(e + 1) * P]))
    o_ref[...] = jnp.concatenate(halves, axis=1)

    @pl.when(blk == pl.num_programs(2) - 1)
    def _():
        h_out_ref[...] = h_scr[...]


def _ssd(u, smt, conv_w, conv_b, conv0, h0, a_log, dt_bias, d_skip, layer, t_valid):
    B, Tp, _ = u.shape
    tc = min(REC_TC, Tp)
    HP = SSD_HEADS // 2
    hpg = SSD_HEADS // SSD_GROUPS // 2
    xcol = lambda hp: hp
    bcol = lambda hp: SSD_D_INNER // 128 + hp // hpg
    ccol = lambda hp: (SSD_D_INNER + SSD_GROUPS * SSD_STATE) // 128 + hp // hpg
    def tri(fn_col):
        return (pl.BlockSpec((None, tc, 128), lambda b, hp, i: (b, i, OC_XBC // 128 + fn_col(hp))),
                pl.BlockSpec((None, CONV_WIDTH, 128), lambda b, hp, i: (layer, 0, fn_col(hp))),
                pl.BlockSpec((None, 1, 128), lambda b, hp, i: (layer, 0, fn_col(hp))),
                pl.BlockSpec((None, CONV_WIDTH - 1, 128), lambda b, hp, i: (b, 0, fn_col(hp))))
    (xs, wx, bx, cx), (bs, wb, bb, cb), (cs, wc, bc, cc) = tri(xcol), tri(bcol), tri(ccol)
    return pl.pallas_call(
        functools.partial(_ssd_kernel, t_valid=t_valid),
        grid=(B, HP, Tp // tc),
        in_specs=[
            _smem_spec(), _smem_spec(), _smem_spec(),
            xs, bs, cs,
            pl.BlockSpec((None, tc, 128), lambda b, hp, i: (b, i, OC_ZD // 128 + hp)),
            pl.BlockSpec((None, tc, 128), lambda b, hp, i: (b, i, OC_SM // 128)),
            pl.BlockSpec((None, 128, tc), lambda b, hp, i: (b, 0, i)),
            wx, wb, wc, bx, bb, bc, cx, cb, cc,
            pl.BlockSpec((None, 2, SSD_HEAD_DIM, SSD_STATE), lambda b, hp, i: (b, hp, 0, 0)),
        ],
        out_specs=[
            pl.BlockSpec((None, tc, 128), lambda b, hp, i: (b, i, hp)),
            pl.BlockSpec((None, 2, SSD_HEAD_DIM, SSD_STATE), lambda b, hp, i: (b, hp, 0, 0)),
        ],
        out_shape=[jax.ShapeDtypeStruct((B, Tp, SSD_D_INNER), f32),
                   jax.ShapeDtypeStruct((B, SSD_HEADS, SSD_HEAD_DIM, SSD_STATE), f32)],
        scratch_shapes=[pltpu.VMEM((CONV_PAD + tc, 3 * 128), f32), pltpu.VMEM((2, SSD_HEAD_DIM, SSD_STATE), f32)],
        compiler_params=pltpu.CompilerParams(
            dimension_semantics=("parallel", "parallel", "arbitrary"), vmem_limit_bytes=VMEM_LIMIT),
        name="ssd_scan",
    )(a_log, dt_bias, d_skip, u, u, u, u, u, smt, conv_w, conv_w, conv_w, conv_b, conv_b, conv_b,
      conv0, conv0, conv0, h0)


def _split(u, sizes):
    return jnp.split(u, np.cumsum(sizes)[:-1].tolist(), axis=-1)


def _even_mixer(u, cmp_w, cache, i):
    B, T, _ = u.shape
    kva = u[..., COL_KVA:COL_KVB].reshape(B, T, 6, NSA_KV_HEADS, HEAD_DIM)
    kvb = u[..., COL_KVB:COL_QI].reshape(B, T, 2, DSA_KV_HEADS, HEAD_DIM)
    ki = u[..., COL_SM + SM_KI:COL_SM + SM_GA]
    nsa_rows, win_rows = kva[:, :, :4], kva[:, :, 4:]
    if cache is None:
        o_a = _nsa_prompt(u, cmp_w)
        o_b = _dsa_prompt(u, min(DSA_TOPK, T // 4))
        return (o_a, o_b), (nsa_rows, kvb, ki, win_rows[:, T - min(NSA_WINDOW, T):])
    flat = lambda a: a.reshape(a.shape[:3] + (-1,))
    us = jnp.pad(u, ((0, 0), (0, SMP_ROWS - T), (0, 0)))
    pt = cache['page_table']
    o_a = _nsa_sample(us, cmp_w, flat(cache['nsa_win']), _page_rows_view(cache['nsa_kv']), pt, i, T)[:, :T]
    idx_t = jnp.swapaxes(cache['dsa_idx_k'], 2, 3)
    o_b = _dsa_sample(us, idx_t, _page_rows_view(cache['dsa_kv']), pt, i, T)[:, :T]
    new_win = jnp.concatenate([cache['nsa_win'][i], win_rows], axis=1)[:, T:]
    return (o_a, o_b), (nsa_rows, kvb, ki, new_win)


def _odd_mixer(u, W, i, init, t_valid):
    S0, conv_c0, h0, conv_d0 = init
    nr = W['ssd_conv_b'].shape[0]
    smt = jnp.swapaxes(u[..., OC_SM:OC_SM + 128], 1, 2)
    o_c, S = _gdn(u, smt, W['gdn_conv_w'], conv_c0, S0, W['gdn_a_log'][i], W['gdn_dt_bias'][i],
                  W['gdn_norm_g'][i], i, t_valid)
    y, hN = _ssd(u, smt, W['ssd_conv_w'], W['ssd_conv_b'].reshape(nr, 1, -1), conv_d0, h0, W['ssd_a_log'][i],
                 W['ssd_dt_bias'][i], W['ssd_d'][i], i, t_valid)

    def conv_state(buf, lo, width):
        rows = u[:, :t_valid, lo:lo + width]
        nh = CONV_WIDTH - 1
        return rows[:, t_valid - nh:] if t_valid >= nh else jnp.concatenate([buf, rows], axis=1)[:, -nh:]

    return (o_c[:, :t_valid], y[:, :t_valid]), (S, conv_state(conv_c0, OC_QKV, GDN_CONV_DIM), hN,
                                               conv_state(conv_d0, OC_XBC, SSD_CONV_DIM))


def _trunk(x, W, cache):
    B, T, D = x.shape
    x = x.reshape(B * T, D)
    attn_new, rec_new = [], []
    for l in range(DEPTH):
        i = l // 2
        x = _ffn(x, W['norm_g'][l, 0], W['ffn_wg'], W['ffn_wu'], W['ffn_wd'], l, 0)
        if l % 2 == 0:
            u = _inproj(x, W['norm_g'][l, 1], W['attn_w_in'], i).reshape(B, T, -1)
            (o0, o1), st = _even_mixer(u, W['nsa_cmp_w'][i], cache, i)
            attn_new.append(st)
            x = _outproj(x, o0.reshape(B * T, -1), o1.reshape(B * T, -1), W['attn_w_out'], i)
        else:
            u = _inproj(x, W['norm_g'][l, 1], W['rec_w_in'], i).reshape(B, T, -1)
            u = jnp.pad(u, ((0, 0), (0, _round_up(T, REC_CHUNK) - T), (0, 0)))
            if cache is None:
                init = (jnp.zeros((B, GDN_HEADS, GDN_DK, GDN_DV), f32),
                        jnp.zeros((B, CONV_WIDTH - 1, GDN_CONV_DIM), f32),
                        jnp.zeros((B, SSD_HEADS, SSD_HEAD_DIM, SSD_STATE), f32),
                        jnp.zeros((B, CONV_WIDTH - 1, SSD_CONV_DIM), f32))
            else:
                init = (cache['gdn'][i], cache['gdn_conv'][i], cache['ssd'][i], cache['ssd_conv'][i])
            (o0, o1), st = _odd_mixer(u, W, i, init, T)
            rec_new.append(st)
            x = _outproj(x, o0.reshape(B * T, -1), o1.reshape(B * T, -1), W['rec_w_out'], i,
                         g1=W['ssd_norm_g'][i], norm_groups=SSD_GROUPS)
        x = _ffn(x, W['norm_g'][l, 2], W['ffn_wg'], W['ffn_wu'], W['ffn_wd'], l, 1,
                 final_g=W['final_norm_g'] if l == DEPTH - 1 else None)
    y = x.reshape(B, T, D)

    def stack(lst, j):
        return jnp.stack([s[j] for s in lst])

    return (y, stack(attn_new, 0), stack(attn_new, 1), stack(attn_new, 2), stack(attn_new, 3),
            stack(rec_new, 0), stack(rec_new, 1), stack(rec_new, 2), stack(rec_new, 3))


def _pad_cols(w, n):
    return jnp.pad(w, ((0, 0), (0, 0), (0, n - w.shape[-1])))


def _pack_even_w(w):
    qa, kva, ga, qb, kvb, qi, ki, wi = _split(w, EVEN_SIZES)
    return _pad_cols(jnp.concatenate([qa, qb, kva, kvb, qi, ki, ga, wi], axis=-1), _round_up(EVEN_IN, PROJ_TN))


def _pack_odd_w(w):
    qkv, beta, a, zc, zd, xbc, dt = _split(w, ODD_SIZES)
    return _pad_cols(jnp.concatenate([qkv, zc, zd, xbc, beta, a, dt], axis=-1), _round_up(OC_SM + 128, PROJ_TN))


def kernel(x_prompt, x_sample, cache_nsa_kv, cache_dsa_kv, cache_dsa_idx_k, page_table, state_nsa_win,
           state_gdn, state_gdn_conv, state_ssd, state_ssd_conv, norm_g, final_norm_g, ffn_w_gate, ffn_w_up,
           ffn_w_down, attn_w_in, attn_w_out, nsa_cmp_w, rec_w_in, rec_w_out, gdn_conv_w, gdn_a_log,
           gdn_dt_bias, gdn_norm_g, ssd_conv_w, ssd_conv_b, ssd_dt_bias, ssd_a_log, ssd_d, ssd_norm_g):
    W = {'norm_g': norm_g, 'final_norm_g': final_norm_g,
         'ffn_wg': ffn_w_gate.astype(bf16), 'ffn_wu': ffn_w_up.astype(bf16), 'ffn_wd': ffn_w_down.astype(bf16),
         'attn_w_in': _pack_even_w(attn_w_in).astype(bf16),
         'attn_w_out': attn_w_out.astype(bf16), 'nsa_cmp_w': nsa_cmp_w,
         'rec_w_in': _pack_odd_w(rec_w_in).astype(bf16),
         'rec_w_out': rec_w_out.astype(bf16), 'gdn_conv_w': gdn_conv_w, 'gdn_a_log': gdn_a_log,
         'gdn_dt_bias': gdn_dt_bias, 'gdn_norm_g': gdn_norm_g, 'ssd_conv_w': ssd_conv_w, 'ssd_conv_b': ssd_conv_b,
         'ssd_dt_bias': ssd_dt_bias, 'ssd_a_log': ssd_a_log, 'ssd_d': ssd_d, 'ssd_norm_g': ssd_norm_g}
    cache = {'nsa_kv': cache_nsa_kv, 'dsa_kv': cache_dsa_kv, 'dsa_idx_k': cache_dsa_idx_k,
             'page_table': page_table, 'nsa_win': state_nsa_win, 'gdn': state_gdn, 'gdn_conv': state_gdn_conv,
             'ssd': state_ssd, 'ssd_conv': state_ssd_conv}
    (y_prompt, p_nsa_kv, p_dsa_kv, p_dsa_idx_k, p_nsa_win,
     p_gdn, p_gdn_conv, p_ssd, p_ssd_conv) = _trunk(x_prompt, W, None)
    (y_sample, s_nsa_kv, s_dsa_kv, s_dsa_idx_k, s_nsa_win,
     s_gdn, s_gdn_conv, s_ssd, s_ssd_conv) = _trunk(x_sample, W, cache)
    return (y_prompt, y_sample, p_nsa_kv, p_dsa_kv, p_dsa_idx_k, p_nsa_win, p_gdn, p_gdn_conv, p_ssd, p_ssd_conv,
            s_nsa_kv, s_dsa_kv, s_dsa_idx_k, s_nsa_win, s_gdn, s_gdn_conv, s_ssd, s_ssd_conv)
```
